```python
import math
import jax
import jax.numpy as jnp
from jax import lax
import numpy as np

D_MODEL = 1024
BATCH = 16
SEQ = 256
DEPTH = 4
DEC_BATCH = 4
DEC_SEQ = 2048
PAST_LEN = 256

GRID_W = 64
HEAD_DIM = 64
N_EVEN = (DEPTH + 1) // 2
N_ODD = DEPTH // 2
MIX_W = D_MODEL

HY_W = MIX_W // 2
HY_PROJ = 3 * HY_W
HY_SHORT = 3
HY_BANDS = 16
HY_EMB = 1 + 2 * HY_BANDS
HY_FF = 64
HY_DECAY_TARGET = 1e-2
HY_FAST_DECAY = 0.3
HY_SLOW_DECAY = 1.5

NA_HEADS = (MIX_W - HY_W) // HEAD_DIM
NA_W = NA_HEADS * HEAD_DIM
NA_KH = 8
NA_KW = 16
NA_QB = 16
NA_KB = NA_QB + NA_KW
EVEN_PROJ = HY_PROJ + 3 * NA_W

SWA_HEADS = MIX_W // HEAD_DIM
SWA_KV_HEADS = SWA_HEADS // 4
SWA_WINDOW = 128
SWA_BLOCK = 128
SWA_PROJ = (SWA_HEADS + 2 * SWA_KV_HEADS) * HEAD_DIM
ROPE_THETA = 10000.0

N_EXPERTS = 32
TOP_K = 4
D_FF = D_MODEL
SWIGLU_LIMIT = 7.0
SWIGLU_ALPHA = 1.702
MOE_BLOCK = 128

DN_ALPHA = (2 * DEPTH) ** 0.25
DN_BETA = (8 * DEPTH) ** -0.25
LN_EPS = 1e-5
NEG_INF = -1e30

kernel_name = 'hybrid_hyena_natten_swa_moe_diffusion_step'


def layer_norm(x, g, b):
    xf = x.astype(jnp.float32)
    mu = jnp.mean(xf, axis=-1, keepdims=True)
    var = jnp.mean(jnp.square(xf - mu), axis=-1, keepdims=True)
    return ((xf - mu) * lax.rsqrt(var + LN_EPS) * g + b).astype(x.dtype)


def adaln(cond, w, b):
    m = (jax.nn.silu(cond) @ w + b).reshape(cond.shape[:-1] + (6, D_MODEL))
    return [m[..., None, i, :] for i in range(6)]


def modulate(x, shift, scale):
    return x * (1 + scale) + shift


def post_norm(x, out, gate, g, b):
    return layer_norm(DN_ALPHA * x + gate * out, g, b)


def hyena_filter(L, w1, b1, freq, w2, b2, w3):
    n = jnp.arange(L, dtype=jnp.float32)[:, None]
    t = jnp.linspace(0.0, 1.0, L, dtype=jnp.float32)[:, None]
    bands = jnp.linspace(1e-4, HY_BANDS - 1, HY_BANDS, dtype=jnp.float32)[None, :]
    ang = (2.0 * math.pi / L) * n * bands
    z = jnp.concatenate([t, jnp.cos(ang), -jnp.sin(ang)], axis=-1)
    h = jnp.sin(freq * (z @ w1 + b1))
    h = jnp.sin(freq * (h @ w2 + b2))
    h = (h @ w3).astype(jnp.float32).reshape(L, 2, HY_W)
    deltas = jnp.abs(jnp.linspace(math.log(HY_DECAY_TARGET) / HY_SLOW_DECAY,
                                  math.log(HY_DECAY_TARGET) / HY_FAST_DECAY, HY_W, dtype=jnp.float32))
    h = h * jnp.exp(-t * deltas)[:, None, :]
    fwd, bwd = h[:, 0], h[:, 1]
    return jnp.concatenate([fwd, jnp.zeros((1, HY_W), jnp.float32), bwd[:0:-1]], axis=0)


def hyena_mixer(u, conv_w, conv_b, w1, b1, freq, w2, b2, w3, skip):
    L = u.shape[1]
    pad = HY_SHORT // 2
    u = lax.conv_general_dilated(u, conv_w, window_strides=(1,), padding=[(pad, pad)],
                                 dimension_numbers=('NWC', 'WIO', 'NWC'),
                                 feature_group_count=HY_PROJ) + conv_b
    x0, x1, v = jnp.split(u, 3, axis=-1)
    v = (v * x1).astype(jnp.float32)
    k = hyena_filter(L, w1, b1, freq, w2, b2, w3)
    n_fft = 2 * L
    y = jnp.fft.irfft(jnp.fft.rfft(v, n=n_fft, axis=1) * jnp.fft.rfft(k, n=n_fft, axis=0)[None],
                      n=n_fft, axis=1)[:, :L]
    return ((y + v * skip) * x0).astype(u.dtype)


def axial_rope(x, rows, cols):
    half = HEAD_DIM // 2
    quarter = half // 2
    inv = ROPE_THETA ** (-jnp.arange(quarter, dtype=jnp.float32) / quarter)

    def rot(xa, pos):
        ang = pos.astype(jnp.float32)[:, None] * inv
        cos, sin = jnp.cos(ang)[:, None, :], jnp.sin(ang)[:, None, :]
        a, b = xa[..., :quarter], xa[..., quarter:]
        return jnp.concatenate([a * cos - b * sin, b * cos + a * sin], axis=-1)

    xf = x.astype(jnp.float32)
    return jnp.concatenate([rot(xf[..., :half], rows), rot(xf[..., half:], cols)], axis=-1).astype(x.dtype)


def dense_context_attention(q, k, v, sink):
    B, L, H, Dh = q.shape
    G = k.shape[2]
    qg = q.reshape(B, L, G, H // G, Dh)
    s = jnp.einsum('blgrd,bcgd->bgrlc', qg, k, preferred_element_type=jnp.float32) * (Dh ** -0.5)
    if sink is not None:
        sk = jnp.broadcast_to(sink.astype(jnp.float32).reshape(G, H // G)[None, :, :, None, None], s.shape[:-1] + (1,))
        p = jax.nn.softmax(jnp.concatenate([s, sk], axis=-1), axis=-1)[..., :-1]
    else:
        p = jax.nn.softmax(s, axis=-1)
    o = jnp.einsum('bgrlc,bcgd->blgrd', p.astype(v.dtype), v)
    return o.reshape(B, L, H * Dh)


def neighbourhood_attention(q, k, v, k_ctx, v_ctx, rpb):
    B, L, H, Dh = q.shape
    R = L // GRID_W
    kh = min(NA_KH, R)
    ncb = GRID_W // NA_QB
    scale = Dh ** -0.5
    r = jnp.arange(R)
    row_idx = jnp.clip(r - kh // 2, 0, R - kh)[:, None] + jnp.arange(kh)
    qcol = jnp.arange(GRID_W).reshape(ncb, NA_QB)
    col_idx = jnp.clip(qcol[:, 0] - NA_KW // 2, 0, GRID_W - NA_KB)[:, None] + jnp.arange(NA_KB)
    win_lo = jnp.clip(qcol - NA_KW // 2, 0, GRID_W - NA_KW)[..., None]
    col_ok = (col_idx[:, None, :] >= win_lo) & (col_idx[:, None, :] < win_lo + NA_KW)
    rel_row = row_idx - r[:, None]
    rel_col = jnp.clip(col_idx[:, None, :] - qcol[:, :, None], 1 - NA_KW, NA_KW - 1)
    bias = rpb[:, rel_row[:, None, None, :, None] + NA_KH - 1,
               rel_col[None, :, :, None, :] + NA_KW - 1].astype(jnp.float32)

    def to_grid(t):
        return t.reshape(B, R, GRID_W, H, Dh).transpose(0, 3, 1, 2, 4)

    qg = to_grid(q).reshape(B, H, R, ncb, NA_QB, Dh)
    kg = to_grid(k)[:, :, row_idx][:, :, :, :, col_idx]
    vg = to_grid(v)[:, :, row_idx][:, :, :, :, col_idx]
    s_loc = jnp.einsum('bhrcqd,bhrkcmd->bhrcqkm', qg, kg, preferred_element_type=jnp.float32) * scale + bias
    s_loc = jnp.where(col_ok[:, :, None, :], s_loc, NEG_INF)
    s_ctx = jnp.einsum('bhrcqd,bnhd->bhrcqn', qg, k_ctx, preferred_element_type=jnp.float32) * scale
    n_loc = kh * NA_KB
    p = jax.nn.softmax(jnp.concatenate([s_loc.reshape(B, H, R, ncb, NA_QB, n_loc), s_ctx], axis=-1),
                       axis=-1).astype(v.dtype)
    p_loc = p[..., :n_loc].reshape(B, H, R, ncb, NA_QB, kh, NA_KB)
    o = (jnp.einsum('bhrcqkm,bhrkcmd->bhrcqd', p_loc, vg)
         + jnp.einsum('bhrcqn,bnhd->bhrcqd', p[..., n_loc:], v_ctx))
    return o.reshape(B, H, R, GRID_W, Dh).transpose(0, 2, 3, 1, 4).reshape(B, L, H * Dh)


def window_attention(q, k, v, k_ctx, v_ctx, sink):
    B, L, H, Dh = q.shape
    G = k.shape[2]
    rep = H // G
    nb = L // SWA_BLOCK
    scale = Dh ** -0.5
    qb = q.reshape(B, nb, SWA_BLOCK, G, rep, Dh)

    def band(t):
        tp = jnp.pad(t, ((0, 0), (SWA_BLOCK, SWA_BLOCK), (0, 0), (0, 0))).reshape(B, nb + 2, SWA_BLOCK, G, Dh)
        return jnp.concatenate([tp[:, :-2], tp[:, 1:-1], tp[:, 2:]], axis=2)

    kb, vb = band(k), band(v)
    blk = jnp.arange(nb)[:, None]
    qpos = blk * SWA_BLOCK + jnp.arange(SWA_BLOCK)
    kpos = blk * SWA_BLOCK - SWA_BLOCK + jnp.arange(3 * SWA_BLOCK)
    ok = ((jnp.abs(qpos[:, :, None] - kpos[:, None, :]) <= SWA_WINDOW)
          & ((kpos >= 0) & (kpos < L))[:, None, :])
    s_loc = jnp.einsum('bnqgrd,bnkgd->bngrqk', qb, kb, preferred_element_type=jnp.float32) * scale
    s_loc = jnp.where(ok[None, :, None, None], s_loc, NEG_INF)
    s_ctx = jnp.einsum('bnqgrd,bcgd->bngrqc', qb, k_ctx, preferred_element_type=jnp.float32) * scale
    s_sink = jnp.broadcast_to(sink.astype(jnp.float32).reshape(G, rep)[None, None, :, :, None, None],
                              s_loc.shape[:-1] + (1,))
    n_loc = 3 * SWA_BLOCK
    n_ctx = k_ctx.shape[1]
    p = jax.nn.softmax(jnp.concatenate([s_loc, s_ctx, s_sink], axis=-1), axis=-1).astype(v.dtype)
    o = (jnp.einsum('bngrqk,bnkgd->bnqgrd', p[..., :n_loc], vb)
         + jnp.einsum('bngrqc,bcgd->bnqgrd', p[..., n_loc:n_loc + n_ctx], v_ctx))
    return o.reshape(B, L, H * Dh)


def expert_ffn(xb, gu_w, gu_b, dn_w, dn_b):
    hgu = xb @ gu_w + gu_b
    g = jnp.minimum(hgu[..., :D_FF], SWIGLU_LIMIT)
    lin = jnp.clip(hgu[..., D_FF:], -SWIGLU_LIMIT, SWIGLU_LIMIT)
    return ((lin + 1) * (g * jax.nn.sigmoid(SWIGLU_ALPHA * g))) @ dn_w + dn_b


def moe(x, router_w, router_b, gu_w, gu_b, dn_w, dn_b):
    shp = x.shape
    xt = x.reshape(-1, D_MODEL)
    T = xt.shape[0]
    logits = (xt @ router_w + router_b).astype(jnp.float32)
    top_v, top_e = lax.top_k(logits, TOP_K)
    gates = jax.nn.softmax(top_v, axis=-1)
    A = T * TOP_K
    flat_e = top_e.reshape(-1)
    order = jnp.argsort(flat_e)
    sorted_e = flat_e[order]
    sorted_tok = order // TOP_K
    counts = jnp.bincount(flat_e, length=N_EXPERTS)
    padded = (counts + MOE_BLOCK - 1) // MOE_BLOCK * MOE_BLOCK
    pad_end = jnp.cumsum(padded)
    pad_start = pad_end - padded
    grp_start = jnp.cumsum(counts) - counts
    slot = pad_start[sorted_e] + jnp.arange(A) - grp_start[sorted_e]
    n_blocks = -(-A // MOE_BLOCK) + N_EXPERTS
    slot_tok = jnp.full((n_blocks * MOE_BLOCK,), T, jnp.int32).at[slot].set(sorted_tok.astype(jnp.int32))
    x_pad = jnp.concatenate([xt, jnp.zeros((1, D_MODEL), xt.dtype)], axis=0)
    xb = x_pad[slot_tok].reshape(n_blocks, MOE_BLOCK, D_MODEL)
    block_e = jnp.minimum(jnp.searchsorted(pad_end, jnp.arange(n_blocks) * MOE_BLOCK, side='right'),
                          N_EXPERTS - 1)

    def run_block(args):
        xblk, e = args
        return expert_ffn(xblk, gu_w[e], gu_b[e], dn_w[e], dn_b[e])

    yb = lax.map(run_block, (xb, block_e)).reshape(-1, D_MODEL)
    y_assign = (yb[slot] * gates.reshape(-1)[order][:, None]).astype(xt.dtype)
    y = jnp.zeros_like(xt).at[sorted_tok].add(y_assign)
    return y.reshape(shp)


def setup_inputs(seed: int = 0) -> dict:
    key = jax.random.key(seed)
    keys = iter(jax.random.split(key, 40))

    def nrm(shape, scale):
        return jax.random.normal(next(keys), shape, jnp.float32) * scale

    D = D_MODEL
    return {
        'x_prompt': nrm((BATCH, SEQ, D), 1.0),
        'x_sample': nrm((DEC_BATCH, DEC_SEQ, D), 1.0),
        'cache_na_k': nrm((DEC_BATCH, N_EVEN, PAST_LEN, NA_HEADS, HEAD_DIM), 1.0),
        'cache_na_v': nrm((DEC_BATCH, N_EVEN, PAST_LEN, NA_HEADS, HEAD_DIM), 1.0),
        'cache_swa_k': nrm((DEC_BATCH, N_ODD, PAST_LEN, SWA_KV_HEADS, HEAD_DIM), 1.0),
        'cache_swa_v': nrm((DEC_BATCH, N_ODD, PAST_LEN, SWA_KV_HEADS, HEAD_DIM), 1.0),
        'c': nrm((DEC_BATCH, D), 1.0),
        'c_ctx': nrm((D,), 1.0),
        'ada_w': nrm((DEPTH, D, 6 * D), D ** -0.5),
        'ada_b': nrm((DEPTH, 6 * D), 0.02),
        'ln_g': 1.0 + nrm((DEPTH, 2, D), 0.02),
        'ln_b': nrm((DEPTH, 2, D), 0.02),
        'ev_in_w': nrm((N_EVEN, D, EVEN_PROJ), D ** -0.5),
        'ev_in_b': nrm((N_EVEN, EVEN_PROJ), 0.02),
        'hy_conv_w': nrm((N_EVEN, HY_SHORT, 1, HY_PROJ), HY_SHORT ** -0.5),
        'hy_conv_b': nrm((N_EVEN, HY_PROJ), 0.02),
        'hy_f_w1': nrm((N_EVEN, HY_EMB, HY_FF), HY_EMB ** -0.5),
        'hy_f_b1': nrm((N_EVEN, HY_FF), 0.02),
        'hy_f_freq': 1.0 + nrm((N_EVEN, HY_FF), 0.02),
        'hy_f_w2': nrm((N_EVEN, HY_FF, HY_FF), HY_FF ** -0.5),
        'hy_f_b2': nrm((N_EVEN, HY_FF), 0.02),
        'hy_f_w3': nrm((N_EVEN, HY_FF, 2 * HY_W), 0.1 * HY_FF ** -0.5),
        'hy_skip': nrm((N_EVEN, HY_W), 1.0),
        'na_rpb': nrm((N_EVEN, NA_HEADS, 2 * NA_KH - 1, 2 * NA_KW - 1), 0.02),
        'od_in_w': nrm((N_ODD, D, SWA_PROJ), D ** -0.5),
        'od_in_b': nrm((N_ODD, SWA_PROJ), 0.02),
        'swa_sink': nrm((N_ODD, SWA_HEADS), 1.0),
        'mix_out_w': nrm((DEPTH, MIX_W, D), DN_BETA * MIX_W ** -0.5),
        'mix_out_b': nrm((DEPTH, D), 0.02),
        'router_w': nrm((DEPTH, D, N_EXPERTS), D ** -0.5),
        'router_b': nrm((DEPTH, N_EXPERTS), 0.01),
        'exp_gu_w': nrm((DEPTH, N_EXPERTS, D, 2 * D_FF), D ** -0.5),
        'exp_gu_b': nrm((DEPTH, N_EXPERTS, 2 * D_FF), 0.02),
        'exp_dn_w': nrm((DEPTH, N_EXPERTS, D_FF, D), DN_BETA * D_FF ** -0.5),
        'exp_dn_b': nrm((DEPTH, N_EXPERTS, D), 0.02),
    }


def reference(x_prompt, x_sample, cache_na_k, cache_na_v, cache_swa_k, cache_swa_v, c, c_ctx,
              ada_w, ada_b, ln_g, ln_b,
              ev_in_w, ev_in_b, hy_conv_w, hy_conv_b, hy_f_w1, hy_f_b1, hy_f_freq, hy_f_w2, hy_f_b2, hy_f_w3,
              hy_skip, na_rpb,
              od_in_w, od_in_b, swa_sink,
              mix_out_w, mix_out_b,
              router_w, router_b, exp_gu_w, exp_gu_b, exp_dn_w, exp_dn_b):
    def hyena_l(u, e):
        return hyena_mixer(u, hy_conv_w[e], hy_conv_b[e], hy_f_w1[e], hy_f_b1[e], hy_f_freq[e],
                           hy_f_w2[e], hy_f_b2[e], hy_f_w3[e], hy_skip[e])

    def split_even(proj):
        B, L, _ = proj.shape
        q, k, v = jnp.split(proj[..., HY_PROJ:], 3, axis=-1)
        shp = (B, L, NA_HEADS, HEAD_DIM)
        return proj[..., :HY_PROJ], q.reshape(shp), k.reshape(shp), v.reshape(shp)

    def split_odd(proj):
        B, L, _ = proj.shape
        nq, nk = SWA_HEADS * HEAD_DIM, SWA_KV_HEADS * HEAD_DIM
        q = proj[..., :nq].reshape(B, L, SWA_HEADS, HEAD_DIM)
        k = proj[..., nq:nq + nk].reshape(B, L, SWA_KV_HEADS, HEAD_DIM)
        v = proj[..., nq + nk:].reshape(B, L, SWA_KV_HEADS, HEAD_DIM)
        return q, k, v

    def finish_layer(x, y_mix, g1, sh2, sc2, g2, l):
        x = post_norm(x, y_mix @ mix_out_w[l] + mix_out_b[l], g1, ln_g[l, 0], ln_b[l, 0])
        f = moe(modulate(x, sh2, sc2), router_w[l], router_b[l], exp_gu_w[l], exp_gu_b[l],
                exp_dn_w[l], exp_dn_b[l])
        return post_norm(x, f, g2, ln_g[l, 1], ln_b[l, 1])

    x = x_prompt
    na_k, na_v, swa_k, swa_v = [], [], [], []
    for l in range(DEPTH):
        sh1, sc1, g1, sh2, sc2, g2 = adaln(c_ctx, ada_w[l], ada_b[l])
        h = modulate(x, sh1, sc1)
        if l % 2 == 0:
            e = l // 2
            hy, q, k, v = split_even(h @ ev_in_w[e] + ev_in_b[e])
            y_mix = jnp.concatenate([hyena_l(hy, e), dense_context_attention(q, k, v, None)], axis=-1)
            na_k.append(k)
            na_v.append(v)
        else:
            o = l // 2
            q, k, v = split_odd(h @ od_in_w[o] + od_in_b[o])
            y_mix = dense_context_attention(q, k, v, swa_sink[o])
            swa_k.append(k)
            swa_v.append(v)
        x = finish_layer(x, y_mix, g1, sh2, sc2, g2, l)
    y_prompt = x

    x = x_sample
    L = x.shape[1]
    pos = jnp.arange(L)
    rows, cols = pos // GRID_W, pos % GRID_W
    for l in range(DEPTH):
        sh1, sc1, g1, sh2, sc2, g2 = adaln(c, ada_w[l], ada_b[l])
        h = modulate(x, sh1, sc1)
        if l % 2 == 0:
            e = l // 2
            hy, q, k, v = split_even(h @ ev_in_w[e] + ev_in_b[e])
            y_na = neighbourhood_attention(q, k, v, cache_na_k[:, e], cache_na_v[:, e], na_rpb[e])
            y_mix = jnp.concatenate([hyena_l(hy, e), y_na], axis=-1)
        else:
            o = l // 2
            q, k, v = split_odd(h @ od_in_w[o] + od_in_b[o])
            y_mix = window_attention(axial_rope(q, rows, cols), axial_rope(k, rows, cols), v,
                                     cache_swa_k[:, o], cache_swa_v[:, o], swa_sink[o])
        x = finish_layer(x, y_mix, g1, sh2, sc2, g2, l)
    y_sample = x

    new_na_k = jnp.stack(na_k, axis=1)
    new_na_v = jnp.stack(na_v, axis=1)
    new_swa_k = jnp.stack(swa_k, axis=1)
    new_swa_v = jnp.stack(swa_v, axis=1)
    return (y_prompt, y_sample, new_na_k, new_na_v, new_swa_k, new_swa_v)
```

```python
import functools
import math

import jax
import jax.numpy as jnp
from jax import lax
from jax.experimental import pallas as pl
from jax.experimental.pallas import tpu as pltpu

F32 = jnp.float32
BF16 = jnp.bfloat16
I32 = jnp.int32
HI = lax.Precision.HIGHEST

D = 1024
DEPTH = 4
BATCH, SEQ = 16, 256
DEC_BATCH, DEC_SEQ = 4, 2048
PAST = 256
GRID_W = 64
HEAD = 64
HY_W = 512
HY_PROJ = 3 * HY_W
HY_BANDS = 16
HY_EMB = 1 + 2 * HY_BANDS
HY_FF = 64
NA_HEADS = 8
NA_W = NA_HEADS * HEAD
NA_KH, NA_KW = 8, 16
SWA_HEADS, SWA_KV = 16, 4
SWA_WINDOW = 128
N_EXPERTS, TOP_K = 32, 4
D_FF = 1024
SWIGLU_LIMIT = 7.0
SWIGLU_ALPHA = 1.702
DN_ALPHA = (2 * DEPTH) ** 0.25
LN_EPS = 1e-5
NEG = -1e30
ROPE_THETA = 10000.0

TC = BATCH * SEQ
TS = DEC_BATCH * DEC_SEQ
T = TC + TS
TM = 256
NT = T // TM
NCT = TC // TM
TPS = DEC_SEQ // TM
LANE = 128
MOE_BM = 256
N_SLOT_BLOCKS = T * TOP_K // MOE_BM + N_EXPERTS
N_SLOTS = N_SLOT_BLOCKS * MOE_BM
VMEM_LIMIT = 56 * 1024 * 1024


def _params(sem, vmem=None):
    return pltpu.CompilerParams(dimension_semantics=sem, vmem_limit_bytes=vmem)


def _mod_row(i):
    return jnp.where(i < NCT, 0, 1 + (i - NCT) // TPS)


SUB = 8
ROW_TILES = D // LANE


def _load_rows(ref, nrows, lead=()):
    parts = [ref[lead + (pl.ds(s, nrows, stride=ROW_TILES), slice(None))] for s in range(ROW_TILES)]
    return jnp.concatenate(parts, axis=1)


def _store_rows(ref, val):
    nrows = val.shape[0]
    for s in range(ROW_TILES):
        ref[pl.ds(s, nrows, stride=ROW_TILES), :] = val[:, s * LANE:(s + 1) * LANE]


def _adaln_body(c_ref, w_ref, b_ref, o_ref):
    c = c_ref[...]
    s = c / (1.0 + jnp.exp(-c))
    o_ref[...] = jnp.dot(s, w_ref[...], precision=HI, preferred_element_type=F32) + b_ref[...]


def _adaln(cond8, ada_w, ada_b):
    nt = 1536
    return pl.pallas_call(
        _adaln_body,
        grid=(DEPTH, 6 * D // nt),
        in_specs=[pl.BlockSpec((8, D), lambda l, j: (0, 0)),
                  pl.BlockSpec((None, D, nt), lambda l, j: (l, 0, j)),
                  pl.BlockSpec((None, 1, nt), lambda l, j: (l, 0, j))],
        out_specs=pl.BlockSpec((None, 8, nt), lambda l, j: (l, 0, j)),
        out_shape=jax.ShapeDtypeStruct((DEPTH, 8, 6 * D), F32),
        compiler_params=_params(("arbitrary", "arbitrary"), VMEM_LIMIT),
        name="adaln",
    )(cond8, ada_w, ada_b.reshape(DEPTH, 1, 6 * D))


def _rope(x, cos, sin):
    lane = lax.broadcasted_iota(I32, (1, LANE), 1)
    first = (lane & 16) == 0
    outs = []
    for j in range(x.shape[1] // LANE):
        blk = x[:, j * LANE:(j + 1) * LANE]
        partner = jnp.where(first, pltpu.roll(blk, LANE - 16, 1), pltpu.roll(blk, 16, 1))
        outs.append(blk * cos + partner * sin)
    return jnp.concatenate(outs, axis=1) if len(outs) > 1 else outs[0]


def _inproj_body(*refs, splits, rope):
    if rope:
        x_ref, m_ref, w_ref, b_ref, cos_ref, sin_ref = refs[:6]
        outs = refs[6:]
    else:
        x_ref, m_ref, w_ref, b_ref = refs[:4]
        outs = refs[4:]
    h = x_ref[...] * (1.0 + m_ref[1:2, :]) + m_ref[0:1, :]
    y = jnp.dot(h.astype(BF16), w_ref[...], preferred_element_type=F32) + b_ref[...]
    off = 0
    for o_ref, (width, do_rope) in zip(outs, splits):
        part = y[:, off:off + width]
        if do_rope:
            part = _rope(part, cos_ref[...], sin_ref[...])
        o_ref[...] = part.astype(o_ref.dtype)
        off += width


def _inproj(x, mods, l, w_bf, b, splits, rope_tabs=None):
    p = w_bf.shape[1]
    in_specs = [pl.BlockSpec((TM, D), lambda i: (i, 0)),
                pl.BlockSpec((None, None, 6, D), lambda i: (l, _mod_row(i), 0, 0)),
                pl.BlockSpec((D, p), lambda i: (0, 0)),
                pl.BlockSpec((1, p), lambda i: (0, 0))]
    args = [x, mods, w_bf, b.reshape(1, p)]
    if rope_tabs is not None:
        in_specs += [pl.BlockSpec((TM, LANE), lambda i: (i, 0))] * 2
        args += list(rope_tabs)
    return pl.pallas_call(
        functools.partial(_inproj_body, splits=splits, rope=rope_tabs is not None),
        grid=(NT,),
        in_specs=in_specs,
        out_specs=[pl.BlockSpec((TM, w), lambda i: (i, 0)) for w, _ in splits],
        out_shape=[jax.ShapeDtypeStruct((T, w), F32) for w, _ in splits],
        compiler_params=_params(("parallel",), VMEM_LIMIT),
        name="inproj",
    )(*args)


def _lo_mask():
    return lax.broadcasted_iota(I32, (1, LANE), 1) < HEAD


def _dup_half(x2, upper):
    swapped = pltpu.roll(x2, HEAD, 1)
    keep = jnp.logical_xor(_lo_mask(), upper)
    return jnp.where(keep, x2, swapped)


def _pair_attention(q2, segs, sink_lo=None, sink_hi=None):
    m_rows = q2.shape[0]
    lo = _lo_mask()
    qs = jnp.concatenate([jnp.where(lo, q2, 0.0), jnp.where(lo, 0.0, q2)], axis=0).astype(BF16)
    scale = HEAD ** -0.5
    scores = []
    for kd, _, bias in segs:
        s = lax.dot_general(qs, kd, (((1,), (1,)), ((), ())), preferred_element_type=F32) * scale
        if bias is not None:
            s = s + bias
        scores.append(s)
    mx = functools.reduce(jnp.maximum, [jnp.max(s, axis=-1, keepdims=True) for s in scores])
    sink = None
    if sink_lo is not None:
        row = lax.broadcasted_iota(I32, (2 * m_rows, 1), 0)
        sink = jnp.where(row < m_rows, sink_lo, sink_hi)
        mx = jnp.maximum(mx, sink)
    den = jnp.zeros_like(mx)
    acc = jnp.zeros((2 * m_rows, LANE), F32)
    for s, (_, vd, _) in zip(scores, segs):
        p = jnp.exp(s - mx)
        den = den + jnp.sum(p, axis=-1, keepdims=True)
        acc = acc + jnp.dot(p.astype(BF16), vd, preferred_element_type=F32)
    if sink is not None:
        den = den + jnp.exp(sink - mx)
    o = acc / den
    return jnp.where(lo, o[:m_rows], o[m_rows:])


def _ctx_even_body(q_ref, k_ref, v_ref, o_ref):
    o_ref[...] = _pair_attention(q_ref[...], [(k_ref[...].astype(BF16), v_ref[...].astype(BF16), None)]
                                 ).astype(o_ref.dtype)


def _ctx_even_attn(q, k, v):
    spec = pl.BlockSpec((SEQ, LANE), lambda b, j: (b, j))
    return pl.pallas_call(
        _ctx_even_body,
        grid=(BATCH, NA_W // LANE),
        in_specs=[spec, spec, spec],
        out_specs=spec,
        out_shape=jax.ShapeDtypeStruct((TC, NA_W), BF16),
        compiler_params=_params(("parallel", "parallel")),
        name="ctx_attn_even",
    )(q, k, v)


def _ctx_odd_body(sink_ref, q_ref, k_ref, v_ref, o_ref):
    j = pl.program_id(1)
    upper = ((j // 2) % 2) == 1
    kd = _dup_half(k_ref[...], upper).astype(BF16)
    vd = _dup_half(v_ref[...], upper).astype(BF16)
    o_ref[...] = _pair_attention(q_ref[...], [(kd, vd, None)],
                                 sink_ref[2 * j], sink_ref[2 * j + 1]).astype(o_ref.dtype)


def _ctx_odd_attn(sink, q, k, v):
    qspec = pl.BlockSpec((SEQ, LANE), lambda b, j, s: (b, j))
    kspec = pl.BlockSpec((SEQ, LANE), lambda b, j, s: (b, j // 4))
    return pl.pallas_call(
        _ctx_odd_body,
        grid_spec=pltpu.PrefetchScalarGridSpec(
            num_scalar_prefetch=1, grid=(BATCH, SWA_HEADS * HEAD // LANE),
            in_specs=[qspec, kspec, kspec], out_specs=qspec),
        out_shape=jax.ShapeDtypeStruct((TC, SWA_HEADS * HEAD), BF16),
        compiler_params=_params(("parallel", "parallel")),
        name="ctx_attn_odd",
    )(sink, q, k, v)


SWA_BLK = 128
SWA_NB = DEC_SEQ // SWA_BLK


def _swa_body(sink_ref, q_ref, kp_ref, kc_ref, kn_ref, vp_ref, vc_ref, vn_ref, ck_ref, cv_ref, o_ref):
    n = pl.program_id(1)
    g = pl.program_id(2)
    upper = (g % 2) == 1
    qi = lax.broadcasted_iota(I32, (SWA_BLK, 3 * SWA_BLK), 0)
    kj = lax.broadcasted_iota(I32, (SWA_BLK, 3 * SWA_BLK), 1)
    kpos = (n - 1) * SWA_BLK + kj
    rel = kj - SWA_BLK - qi
    ok = (jnp.abs(rel) <= SWA_WINDOW) & (kpos >= 0) & (kpos < DEC_SEQ)
    mask = jnp.where(ok, 0.0, NEG)
    mask2 = jnp.concatenate([mask, mask], axis=0)
    kloc = jnp.concatenate([kp_ref[...], kc_ref[...], kn_ref[...]], axis=0)
    vloc = jnp.concatenate([vp_ref[...], vc_ref[...], vn_ref[...]], axis=0)
    segs = [(_dup_half(kloc, upper).astype(BF16), _dup_half(vloc, upper).astype(BF16), mask2),
            (_dup_half(ck_ref[...], upper).astype(BF16), _dup_half(cv_ref[...], upper).astype(BF16), None)]
    for jj in range(2):
        h0 = 4 * g + 2 * jj
        o_ref[:, jj * LANE:(jj + 1) * LANE] = _pair_attention(
            q_ref[:, jj * LANE:(jj + 1) * LANE], segs, sink_ref[h0], sink_ref[h0 + 1]).astype(o_ref.dtype)


def _swa_attn(sink, q, k, v, ck, cv, o):
    base = TC // SWA_BLK
    qspec = pl.BlockSpec((SWA_BLK, 2 * LANE), lambda b, n, g, s: (base + b * SWA_NB + n, g))

    def kv(shift):
        return pl.BlockSpec(
            (SWA_BLK, LANE),
            lambda b, n, g, s: (base + b * SWA_NB + jnp.clip(n + shift, 0, SWA_NB - 1), g // 2))

    cspec = pl.BlockSpec((None, None, PAST, LANE), lambda b, n, g, s: (b, o, 0, g // 2))
    ospec = pl.BlockSpec((SWA_BLK, 2 * LANE), lambda b, n, g, s: (b * SWA_NB + n, g))
    return pl.pallas_call(
        _swa_body,
        grid_spec=pltpu.PrefetchScalarGridSpec(
            num_scalar_prefetch=1, grid=(DEC_BATCH, SWA_NB, SWA_KV),
            in_specs=[qspec, kv(-1), kv(0), kv(1), kv(-1), kv(0), kv(1), cspec, cspec],
            out_specs=ospec),
        out_shape=jax.ShapeDtypeStruct((TS, SWA_HEADS * HEAD), BF16),
        compiler_params=_params(("parallel", "parallel", "parallel")),
        name="swa_attn",
    )(sink, q, k, k, k, v, v, v, ck, cv)


NA_QROWS = 4
NA_TILES = DEC_SEQ // TM
NA_KTILES = 3
NA_KSTART_MAX = NA_TILES - NA_KTILES


def _na_body(q_ref, k0_ref, k1_ref, k2_ref, v0_ref, v1_ref, v2_ref, ck_ref, cv_ref, bias_ref, o_ref):
    kloc = jnp.concatenate([k0_ref[...], k1_ref[...], k2_ref[...]], axis=0).astype(BF16)
    vloc = jnp.concatenate([v0_ref[...], v1_ref[...], v2_ref[...]], axis=0).astype(BF16)
    bias = jnp.concatenate([bias_ref[0], bias_ref[1]], axis=0)
    segs = [(kloc, vloc, bias), (ck_ref[...].astype(BF16), cv_ref[...].astype(BF16), None)]
    o_ref[...] = _pair_attention(q_ref[...], segs).astype(o_ref.dtype)


def _na_attn(q, k, v, ck, cv, biasmask, e):
    base = NCT

    def kstart(i):
        return jnp.clip(i - 1, 0, NA_KSTART_MAX)

    qspec = pl.BlockSpec((TM, LANE), lambda i, j, b: (base + b * NA_TILES + i, j))

    def kv(t):
        return pl.BlockSpec((TM, LANE), lambda i, j, b: (base + b * NA_TILES + kstart(i) + t, j))

    cspec = pl.BlockSpec((None, None, PAST, LANE), lambda i, j, b: (b, e, 0, j))
    bspec = pl.BlockSpec((None, 2, TM, NA_KTILES * TM),
                         lambda i, j, b: (jnp.where(i == 0, 0, jnp.where(i == NA_TILES - 1, 2, 1)), j, 0, 0))
    ospec = pl.BlockSpec((TM, LANE), lambda i, j, b: (b * NA_TILES + i, j))
    return pl.pallas_call(
        _na_body,
        grid=(NA_TILES, NA_W // LANE, DEC_BATCH),
        in_specs=[qspec, kv(0), kv(1), kv(2), kv(0), kv(1), kv(2), cspec, cspec, bspec],
        out_specs=ospec,
        out_shape=jax.ShapeDtypeStruct((TS, NA_W), BF16),
        compiler_params=_params(("parallel", "parallel", "parallel"), VMEM_LIMIT),
        name="na_attn",
    )(q, k, k, k, v, v, v, ck, cv, biasmask)


def _na_biasmask(rpb):
    rows = DEC_SEQ // GRID_W
    out = []
    a = jnp.arange(NA_QROWS)[:, None, None, None]
    c = jnp.arange(GRID_W)[None, :, None, None]
    kr = jnp.arange(NA_KTILES * NA_QROWS)[None, None, :, None]
    c2 = jnp.arange(GRID_W)[None, None, None, :]
    for i in (0, 1, NA_TILES - 1):
        r = NA_QROWS * i + a
        r2 = NA_QROWS * min(max(i - 1, 0), NA_KSTART_MAX) + kr
        rs = jnp.clip(r - NA_KH // 2, 0, rows - NA_KH)
        row_ok = (r2 >= rs) & (r2 < rs + NA_KH)
        win_lo = jnp.clip(c - NA_KW // 2, 0, GRID_W - NA_KW)
        col_ok = (c2 >= win_lo) & (c2 < win_lo + NA_KW)
        ri = jnp.clip(r2 - r + NA_KH - 1, 0, 2 * NA_KH - 2)
        ci = jnp.clip(c2 - c, 1 - NA_KW, NA_KW - 1) + NA_KW - 1
        ri, ci, ok = jnp.broadcast_arrays(ri, ci, row_ok & col_ok)
        bias = rpb[:, ri, ci].astype(F32)
        bm = jnp.where(ok[None], bias, NEG)
        out.append(bm.reshape(NA_HEADS, TM, NA_KTILES * TM))
    return jnp.stack(out, axis=0)


def _hy_gate_body(u_ref, up_ref, un_ref, w_ref, b_ref, x0_ref, vg_ref, vgb_ref):
    i = pl.program_id(0)
    ti = (i - NCT) % TPS
    first = (i < NCT) | (ti == 0)
    last = (i < NCT) | (ti == TPS - 1)
    u = u_ref[...]
    row = lax.broadcasted_iota(I32, (TM, 1), 0)
    prev_row = jnp.where(first, 0.0, up_ref[7:8, :])
    next_row = jnp.where(last, 0.0, un_ref[0:1, :])
    before = jnp.where(row == 0, prev_row, pltpu.roll(u, 1, 0))
    after = jnp.where(row == TM - 1, next_row, pltpu.roll(u, TM - 1, 0))
    uc = w_ref[0:1, :] * before + w_ref[1:2, :] * u + w_ref[2:3, :] * after + b_ref[...]
    x0_ref[...] = uc[:, :HY_W]
    vg = uc[:, 2 * HY_W:] * uc[:, HY_W:2 * HY_W]
    vg_ref[...] = vg
    vgb_ref[...] = vg.astype(BF16)


def _hy_gate(hy, conv_w, conv_b):
    nb8 = TM // 8
    return pl.pallas_call(
        _hy_gate_body,
        grid=(NT,),
        in_specs=[pl.BlockSpec((TM, HY_PROJ), lambda i: (i, 0)),
                  pl.BlockSpec((8, HY_PROJ), lambda i: (jnp.maximum(i * nb8 - 1, 0), 0)),
                  pl.BlockSpec((8, HY_PROJ), lambda i: (jnp.minimum(i * nb8 + nb8, T // 8 - 1), 0)),
                  pl.BlockSpec((3, HY_PROJ), lambda i: (0, 0)),
                  pl.BlockSpec((1, HY_PROJ), lambda i: (0, 0))],
        out_specs=[pl.BlockSpec((TM, HY_W), lambda i: (i, 0))] * 3,
        out_shape=[jax.ShapeDtypeStruct((T, HY_W), F32), jax.ShapeDtypeStruct((T, HY_W), F32),
                   jax.ShapeDtypeStruct((T, HY_W), BF16)],
        compiler_params=_params(("parallel",)),
        name="hy_gate",
    )(hy, hy, hy, conv_w, conv_b)


def _dft_mats(L):
    n = 2 * L
    f = jnp.arange(L, dtype=I32)[:, None]
    t = jnp.arange(L, dtype=I32)[None, :]
    ph = ((f * t) % n).astype(F32) * (2.0 * math.pi / n)
    c = jnp.cos(ph)
    s = jnp.where(f == 0, jnp.where(t % 2 == 0, 1.0, -1.0), -jnp.sin(ph))
    return c.astype(BF16), s.astype(BF16), s.T.astype(BF16)


def _hy_features(L):
    nn = jnp.arange(L, dtype=F32)[:, None]
    t = jnp.linspace(0.0, 1.0, L, dtype=F32)[:, None]
    bands = jnp.linspace(1e-4, HY_BANDS - 1, HY_BANDS, dtype=F32)[None, :]
    ang = (2.0 * math.pi / L) * nn * bands
    z = jnp.concatenate([t, jnp.cos(ang), -jnp.sin(ang)], axis=-1)
    z = jnp.pad(z, ((0, 0), (0, 40 - HY_EMB)))
    deltas = jnp.abs(jnp.linspace(math.log(1e-2) / 1.5, math.log(1e-2) / 0.3, HY_W, dtype=F32))
    decay = jnp.exp(-t * deltas)
    bwd = jnp.where(jnp.arange(L)[:, None] == 0, 0.0, decay)
    return z, jnp.concatenate([decay, bwd], axis=1)


def _hy_filter_body(z_ref, w1_ref, b1_ref, fr_ref, w2_ref, b2_ref, w3_ref, dec_ref, c_ref, s_ref,
                    kre_ref, kim_ref, h_scr, *, L, fb):
    f = pl.program_id(0)

    @pl.when(f == 0)
    def _():
        fr = fr_ref[...]
        h = jnp.sin(fr * (jnp.dot(z_ref[...], w1_ref[...], precision=HI, preferred_element_type=F32)
                          + b1_ref[...]))
        h = jnp.sin(fr * (jnp.dot(h, w2_ref[...], precision=HI, preferred_element_type=F32) + b2_ref[...]))
        h = jnp.dot(h, w3_ref[...], precision=HI, preferred_element_type=F32) * dec_ref[...]
        h_scr[...] = h.astype(BF16)

    hb = h_scr[...]
    a_re = jnp.dot(c_ref[...], hb, preferred_element_type=F32)
    a_im = jnp.dot(s_ref[...], hb, preferred_element_type=F32)
    grow = f * fb + lax.broadcasted_iota(I32, (fb, 1), 0)
    packed = grow == 0
    sc = jnp.where(packed, 1.0 / (2 * L), 2.0 / (2 * L))
    kre_ref[...] = (a_re[:, :HY_W] + a_re[:, HY_W:]) * sc
    kim_ref[...] = (a_im[:, :HY_W] + jnp.where(packed, 1.0, -1.0) * a_im[:, HY_W:]) * sc


def _hy_filter(L, feats, mats, w1, b1, freq, w2, b2, w3):
    z, dec = feats
    c, s, _ = mats
    fb = min(L, 512)
    full = lambda shape: pl.BlockSpec(shape, lambda f: (0,) * len(shape))
    w1p = jnp.pad(w1, ((0, 40 - HY_EMB), (0, 0)))
    return pl.pallas_call(
        functools.partial(_hy_filter_body, L=L, fb=fb),
        grid=(L // fb,),
        in_specs=[full((L, 40)), full((40, HY_FF)), full((1, HY_FF)), full((1, HY_FF)),
                  full((HY_FF, HY_FF)), full((1, HY_FF)), full((HY_FF, 2 * HY_W)), full((L, 2 * HY_W)),
                  pl.BlockSpec((fb, L), lambda f: (f, 0)), pl.BlockSpec((fb, L), lambda f: (f, 0))],
        out_specs=[pl.BlockSpec((fb, HY_W), lambda f: (f, 0))] * 2,
        out_shape=[jax.ShapeDtypeStruct((L, HY_W), F32)] * 2,
        scratch_shapes=[pltpu.VMEM((L, 2 * HY_W), BF16)],
        compiler_params=_params(("arbitrary",), VMEM_LIMIT),
        name="hy_filter",
    )(z, w1p, b1.reshape(1, -1), freq.reshape(1, -1), w2, b2.reshape(1, -1), w3, dec, c, s)


def _hy_conv_body(v_ref, cr_ref, sr_ref, cc_ref, sc_ref, kre_ref, kim_ref, o_ref, *, fb):
    f = pl.program_id(1)
    vb = v_ref[...]
    vre = jnp.dot(cr_ref[...], vb, preferred_element_type=F32)
    vim = jnp.dot(sr_ref[...], vb, preferred_element_type=F32)
    kre = kre_ref[...]
    kim = kim_ref[...]
    grow = f * fb + lax.broadcasted_iota(I32, (fb, 1), 0)
    packed = grow == 0
    yre = vre * kre - jnp.where(packed, 0.0, vim * kim)
    yim = jnp.where(packed, vim * kim, vre * kim + vim * kre)
    part = (jnp.dot(cc_ref[...], yre.astype(BF16), preferred_element_type=F32)
            + jnp.dot(sc_ref[...], yim.astype(BF16), preferred_element_type=F32))

    @pl.when(f == 0)
    def _():
        o_ref[...] = part

    @pl.when(f > 0)
    def _():
        o_ref[...] += part


def _hy_conv(vgb, mats, kre, kim, L, nseq, row_off):
    c, s, st = mats
    fb = min(L, 256)
    off = row_off // L
    return pl.pallas_call(
        functools.partial(_hy_conv_body, fb=fb),
        grid=(nseq, L // fb),
        in_specs=[pl.BlockSpec((L, HY_W), lambda b, f: (off + b, 0)),
                  pl.BlockSpec((fb, L), lambda b, f: (f, 0)),
                  pl.BlockSpec((fb, L), lambda b, f: (f, 0)),
                  pl.BlockSpec((L, fb), lambda b, f: (0, f)),
                  pl.BlockSpec((L, fb), lambda b, f: (0, f)),
                  pl.BlockSpec((fb, HY_W), lambda b, f: (f, 0)),
                  pl.BlockSpec((fb, HY_W), lambda b, f: (f, 0))],
        out_specs=pl.BlockSpec((L, HY_W), lambda b, f: (b, 0)),
        out_shape=jax.ShapeDtypeStruct((nseq * L, HY_W), F32),
        compiler_params=_params(("parallel", "arbitrary"), VMEM_LIMIT),
        name="hy_conv",
    )(vgb, c, s, c, st, kre, kim)


def _layer_norm(z, g, b):
    mu = jnp.mean(z, axis=-1, keepdims=True)
    zc = z - mu
    var = jnp.mean(zc * zc, axis=-1, keepdims=True)
    return zc * lax.rsqrt(var + LN_EPS) * g + b


def _post1_body(*refs, even):
    if even:
        (x_ref, m_ref, conv_ref, vg_ref, x0_ref, skip_ref, at_ref, w_ref, b_ref, g_ref, be_ref,
         rw_ref, rb_ref, x1_ref, xm_ref, lg_ref) = refs
        yh = ((conv_ref[...] + vg_ref[...] * skip_ref[...]) * x0_ref[...]).astype(BF16)
        y = (jnp.dot(yh, w_ref[:HY_W, :], preferred_element_type=F32)
             + jnp.dot(at_ref[...], w_ref[HY_W:, :], preferred_element_type=F32))
    else:
        (x_ref, m_ref, at_ref, w_ref, b_ref, g_ref, be_ref, rw_ref, rb_ref, x1_ref, xm_ref, lg_ref) = refs
        y = jnp.dot(at_ref[...], w_ref[...], preferred_element_type=F32)
    y = y + b_ref[...]
    x1 = _layer_norm(DN_ALPHA * x_ref[...] + m_ref[2:3, :] * y, g_ref[...], be_ref[...])
    x1_ref[...] = x1
    xm = x1 * (1.0 + m_ref[4:5, :]) + m_ref[3:4, :]
    _store_rows(xm_ref, xm)
    lg_ref[...] = lax.dot_general(rw_ref[...], xm, (((1,), (1,)), ((), ())), precision=HI,
                                  preferred_element_type=F32) + rb_ref[...]


def _post1(x, mods, l, parts, w_bf, b, g, be, rw_t, rb, even):
    row = lambda w: pl.BlockSpec((TM, w), lambda i: (i, 0))
    full = lambda shape: pl.BlockSpec(shape, lambda i: (0,) * len(shape))
    in_specs = [row(D), pl.BlockSpec((None, None, 6, D), lambda i: (l, _mod_row(i), 0, 0))]
    if even:
        in_specs += [row(HY_W), row(HY_W), row(HY_W), full((1, HY_W)), row(NA_W)]
    else:
        in_specs += [row(D)]
    in_specs += [full((D, D)), full((1, D)), full((1, D)), full((1, D)), full((N_EXPERTS, D)),
                 full((N_EXPERTS, 1))]
    return pl.pallas_call(
        functools.partial(_post1_body, even=even),
        grid=(NT,),
        in_specs=in_specs,
        out_specs=[row(D), pl.BlockSpec((TM * SUB, LANE), lambda i: (i, 0)),
                   pl.BlockSpec((None, N_EXPERTS, TM), lambda i: (i, 0, 0))],
        out_shape=[jax.ShapeDtypeStruct((T, D), F32), jax.ShapeDtypeStruct((T * SUB, LANE), F32),
                   jax.ShapeDtypeStruct((NT, N_EXPERTS, TM), F32)],
        compiler_params=_params(("parallel",), VMEM_LIMIT),
        name="post1",
    )(x, mods, *parts, w_bf, b.reshape(1, D), g.reshape(1, D), be.reshape(1, D), rw_t,
      rb.reshape(N_EXPERTS, 1))


def _route_body(lg_ref, slot_ref, gate_ref, pe_ref, pd_ref, be_ref, eidx_scr, rank_scr):
    eio = lax.broadcasted_iota(I32, (N_EXPERTS, TM), 0)
    ri = lax.broadcasted_iota(I32, (TM, TM), 0)
    ci = lax.broadcasted_iota(I32, (TM, TM), 1)
    upper = jnp.where(ri < ci, 1.0, 0.0).astype(BF16)

    def tile(i, carry):
        lg = lg_ref[i]
        sel = jnp.zeros((N_EXPERTS, TM), F32)
        vals, hots = [], []
        for k in range(TOP_K):
            mx = jnp.max(lg, axis=0, keepdims=True)
            idx = jnp.min(jnp.where(lg == mx, eio, N_EXPERTS), axis=0, keepdims=True)
            hot = eio == idx
            lg = jnp.where(hot, -jnp.inf, lg)
            sel = sel + jnp.where(hot, 1.0, 0.0)
            vals.append(mx)
            hots.append(hot)
            eidx_scr[i, k:k + 1, :] = idx
        rank = jnp.dot(sel.astype(BF16), upper, preferred_element_type=F32) + carry
        ex = [jnp.exp(v - vals[0]) for v in vals]
        den = ex[0] + ex[1] + ex[2] + ex[3]
        for k in range(TOP_K):
            gate_ref[i, k:k + 1, :] = ex[k] / den
            rank_scr[i, k:k + 1, :] = jnp.sum(jnp.where(hots[k], rank, 0.0), axis=0, keepdims=True)
        return carry + jnp.sum(sel, axis=1, keepdims=True)

    counts = lax.fori_loop(0, NT, tile, jnp.zeros((N_EXPERTS, 1), F32))
    padded = jnp.floor((counts + (MOE_BM - 1)) * (1.0 / MOE_BM)) * MOE_BM
    e_r = lax.broadcasted_iota(I32, (N_EXPERTS, N_EXPERTS), 0)
    e_c = lax.broadcasted_iota(I32, (N_EXPERTS, N_EXPERTS), 1)
    incl = jnp.where(e_c <= e_r, 1.0, 0.0)
    padded_b = jnp.broadcast_to(padded, (N_EXPERTS, LANE))
    pad_end = jnp.dot(incl, padded_b, precision=HI, preferred_element_type=F32)
    pad_start = pad_end[:, 0:1] - padded
    pe_ref[...] = pad_end.astype(I32)
    pd_ref[...] = padded_b.astype(I32)
    blk0 = (lax.broadcasted_iota(I32, (1, TM), 1) * MOE_BM).astype(F32)
    nle = jnp.sum(jnp.where(pad_end[:, 0:1] <= blk0, 1.0, 0.0), axis=0, keepdims=True)
    be_ref[...] = jnp.minimum(nle, N_EXPERTS - 1.0).astype(I32)

    def place(i, c):
        for k in range(TOP_K):
            hot = eio == eidx_scr[i, k:k + 1, :]
            start = jnp.sum(jnp.where(hot, pad_start, 0.0), axis=0, keepdims=True)
            slot_ref[i, k:k + 1, :] = (start + rank_scr[i, k:k + 1, :]).astype(I32)
        return c

    lax.fori_loop(0, NT, place, 0)


def _route(logits):
    return pl.pallas_call(
        _route_body,
        out_shape=[jax.ShapeDtypeStruct((NT, TOP_K, TM), I32), jax.ShapeDtypeStruct((NT, TOP_K, TM), F32),
                   jax.ShapeDtypeStruct((N_EXPERTS, LANE), I32), jax.ShapeDtypeStruct((N_EXPERTS, LANE), I32),
                   jax.ShapeDtypeStruct((1, TM), I32)],
        scratch_shapes=[pltpu.VMEM((NT, TOP_K, TM), I32), pltpu.VMEM((NT, TOP_K, TM), F32)],
        compiler_params=_params(None, VMEM_LIMIT),
        name="route",
    )(logits)


def _dispatch_body(slot_ref, pe_ref, pd_ref, xm_ref, xb_ref, zero_scr, sem):
    i = pl.program_id(0)

    def fill_block(b):
        start = b * (MOE_BM * SUB)
        if not isinstance(b, int):
            start = pl.multiple_of(start, MOE_BM * SUB)
        return pltpu.make_async_copy(zero_scr, xb_ref.at[pl.ds(start, MOE_BM * SUB)], sem)

    @pl.when(i == 0)
    def _():
        zero_scr[...] = jnp.zeros_like(zero_scr)
        n_used = pe_ref[N_EXPERTS - 1] // MOE_BM
        for start_or_wait in (True, False):
            for e in range(N_EXPERTS):
                @pl.when(pd_ref[e] > 0)
                def _():
                    cp = fill_block(pe_ref[e] // MOE_BM - 1)
                    cp.start() if start_or_wait else cp.wait()
            for b in range(T * TOP_K // MOE_BM, N_SLOT_BLOCKS):
                @pl.when(b >= n_used)
                def _():
                    cp = fill_block(b)
                    cp.start() if start_or_wait else cp.wait()

    def row_copy(r, k):
        s = slot_ref[i * (TOP_K * TM) + k * TM + r]
        return pltpu.make_async_copy(xm_ref.at[pl.ds(pl.multiple_of(r * SUB, SUB), SUB)],
                                     xb_ref.at[pl.ds(pl.multiple_of(s * SUB, SUB), SUB)], sem)

    def issue(r, c):
        for k in range(TOP_K):
            row_copy(r, k).start()
        return c

    def drain(r, c):
        for k in range(TOP_K):
            row_copy(r, k).wait()
        return c

    lax.fori_loop(0, TM, issue, 0)
    lax.fori_loop(0, TM, drain, 0)


def _dispatch(slot_flat, pe, pd, xm):
    return pl.pallas_call(
        _dispatch_body,
        grid_spec=pltpu.PrefetchScalarGridSpec(
            num_scalar_prefetch=3, grid=(NT,),
            in_specs=[pl.BlockSpec((TM * SUB, LANE), lambda i, s, a, b: (i, 0))],
            out_specs=pl.BlockSpec(memory_space=pl.ANY),
            scratch_shapes=[pltpu.VMEM((MOE_BM * SUB, LANE), F32), pltpu.SemaphoreType.DMA]),
        out_shape=jax.ShapeDtypeStruct((N_SLOTS * SUB, LANE), F32),
        compiler_params=_params(("arbitrary",), VMEM_LIMIT),
        name="moe_dispatch",
    )(slot_flat, pe, pd, xm)


def _expert_body(be_ref, nu_ref, xb_ref, guw_ref, gub_ref, dnw_ref, dnb_ref, yb_ref, gu_scr, dn_scr):
    i = pl.program_id(0)

    @pl.when(i >= nu_ref[0])
    def _():
        yb_ref[...] = jnp.zeros_like(yb_ref)

    @pl.when(i < nu_ref[0])
    def _():
        prev = be_ref[jnp.maximum(i - 1, 0)]

        @pl.when((i == 0) | (be_ref[i] != prev))
        def _():
            gu_scr[...] = guw_ref[...].astype(BF16)
            dn_scr[...] = dnw_ref[...].astype(BF16)

        xb = _load_rows(xb_ref, MOE_BM).astype(BF16)
        hgu = jnp.dot(xb, gu_scr[...], preferred_element_type=F32) + gub_ref[...]
        g = jnp.minimum(hgu[:, :D_FF], SWIGLU_LIMIT)
        lin = jnp.clip(hgu[:, D_FF:], -SWIGLU_LIMIT, SWIGLU_LIMIT)
        act = (lin + 1.0) * (g / (1.0 + jnp.exp(-SWIGLU_ALPHA * g)))
        _store_rows(yb_ref, jnp.dot(act.astype(BF16), dn_scr[...], preferred_element_type=F32) + dnb_ref[...])


def _experts(block_e, n_used, xb, l, gu_w, gu_b, dn_w, dn_b):
    def blk(i, be, nu):
        return jnp.minimum(i, nu[0] - 1)

    return pl.pallas_call(
        _expert_body,
        grid_spec=pltpu.PrefetchScalarGridSpec(
            num_scalar_prefetch=2, grid=(N_SLOT_BLOCKS,),
            in_specs=[pl.BlockSpec((MOE_BM * SUB, LANE), lambda i, be, nu: (blk(i, be, nu), 0)),
                      pl.BlockSpec((None, None, D, 2 * D_FF), lambda i, be, nu: (l, be[blk(i, be, nu)], 0, 0)),
                      pl.BlockSpec((None, None, 1, 2 * D_FF), lambda i, be, nu: (l, be[blk(i, be, nu)], 0, 0)),
                      pl.BlockSpec((None, None, D_FF, D), lambda i, be, nu: (l, be[blk(i, be, nu)], 0, 0)),
                      pl.BlockSpec((None, None, 1, D), lambda i, be, nu: (l, be[blk(i, be, nu)], 0, 0))],
            out_specs=pl.BlockSpec((MOE_BM * SUB, LANE), lambda i, be, nu: (i, 0)),
            scratch_shapes=[pltpu.VMEM((D, 2 * D_FF), BF16), pltpu.VMEM((D_FF, D), BF16)]),
        out_shape=jax.ShapeDtypeStruct((N_SLOTS * SUB, LANE), F32),
        compiler_params=_params(("arbitrary",), VMEM_LIMIT),
        name="moe_experts",
    )(block_e, n_used, xb, gu_w, gu_b.reshape(DEPTH, N_EXPERTS, 1, 2 * D_FF), dn_w,
      dn_b.reshape(DEPTH, N_EXPERTS, 1, D))


def _post2_body(slot_ref, x1_ref, gt_ref, m_ref, g_ref, be_ref, yb_ref, x2_ref, y_scr, sem):
    i = pl.program_id(0)

    def row_copy(r, k):
        s = slot_ref[i * (TOP_K * TM) + k * TM + r]
        return pltpu.make_async_copy(yb_ref.at[pl.ds(pl.multiple_of(s * SUB, SUB), SUB)],
                                     y_scr.at[k, pl.ds(pl.multiple_of(r * SUB, SUB), SUB)], sem)

    def issue(r, c):
        for k in range(TOP_K):
            row_copy(r, k).start()
        return c

    def drain(r, c):
        for k in range(TOP_K):
            row_copy(r, k).wait()
        return c

    lax.fori_loop(0, TM, issue, 0)
    lax.fori_loop(0, TM, drain, 0)
    f = _load_rows(y_scr, TM, (0,)) * gt_ref[:, 0:1]
    for k in range(1, TOP_K):
        f = f + _load_rows(y_scr, TM, (k,)) * gt_ref[:, k:k + 1]
    x2_ref[...] = _layer_norm(DN_ALPHA * x1_ref[...] + m_ref[5:6, :] * f, g_ref[...], be_ref[...])


def _post2(slot_flat, x1, gates_t, mods, l, g, be, yb):
    full = lambda shape: pl.BlockSpec(shape, lambda i, s: (0,) * len(shape))
    return pl.pallas_call(
        _post2_body,
        grid_spec=pltpu.PrefetchScalarGridSpec(
            num_scalar_prefetch=1, grid=(NT,),
            in_specs=[pl.BlockSpec((TM, D), lambda i, s: (i, 0)),
                      pl.BlockSpec((TM, TOP_K), lambda i, s: (i, 0)),
                      pl.BlockSpec((None, None, 6, D), lambda i, s: (l, _mod_row(i), 0, 0)),
                      full((1, D)), full((1, D)),
                      pl.BlockSpec(memory_space=pl.ANY)],
            out_specs=pl.BlockSpec((TM, D), lambda i, s: (i, 0)),
            scratch_shapes=[pltpu.VMEM((TOP_K, TM * SUB, LANE), F32), pltpu.SemaphoreType.DMA]),
        out_shape=jax.ShapeDtypeStruct((T, D), F32),
        compiler_params=_params(("arbitrary",), VMEM_LIMIT),
        name="post2",
    )(slot_flat, x1, gates_t, mods, g.reshape(1, D), be.reshape(1, D), yb)


def _rope_tables():
    quarter = HEAD // 4
    pos = jnp.arange(DEC_SEQ)
    rows = (pos // GRID_W).astype(F32)[:, None]
    cols = (pos % GRID_W).astype(F32)[:, None]
    lane = jnp.arange(LANE)[None, :]
    d = lane % HEAD
    inv = ROPE_THETA ** (-(d % quarter).astype(F32) / quarter)
    ang = jnp.where(d < HEAD // 2, rows, cols) * inv
    sign = jnp.where((d // quarter) % 2 == 0, -1.0, 1.0)
    cos = jnp.concatenate([jnp.ones((TC, LANE), F32), jnp.tile(jnp.cos(ang), (DEC_BATCH, 1))], axis=0)
    sin = jnp.concatenate([jnp.zeros((TC, LANE), F32), jnp.tile(sign * jnp.sin(ang), (DEC_BATCH, 1))], axis=0)
    return cos, sin


def _moe_and_norm(x1, xm, logits, mods, l, ln_g, ln_b, exp_gu_w, exp_gu_b, exp_dn_w, exp_dn_b):
    slot, gates, pe, pd, be = _route(logits)
    slot_flat = slot.reshape(-1)
    pe1 = pe[:, 0]
    xb = _dispatch(slot_flat, pe1, pd[:, 0], xm)
    n_used = pe1[N_EXPERTS - 1:] // MOE_BM
    yb = _experts(be.reshape(-1), n_used, xb, l, exp_gu_w, exp_gu_b, exp_dn_w, exp_dn_b)
    gates_t = gates.transpose(0, 2, 1).reshape(T, TOP_K)
    return _post2(slot_flat, x1, gates_t, mods, l, ln_g[l, 1], ln_b[l, 1], yb)


def kernel(x_prompt, x_sample, cache_na_k, cache_na_v, cache_swa_k, cache_swa_v, c, c_ctx, ada_w, ada_b, ln_g, ln_b, ev_in_w, ev_in_b, hy_conv_w, hy_conv_b, hy_f_w1, hy_f_b1, hy_f_freq, hy_f_w2, hy_f_b2, hy_f_w3, hy_skip, na_rpb, od_in_w, od_in_b, swa_sink, mix_out_w, mix_out_b, router_w, router_b, exp_gu_w, exp_gu_b, exp_dn_w, exp_dn_b):
    x = jnp.concatenate([x_prompt.reshape(TC, D), x_sample.reshape(TS, D)], axis=0)
    cond8 = jnp.concatenate([c_ctx[None, :], c, jnp.zeros((8 - 1 - DEC_BATCH, D), F32)], axis=0)
    mods = _adaln(cond8, ada_w, ada_b).reshape(DEPTH, 8, 6, D)

    rope_tabs = _rope_tables()
    hy_consts = {L: (_hy_features(L), _dft_mats(L)) for L in (SEQ, DEC_SEQ)}
    ck_na = cache_na_k.reshape(DEC_BATCH, -1, PAST, NA_W)
    cv_na = cache_na_v.reshape(DEC_BATCH, -1, PAST, NA_W)
    ck_swa = cache_swa_k.reshape(DEC_BATCH, -1, PAST, SWA_KV * HEAD)
    cv_swa = cache_swa_v.reshape(DEC_BATCH, -1, PAST, SWA_KV * HEAD)

    na_k, na_v, swa_k, swa_v = [], [], [], []
    for l in range(DEPTH):
        w_out = mix_out_w[l].astype(BF16)
        rw_t = router_w[l].T
        if l % 2 == 0:
            e = l // 2
            hy, q, k, v = _inproj(x, mods, l, ev_in_w[e].astype(BF16), ev_in_b[e],
                                  ((HY_PROJ, False), (NA_W, False), (NA_W, False), (NA_W, False)))
            x0, vg, vgb = _hy_gate(hy, hy_conv_w[e].reshape(3, HY_PROJ), hy_conv_b[e].reshape(1, HY_PROJ))
            convs = []
            for L, nseq, off in ((SEQ, BATCH, 0), (DEC_SEQ, DEC_BATCH, TC)):
                feats, mats = hy_consts[L]
                kre, kim = _hy_filter(L, feats, mats, hy_f_w1[e], hy_f_b1[e], hy_f_freq[e], hy_f_w2[e],
                                      hy_f_b2[e], hy_f_w3[e])
                convs.append(_hy_conv(vgb, mats, kre, kim, L, nseq, off))
            conv = jnp.concatenate(convs, axis=0)
            at = jnp.concatenate([_ctx_even_attn(q, k, v),
                                  _na_attn(q, k, v, ck_na, cv_na, _na_biasmask(na_rpb[e]), e)], axis=0)
            parts = (conv, vg, x0, hy_skip[e].reshape(1, HY_W), at)
            na_k.append(k[:TC].reshape(BATCH, SEQ, NA_HEADS, HEAD))
            na_v.append(v[:TC].reshape(BATCH, SEQ, NA_HEADS, HEAD))
        else:
            o = l // 2
            q, k, v = _inproj(x, mods, l, od_in_w[o].astype(BF16), od_in_b[o],
                              ((SWA_HEADS * HEAD, True), (SWA_KV * HEAD, True), (SWA_KV * HEAD, False)),
                              rope_tabs)
            at = jnp.concatenate([_ctx_odd_attn(swa_sink[o], q, k, v),
                                  _swa_attn(swa_sink[o], q, k, v, ck_swa, cv_swa, o)], axis=0)
            parts = (at,)
            swa_k.append(k[:TC].reshape(BATCH, SEQ, SWA_KV, HEAD))
            swa_v.append(v[:TC].reshape(BATCH, SEQ, SWA_KV, HEAD))
        x1, xm, logits = _post1(x, mods, l, parts, w_out, mix_out_b[l], ln_g[l, 0], ln_b[l, 0], rw_t,
                                router_b[l], l % 2 == 0)
        x = _moe_and_norm(x1, xm, logits, mods, l, ln_g, ln_b, exp_gu_w, exp_gu_b, exp_dn_w, exp_dn_b)

    return (x[:TC].reshape(BATCH, SEQ, D), x[TC:].reshape(DEC_BATCH, DEC_SEQ, D),
            jnp.stack(na_k, axis=1), jnp.stack(na_v, axis=1), jnp.stack(swa_k, axis=1), jnp.stack(swa_v, axis=1))
```

```python
import functools
import math

import jax
import jax.numpy as jnp
from jax import lax
from jax.experimental import pallas as pl
from jax.experimental.pallas import tpu as pltpu

F32 = jnp.float32
BF16 = jnp.bfloat16
I32 = jnp.int32
HI = lax.Precision.HIGHEST

D = 1024
DEPTH = 4
BATCH, SEQ = 16, 256
DEC_BATCH, DEC_SEQ = 4, 2048
PAST = 256
GRID_W = 64
HEAD = 64
HY_W = 512
HY_PROJ = 3 * HY_W
HY_BANDS = 16
HY_EMB = 1 + 2 * HY_BANDS
HY_FF = 64
NA_HEADS = 8
NA_W = NA_HEADS * HEAD
NA_KH, NA_KW = 8, 16
SWA_HEADS, SWA_KV = 16, 4
SWA_WINDOW = 128
N_EXPERTS, TOP_K = 32, 4
D_FF = 1024
SWIGLU_LIMIT = 7.0
SWIGLU_ALPHA = 1.702
DN_ALPHA = (2 * DEPTH) ** 0.25
LN_EPS = 1e-5
NEG = -1e30
ROPE_THETA = 10000.0

TC = BATCH * SEQ
TS = DEC_BATCH * DEC_SEQ
T = TC + TS
TM = 256
NT = T // TM
NCT = TC // TM
TPS = DEC_SEQ // TM
LANE = 128
MOE_BM = 256
N_SLOT_BLOCKS = T * TOP_K // MOE_BM + N_EXPERTS
N_SLOTS = N_SLOT_BLOCKS * MOE_BM
VMEM_LIMIT = 56 * 1024 * 1024
DMA_PRIORITIES = 2
ROW_DMA_UNROLL = 8


def _params(sem, vmem=None):
    return pltpu.CompilerParams(dimension_semantics=sem, vmem_limit_bytes=vmem)


def _mod_row(i):
    return jnp.where(i < NCT, 0, 1 + (i - NCT) // TPS)


SUB = 8
ROW_TILES = D // LANE


def _load_rows(ref, nrows, lead=()):
    parts = [ref[lead + (pl.ds(s, nrows, stride=ROW_TILES), slice(None))] for s in range(ROW_TILES)]
    return jnp.concatenate(parts, axis=1)


def _store_rows(ref, val):
    nrows = val.shape[0]
    for s in range(ROW_TILES):
        ref[pl.ds(s, nrows, stride=ROW_TILES), :] = val[:, s * LANE:(s + 1) * LANE]


def _adaln_body(c_ref, w_ref, b_ref, o_ref):
    c = c_ref[...]
    s = c / (1.0 + jnp.exp(-c))
    o_ref[...] = jnp.dot(s, w_ref[...], precision=HI, preferred_element_type=F32) + b_ref[...]


def _adaln(cond8, ada_w, ada_b):
    nt = 1536
    return pl.pallas_call(
        _adaln_body,
        grid=(DEPTH, 6 * D // nt),
        in_specs=[pl.BlockSpec((8, D), lambda l, j: (0, 0)),
                  pl.BlockSpec((None, D, nt), lambda l, j: (l, 0, j)),
                  pl.BlockSpec((None, 1, nt), lambda l, j: (l, 0, j))],
        out_specs=pl.BlockSpec((None, 8, nt), lambda l, j: (l, 0, j)),
        out_shape=jax.ShapeDtypeStruct((DEPTH, 8, 6 * D), F32),
        compiler_params=_params(("arbitrary", "arbitrary"), VMEM_LIMIT),
        name="adaln",
    )(cond8, ada_w, ada_b.reshape(DEPTH, 1, 6 * D))


def _rope(x, cos, sin):
    lane = lax.broadcasted_iota(I32, (1, LANE), 1)
    first = (lane & 16) == 0
    outs = []
    for j in range(x.shape[1] // LANE):
        blk = x[:, j * LANE:(j + 1) * LANE]
        partner = jnp.where(first, pltpu.roll(blk, LANE - 16, 1), pltpu.roll(blk, 16, 1))
        outs.append(blk * cos + partner * sin)
    return jnp.concatenate(outs, axis=1) if len(outs) > 1 else outs[0]


def _inproj_body(*refs, splits, rope):
    if rope:
        x_ref, m_ref, w_ref, b_ref, cos_ref, sin_ref = refs[:6]
        outs = refs[6:]
    else:
        x_ref, m_ref, w_ref, b_ref = refs[:4]
        outs = refs[4:]
    h = x_ref[...] * (1.0 + m_ref[1:2, :]) + m_ref[0:1, :]
    y = jnp.dot(h.astype(BF16), w_ref[...], preferred_element_type=F32) + b_ref[...]
    off = 0
    for o_ref, (width, do_rope) in zip(outs, splits):
        part = y[:, off:off + width]
        if do_rope:
            part = _rope(part, cos_ref[...], sin_ref[...])
        o_ref[...] = part.astype(o_ref.dtype)
        off += width


def _inproj(x, mods, l, w_bf, b, splits, rope_tabs=None):
    p = w_bf.shape[1]
    in_specs = [pl.BlockSpec((TM, D), lambda i: (i, 0)),
                pl.BlockSpec((None, None, 6, D), lambda i: (l, _mod_row(i), 0, 0)),
                pl.BlockSpec((D, p), lambda i: (0, 0)),
                pl.BlockSpec((1, p), lambda i: (0, 0))]
    args = [x, mods, w_bf, b.reshape(1, p)]
    if rope_tabs is not None:
        in_specs += [pl.BlockSpec((TM, LANE), lambda i: (i, 0))] * 2
        args += list(rope_tabs)
    return pl.pallas_call(
        functools.partial(_inproj_body, splits=splits, rope=rope_tabs is not None),
        grid=(NT,),
        in_specs=in_specs,
        out_specs=[pl.BlockSpec((TM, w), lambda i: (i, 0)) for w, _ in splits],
        out_shape=[jax.ShapeDtypeStruct((T, w), F32) for w, _ in splits],
        compiler_params=_params(("parallel",), VMEM_LIMIT),
        name="inproj",
    )(*args)


def _lo_mask():
    return lax.broadcasted_iota(I32, (1, LANE), 1) < HEAD


def _dup_half(x2, upper):
    swapped = pltpu.roll(x2, HEAD, 1)
    keep = jnp.logical_xor(_lo_mask(), upper)
    return jnp.where(keep, x2, swapped)


def _pair_attention(q2, segs, sink_lo=None, sink_hi=None):
    m_rows = q2.shape[0]
    lo = _lo_mask()
    qs = jnp.concatenate([jnp.where(lo, q2, 0.0), jnp.where(lo, 0.0, q2)], axis=0).astype(BF16)
    scale = HEAD ** -0.5
    scores = []
    for kd, _, bias in segs:
        s = lax.dot_general(qs, kd, (((1,), (1,)), ((), ())), preferred_element_type=F32) * scale
        if bias is not None:
            s = s + bias
        scores.append(s)
    mx = functools.reduce(jnp.maximum, [jnp.max(s, axis=-1, keepdims=True) for s in scores])
    sink = None
    if sink_lo is not None:
        row = lax.broadcasted_iota(I32, (2 * m_rows, 1), 0)
        sink = jnp.where(row < m_rows, sink_lo, sink_hi)
        mx = jnp.maximum(mx, sink)
    den = jnp.zeros_like(mx)
    acc = jnp.zeros((2 * m_rows, LANE), F32)
    for s, (_, vd, _) in zip(scores, segs):
        p = jnp.exp(s - mx)
        den = den + jnp.sum(p, axis=-1, keepdims=True)
        acc = acc + jnp.dot(p.astype(BF16), vd, preferred_element_type=F32)
    if sink is not None:
        den = den + jnp.exp(sink - mx)
    o = acc / den
    return jnp.where(lo, o[:m_rows], o[m_rows:])


def _ctx_even_body(q_ref, k_ref, v_ref, o_ref):
    o_ref[...] = _pair_attention(q_ref[...], [(k_ref[...].astype(BF16), v_ref[...].astype(BF16), None)]
                                 ).astype(o_ref.dtype)


def _ctx_even_attn(q, k, v):
    spec = pl.BlockSpec((SEQ, LANE), lambda b, j: (b, j))
    return pl.pallas_call(
        _ctx_even_body,
        grid=(BATCH, NA_W // LANE),
        in_specs=[spec, spec, spec],
        out_specs=spec,
        out_shape=jax.ShapeDtypeStruct((TC, NA_W), BF16),
        compiler_params=_params(("parallel", "parallel")),
        name="ctx_attn_even",
    )(q, k, v)


def _ctx_odd_body(sink_ref, q_ref, k_ref, v_ref, o_ref):
    j = pl.program_id(1)
    upper = ((j // 2) % 2) == 1
    kd = _dup_half(k_ref[...], upper).astype(BF16)
    vd = _dup_half(v_ref[...], upper).astype(BF16)
    o_ref[...] = _pair_attention(q_ref[...], [(kd, vd, None)],
                                 sink_ref[2 * j], sink_ref[2 * j + 1]).astype(o_ref.dtype)


def _ctx_odd_attn(sink, q, k, v):
    qspec = pl.BlockSpec((SEQ, LANE), lambda b, j, s: (b, j))
    kspec = pl.BlockSpec((SEQ, LANE), lambda b, j, s: (b, j // 4))
    return pl.pallas_call(
        _ctx_odd_body,
        grid_spec=pltpu.PrefetchScalarGridSpec(
            num_scalar_prefetch=1, grid=(BATCH, SWA_HEADS * HEAD // LANE),
            in_specs=[qspec, kspec, kspec], out_specs=qspec),
        out_shape=jax.ShapeDtypeStruct((TC, SWA_HEADS * HEAD), BF16),
        compiler_params=_params(("parallel", "parallel")),
        name="ctx_attn_odd",
    )(sink, q, k, v)


SWA_BLK = 128
SWA_NB = DEC_SEQ // SWA_BLK


def _swa_body(sink_ref, q_ref, kp_ref, kc_ref, kn_ref, vp_ref, vc_ref, vn_ref, ck_ref, cv_ref, o_ref):
    n = pl.program_id(1)
    g = pl.program_id(2)
    upper = (g % 2) == 1
    qi = lax.broadcasted_iota(I32, (SWA_BLK, 3 * SWA_BLK), 0)
    kj = lax.broadcasted_iota(I32, (SWA_BLK, 3 * SWA_BLK), 1)
    kpos = (n - 1) * SWA_BLK + kj
    rel = kj - SWA_BLK - qi
    ok = (jnp.abs(rel) <= SWA_WINDOW) & (kpos >= 0) & (kpos < DEC_SEQ)
    mask = jnp.where(ok, 0.0, NEG)
    mask2 = jnp.concatenate([mask, mask], axis=0)
    kloc = jnp.concatenate([kp_ref[...], kc_ref[...], kn_ref[...]], axis=0)
    vloc = jnp.concatenate([vp_ref[...], vc_ref[...], vn_ref[...]], axis=0)
    segs = [(_dup_half(kloc, upper).astype(BF16), _dup_half(vloc, upper).astype(BF16), mask2),
            (_dup_half(ck_ref[...], upper).astype(BF16), _dup_half(cv_ref[...], upper).astype(BF16), None)]
    for jj in range(2):
        h0 = 4 * g + 2 * jj
        o_ref[:, jj * LANE:(jj + 1) * LANE] = _pair_attention(
            q_ref[:, jj * LANE:(jj + 1) * LANE], segs, sink_ref[h0], sink_ref[h0 + 1]).astype(o_ref.dtype)


def _swa_attn(sink, q, k, v, ck, cv, o):
    base = TC // SWA_BLK
    qspec = pl.BlockSpec((SWA_BLK, 2 * LANE), lambda b, n, g, s: (base + b * SWA_NB + n, g))

    def kv(shift):
        return pl.BlockSpec(
            (SWA_BLK, LANE),
            lambda b, n, g, s: (base + b * SWA_NB + jnp.clip(n + shift, 0, SWA_NB - 1), g // 2))

    cspec = pl.BlockSpec((None, None, PAST, LANE), lambda b, n, g, s: (b, o, 0, g // 2))
    ospec = pl.BlockSpec((SWA_BLK, 2 * LANE), lambda b, n, g, s: (b * SWA_NB + n, g))
    return pl.pallas_call(
        _swa_body,
        grid_spec=pltpu.PrefetchScalarGridSpec(
            num_scalar_prefetch=1, grid=(DEC_BATCH, SWA_NB, SWA_KV),
            in_specs=[qspec, kv(-1), kv(0), kv(1), kv(-1), kv(0), kv(1), cspec, cspec],
            out_specs=ospec),
        out_shape=jax.ShapeDtypeStruct((TS, SWA_HEADS * HEAD), BF16),
        compiler_params=_params(("parallel", "parallel", "parallel")),
        name="swa_attn",
    )(sink, q, k, k, k, v, v, v, ck, cv)


NA_QROWS = 4
NA_TILES = DEC_SEQ // TM
NA_KTILES = 3
NA_KSTART_MAX = NA_TILES - NA_KTILES


def _na_body(q_ref, k0_ref, k1_ref, k2_ref, v0_ref, v1_ref, v2_ref, ck_ref, cv_ref, bias_ref, o_ref):
    kloc = jnp.concatenate([k0_ref[...], k1_ref[...], k2_ref[...]], axis=0).astype(BF16)
    vloc = jnp.concatenate([v0_ref[...], v1_ref[...], v2_ref[...]], axis=0).astype(BF16)
    bias = jnp.concatenate([bias_ref[0], bias_ref[1]], axis=0)
    segs = [(kloc, vloc, bias), (ck_ref[...].astype(BF16), cv_ref[...].astype(BF16), None)]
    o_ref[...] = _pair_attention(q_ref[...], segs).astype(o_ref.dtype)


def _na_attn(q, k, v, ck, cv, biasmask, e):
    base = NCT

    def kstart(i):
        return jnp.clip(i - 1, 0, NA_KSTART_MAX)

    qspec = pl.BlockSpec((TM, LANE), lambda i, j, b: (base + b * NA_TILES + i, j))

    def kv(t):
        return pl.BlockSpec((TM, LANE), lambda i, j, b: (base + b * NA_TILES + kstart(i) + t, j))

    cspec = pl.BlockSpec((None, None, PAST, LANE), lambda i, j, b: (b, e, 0, j))
    bspec = pl.BlockSpec((None, 2, TM, NA_KTILES * TM),
                         lambda i, j, b: (jnp.where(i == 0, 0, jnp.where(i == NA_TILES - 1, 2, 1)), j, 0, 0))
    ospec = pl.BlockSpec((TM, LANE), lambda i, j, b: (b * NA_TILES + i, j))
    return pl.pallas_call(
        _na_body,
        grid=(NA_TILES, NA_W // LANE, DEC_BATCH),
        in_specs=[qspec, kv(0), kv(1), kv(2), kv(0), kv(1), kv(2), cspec, cspec, bspec],
        out_specs=ospec,
        out_shape=jax.ShapeDtypeStruct((TS, NA_W), BF16),
        compiler_params=_params(("parallel", "parallel", "parallel"), VMEM_LIMIT),
        name="na_attn",
    )(q, k, k, k, v, v, v, ck, cv, biasmask)


NA_REL_ROWS = 2 * NA_KH - 1
NA_REL_COLS = 2 * NA_KW - 1


def _na_bias_body(rpb_ref, o_ref):
    h = pl.program_id(0)
    rows = DEC_SEQ // GRID_W
    c = lax.broadcasted_iota(I32, (GRID_W, LANE), 0)
    lane = lax.broadcasted_iota(I32, (GRID_W, LANE), 1)
    c2 = lane % GRID_W
    rel = c2 - c + (NA_KW - 1)
    win_lo = jnp.clip(c - NA_KW // 2, 0, GRID_W - NA_KW)
    col_ok = (c2 >= win_lo) & (c2 < win_lo + NA_KW)
    hits = [rel == j for j in range(NA_REL_COLS)]
    outside = jnp.full((GRID_W, LANE), NEG, F32)
    toep = []
    for dr in range(NA_REL_ROWS):
        base = (h * NA_REL_ROWS + dr) * NA_REL_COLS
        t = outside
        for j in range(NA_REL_COLS):
            t = jnp.where(hits[j], rpb_ref[base + j], t)
        toep.append(jnp.where(col_ok, t, NEG))
    first_half = lane < GRID_W
    for pi, i in enumerate((0, 1, NA_TILES - 1)):
        key_row0 = NA_QROWS * min(max(i - 1, 0), NA_KSTART_MAX)
        for a in range(NA_QROWS):
            r = NA_QROWS * i + a
            rs = min(max(r - NA_KH // 2, 0), rows - NA_KH)

            def block(kr):
                r2 = key_row0 + kr
                return toep[r2 - r + NA_KH - 1] if rs <= r2 < rs + NA_KH else outside

            for s in range(NA_KTILES * NA_QROWS // 2):
                o_ref[pi, a * GRID_W:(a + 1) * GRID_W, s * LANE:(s + 1) * LANE] = jnp.where(
                    first_half, block(2 * s), block(2 * s + 1))


def _na_biasmask(rpb):
    return pl.pallas_call(
        _na_bias_body,
        grid_spec=pltpu.PrefetchScalarGridSpec(
            num_scalar_prefetch=1, grid=(NA_HEADS,), in_specs=[],
            out_specs=pl.BlockSpec((3, None, TM, NA_KTILES * TM), lambda h, s: (0, h, 0, 0))),
        out_shape=jax.ShapeDtypeStruct((3, NA_HEADS, TM, NA_KTILES * TM), F32),
        compiler_params=_params(("parallel",), VMEM_LIMIT),
        name="na_bias",
    )(rpb.reshape(-1))


def _hy_gate_body(u_ref, up_ref, un_ref, w_ref, b_ref, x0_ref, vg_ref, vgb_ref):
    i = pl.program_id(0)
    ti = (i - NCT) % TPS
    first = (i < NCT) | (ti == 0)
    last = (i < NCT) | (ti == TPS - 1)
    u = u_ref[...]
    row = lax.broadcasted_iota(I32, (TM, 1), 0)
    prev_row = jnp.where(first, 0.0, up_ref[7:8, :])
    next_row = jnp.where(last, 0.0, un_ref[0:1, :])
    before = jnp.where(row == 0, prev_row, pltpu.roll(u, 1, 0))
    after = jnp.where(row == TM - 1, next_row, pltpu.roll(u, TM - 1, 0))
    uc = w_ref[0:1, :] * before + w_ref[1:2, :] * u + w_ref[2:3, :] * after + b_ref[...]
    x0_ref[...] = uc[:, :HY_W]
    vg = uc[:, 2 * HY_W:] * uc[:, HY_W:2 * HY_W]
    vg_ref[...] = vg
    vgb_ref[...] = vg.astype(BF16)


def _hy_gate(hy, conv_w, conv_b):
    nb8 = TM // 8
    return pl.pallas_call(
        _hy_gate_body,
        grid=(NT,),
        in_specs=[pl.BlockSpec((TM, HY_PROJ), lambda i: (i, 0)),
                  pl.BlockSpec((8, HY_PROJ), lambda i: (jnp.maximum(i * nb8 - 1, 0), 0)),
                  pl.BlockSpec((8, HY_PROJ), lambda i: (jnp.minimum(i * nb8 + nb8, T // 8 - 1), 0)),
                  pl.BlockSpec((3, HY_PROJ), lambda i: (0, 0)),
                  pl.BlockSpec((1, HY_PROJ), lambda i: (0, 0))],
        out_specs=[pl.BlockSpec((TM, HY_W), lambda i: (i, 0))] * 3,
        out_shape=[jax.ShapeDtypeStruct((T, HY_W), F32), jax.ShapeDtypeStruct((T, HY_W), F32),
                   jax.ShapeDtypeStruct((T, HY_W), BF16)],
        compiler_params=_params(("parallel",)),
        name="hy_gate",
    )(hy, hy, hy, conv_w, conv_b)


def _dft_mats(L):
    n = 2 * L
    f = jnp.arange(L, dtype=I32)[:, None]
    t = jnp.arange(L, dtype=I32)[None, :]
    ph = ((f * t) % n).astype(F32) * (2.0 * math.pi / n)
    c = jnp.cos(ph)
    s = jnp.where(f == 0, jnp.where(t % 2 == 0, 1.0, -1.0), -jnp.sin(ph))
    return c.astype(BF16), s.astype(BF16), s.T.astype(BF16)


def _hy_features(L):
    nn = jnp.arange(L, dtype=F32)[:, None]
    t = jnp.linspace(0.0, 1.0, L, dtype=F32)[:, None]
    bands = jnp.linspace(1e-4, HY_BANDS - 1, HY_BANDS, dtype=F32)[None, :]
    ang = (2.0 * math.pi / L) * nn * bands
    z = jnp.concatenate([t, jnp.cos(ang), -jnp.sin(ang)], axis=-1)
    z = jnp.pad(z, ((0, 0), (0, 40 - HY_EMB)))
    deltas = jnp.abs(jnp.linspace(math.log(1e-2) / 1.5, math.log(1e-2) / 0.3, HY_W, dtype=F32))
    decay = jnp.exp(-t * deltas)
    bwd = jnp.where(jnp.arange(L)[:, None] == 0, 0.0, decay)
    return z, jnp.concatenate([decay, bwd], axis=1)


def _hy_filter_body(z_ref, w1_ref, b1_ref, fr_ref, w2_ref, b2_ref, w3_ref, dec_ref, c_ref, s_ref,
                    kre_ref, kim_ref, h_scr, *, L, fb):
    f = pl.program_id(0)

    @pl.when(f == 0)
    def _():
        fr = fr_ref[...]
        h = jnp.sin(fr * (jnp.dot(z_ref[...], w1_ref[...], precision=HI, preferred_element_type=F32)
                          + b1_ref[...]))
        h = jnp.sin(fr * (jnp.dot(h, w2_ref[...], precision=HI, preferred_element_type=F32) + b2_ref[...]))
        h = jnp.dot(h, w3_ref[...], precision=HI, preferred_element_type=F32) * dec_ref[...]
        h_scr[...] = h.astype(BF16)

    hb = h_scr[...]
    a_re = jnp.dot(c_ref[...], hb, preferred_element_type=F32)
    a_im = jnp.dot(s_ref[...], hb, preferred_element_type=F32)
    grow = f * fb + lax.broadcasted_iota(I32, (fb, 1), 0)
    packed = grow == 0
    sc = jnp.where(packed, 1.0 / (2 * L), 2.0 / (2 * L))
    kre_ref[...] = (a_re[:, :HY_W] + a_re[:, HY_W:]) * sc
    kim_ref[...] = (a_im[:, :HY_W] + jnp.where(packed, 1.0, -1.0) * a_im[:, HY_W:]) * sc


def _hy_filter(L, feats, mats, w1, b1, freq, w2, b2, w3):
    z, dec = feats
    c, s, _ = mats
    fb = min(L, 512)
    full = lambda shape: pl.BlockSpec(shape, lambda f: (0,) * len(shape))
    w1p = jnp.pad(w1, ((0, 40 - HY_EMB), (0, 0)))
    return pl.pallas_call(
        functools.partial(_hy_filter_body, L=L, fb=fb),
        grid=(L // fb,),
        in_specs=[full((L, 40)), full((40, HY_FF)), full((1, HY_FF)), full((1, HY_FF)),
                  full((HY_FF, HY_FF)), full((1, HY_FF)), full((HY_FF, 2 * HY_W)), full((L, 2 * HY_W)),
                  pl.BlockSpec((fb, L), lambda f: (f, 0)), pl.BlockSpec((fb, L), lambda f: (f, 0))],
        out_specs=[pl.BlockSpec((fb, HY_W), lambda f: (f, 0))] * 2,
        out_shape=[jax.ShapeDtypeStruct((L, HY_W), F32)] * 2,
        scratch_shapes=[pltpu.VMEM((L, 2 * HY_W), BF16)],
        compiler_params=_params(("arbitrary",), VMEM_LIMIT),
        name="hy_filter",
    )(z, w1p, b1.reshape(1, -1), freq.reshape(1, -1), w2, b2.reshape(1, -1), w3, dec, c, s)


def _hy_conv_body(v_ref, cr_ref, sr_ref, cc_ref, sc_ref, kre_ref, kim_ref, o_ref, *, fb):
    f = pl.program_id(1)
    vb = v_ref[...]
    vre = jnp.dot(cr_ref[...], vb, preferred_element_type=F32)
    vim = jnp.dot(sr_ref[...], vb, preferred_element_type=F32)
    kre = kre_ref[...]
    kim = kim_ref[...]
    grow = f * fb + lax.broadcasted_iota(I32, (fb, 1), 0)
    packed = grow == 0
    yre = vre * kre - jnp.where(packed, 0.0, vim * kim)
    yim = jnp.where(packed, vim * kim, vre * kim + vim * kre)
    part = (jnp.dot(cc_ref[...], yre.astype(BF16), preferred_element_type=F32)
            + jnp.dot(sc_ref[...], yim.astype(BF16), preferred_element_type=F32))

    @pl.when(f == 0)
    def _():
        o_ref[...] = part

    @pl.when(f > 0)
    def _():
        o_ref[...] += part


def _hy_conv(vgb, mats, kre, kim, L, nseq, row_off):
    c, s, st = mats
    fb = min(L, 256)
    off = row_off // L
    return pl.pallas_call(
        functools.partial(_hy_conv_body, fb=fb),
        grid=(nseq, L // fb),
        in_specs=[pl.BlockSpec((L, HY_W), lambda b, f: (off + b, 0)),
                  pl.BlockSpec((fb, L), lambda b, f: (f, 0)),
                  pl.BlockSpec((fb, L), lambda b, f: (f, 0)),
                  pl.BlockSpec((L, fb), lambda b, f: (0, f)),
                  pl.BlockSpec((L, fb), lambda b, f: (0, f)),
                  pl.BlockSpec((fb, HY_W), lambda b, f: (f, 0)),
                  pl.BlockSpec((fb, HY_W), lambda b, f: (f, 0))],
        out_specs=pl.BlockSpec((L, HY_W), lambda b, f: (b, 0)),
        out_shape=jax.ShapeDtypeStruct((nseq * L, HY_W), F32),
        compiler_params=_params(("parallel", "arbitrary"), VMEM_LIMIT),
        name="hy_conv",
    )(vgb, c, s, c, st, kre, kim)


def _layer_norm(z, g, b):
    mu = jnp.mean(z, axis=-1, keepdims=True)
    zc = z - mu
    var = jnp.mean(zc * zc, axis=-1, keepdims=True)
    return zc * lax.rsqrt(var + LN_EPS) * g + b


def _post1_body(*refs, even):
    is_ctx = pl.program_id(0) < NCT
    if even:
        (x_ref, m_ref, convc_ref, convs_ref, vg_ref, x0_ref, skip_ref, atc_ref, ats_ref, w_ref, b_ref,
         g_ref, be_ref, rw_ref, rb_ref, x1_ref, xm_ref, lg_ref) = refs
        conv = jnp.where(is_ctx, convc_ref[...], convs_ref[...])
        yh = ((conv + vg_ref[...] * skip_ref[...]) * x0_ref[...]).astype(BF16)
        at = jnp.where(is_ctx, atc_ref[...], ats_ref[...])
        y = (jnp.dot(yh, w_ref[:HY_W, :], preferred_element_type=F32)
             + jnp.dot(at, w_ref[HY_W:, :], preferred_element_type=F32))
    else:
        (x_ref, m_ref, atc_ref, ats_ref, w_ref, b_ref, g_ref, be_ref, rw_ref, rb_ref, x1_ref, xm_ref,
         lg_ref) = refs
        at = jnp.where(is_ctx, atc_ref[...], ats_ref[...])
        y = jnp.dot(at, w_ref[...], preferred_element_type=F32)
    y = y + b_ref[...]
    x1 = _layer_norm(DN_ALPHA * x_ref[...] + m_ref[2:3, :] * y, g_ref[...], be_ref[...])
    x1_ref[...] = x1
    xm = x1 * (1.0 + m_ref[4:5, :]) + m_ref[3:4, :]
    _store_rows(xm_ref, xm)
    lg_ref[...] = lax.dot_general(rw_ref[...], xm, (((1,), (1,)), ((), ())), precision=HI,
                                  preferred_element_type=F32) + rb_ref[...]


def _post1(x, mods, l, parts, w_bf, b, g, be, rw_t, rb, even):
    row = lambda w: pl.BlockSpec((TM, w), lambda i: (i, 0))
    full = lambda shape: pl.BlockSpec(shape, lambda i: (0,) * len(shape))
    in_specs = [row(D), pl.BlockSpec((None, None, 6, D), lambda i: (l, _mod_row(i), 0, 0))]
    ctx = lambda w: pl.BlockSpec((TM, w), lambda i: (jnp.minimum(i, NCT - 1), 0))
    lat = lambda w: pl.BlockSpec((TM, w), lambda i: (jnp.maximum(i - NCT, 0), 0))
    if even:
        in_specs += [ctx(HY_W), lat(HY_W), row(HY_W), row(HY_W), full((1, HY_W)), ctx(NA_W), lat(NA_W)]
    else:
        in_specs += [ctx(D), lat(D)]
    in_specs += [full((D, D)), full((1, D)), full((1, D)), full((1, D)), full((N_EXPERTS, D)),
                 full((N_EXPERTS, 1))]
    return pl.pallas_call(
        functools.partial(_post1_body, even=even),
        grid=(NT,),
        in_specs=in_specs,
        out_specs=[row(D), pl.BlockSpec((TM * SUB, LANE), lambda i: (i, 0)),
                   pl.BlockSpec((None, N_EXPERTS, TM), lambda i: (i, 0, 0))],
        out_shape=[jax.ShapeDtypeStruct((T, D), F32), jax.ShapeDtypeStruct((T * SUB, LANE), F32),
                   jax.ShapeDtypeStruct((NT, N_EXPERTS, TM), F32)],
        compiler_params=_params(("parallel",), VMEM_LIMIT),
        name="post1",
    )(x, mods, *parts, w_bf, b.reshape(1, D), g.reshape(1, D), be.reshape(1, D), rw_t,
      rb.reshape(N_EXPERTS, 1))


def _route_body(lg_ref, slot_ref, gate_ref, pe_ref, pd_ref, be_ref, eidx_scr, rank_scr):
    eio = lax.broadcasted_iota(I32, (N_EXPERTS, TM), 0)
    ri = lax.broadcasted_iota(I32, (TM, TM), 0)
    ci = lax.broadcasted_iota(I32, (TM, TM), 1)
    upper = jnp.where(ri < ci, 1.0, 0.0).astype(BF16)

    def tile(i, carry):
        lg = lg_ref[i]
        sel = jnp.zeros((N_EXPERTS, TM), F32)
        vals, hots = [], []
        for k in range(TOP_K):
            mx = jnp.max(lg, axis=0, keepdims=True)
            idx = jnp.min(jnp.where(lg == mx, eio, N_EXPERTS), axis=0, keepdims=True)
            hot = eio == idx
            lg = jnp.where(hot, -jnp.inf, lg)
            sel = sel + jnp.where(hot, 1.0, 0.0)
            vals.append(mx)
            hots.append(hot)
            eidx_scr[i, k:k + 1, :] = idx
        rank = jnp.dot(sel.astype(BF16), upper, preferred_element_type=F32) + carry
        ex = [jnp.exp(v - vals[0]) for v in vals]
        den = ex[0] + ex[1] + ex[2] + ex[3]
        for k in range(TOP_K):
            gate_ref[i, k:k + 1, :] = ex[k] / den
            rank_scr[i, k:k + 1, :] = jnp.sum(jnp.where(hots[k], rank, 0.0), axis=0, keepdims=True)
        return carry + jnp.sum(sel, axis=1, keepdims=True)

    counts = lax.fori_loop(0, NT, tile, jnp.zeros((N_EXPERTS, 1), F32))
    padded = jnp.floor((counts + (MOE_BM - 1)) * (1.0 / MOE_BM)) * MOE_BM
    e_r = lax.broadcasted_iota(I32, (N_EXPERTS, N_EXPERTS), 0)
    e_c = lax.broadcasted_iota(I32, (N_EXPERTS, N_EXPERTS), 1)
    incl = jnp.where(e_c <= e_r, 1.0, 0.0)
    padded_b = jnp.broadcast_to(padded, (N_EXPERTS, LANE))
    pad_end = jnp.dot(incl, padded_b, precision=HI, preferred_element_type=F32)
    pad_start = pad_end[:, 0:1] - padded
    pe_ref[...] = pad_end.astype(I32)
    pd_ref[...] = padded_b.astype(I32)
    blk0 = (lax.broadcasted_iota(I32, (1, TM), 1) * MOE_BM).astype(F32)
    nle = jnp.sum(jnp.where(pad_end[:, 0:1] <= blk0, 1.0, 0.0), axis=0, keepdims=True)
    be_ref[...] = jnp.minimum(nle, N_EXPERTS - 1.0).astype(I32)

    def place(i, c):
        for k in range(TOP_K):
            hot = eio == eidx_scr[i, k:k + 1, :]
            start = jnp.sum(jnp.where(hot, pad_start, 0.0), axis=0, keepdims=True)
            slot_ref[i, k:k + 1, :] = (start + rank_scr[i, k:k + 1, :]).astype(I32)
        return c

    lax.fori_loop(0, NT, place, 0)


def _route(logits):
    return pl.pallas_call(
        _route_body,
        out_shape=[jax.ShapeDtypeStruct((NT, TOP_K, TM), I32), jax.ShapeDtypeStruct((NT, TOP_K, TM), F32),
                   jax.ShapeDtypeStruct((N_EXPERTS, LANE), I32), jax.ShapeDtypeStruct((N_EXPERTS, LANE), I32),
                   jax.ShapeDtypeStruct((1, TM), I32)],
        scratch_shapes=[pltpu.VMEM((NT, TOP_K, TM), I32), pltpu.VMEM((NT, TOP_K, TM), F32)],
        compiler_params=_params(None, VMEM_LIMIT),
        name="route",
    )(logits)


def _dispatch_body(slot_ref, pe_ref, pd_ref, xm_ref, xb_ref, zero_scr, sem):
    i = pl.program_id(0)

    def fill_block(b):
        start = b * (MOE_BM * SUB)
        if not isinstance(b, int):
            start = pl.multiple_of(start, MOE_BM * SUB)
        return pltpu.make_async_copy(zero_scr, xb_ref.at[pl.ds(start, MOE_BM * SUB)], sem)

    @pl.when(i == 0)
    def _():
        zero_scr[...] = jnp.zeros_like(zero_scr)
        n_used = pe_ref[N_EXPERTS - 1] // MOE_BM
        for start_or_wait in (True, False):
            for e in range(N_EXPERTS):
                @pl.when(pd_ref[e] > 0)
                def _():
                    cp = fill_block(pe_ref[e] // MOE_BM - 1)
                    cp.start() if start_or_wait else cp.wait()
            for b in range(T * TOP_K // MOE_BM, N_SLOT_BLOCKS):
                @pl.when(b >= n_used)
                def _():
                    cp = fill_block(b)
                    cp.start() if start_or_wait else cp.wait()

    def row_copy(r, k):
        s = slot_ref[i * (TOP_K * TM) + k * TM + r]
        return pltpu.make_async_copy(xm_ref.at[pl.ds(pl.multiple_of(r * SUB, SUB), SUB)],
                                     xb_ref.at[pl.ds(pl.multiple_of(s * SUB, SUB), SUB)], sem)

    def issue(r, c):
        for k in range(TOP_K):
            row_copy(r, k).start(priority=k % DMA_PRIORITIES)
        return c

    def drain(r, c):
        for k in range(TOP_K):
            row_copy(r, k).wait()
        return c

    lax.fori_loop(0, TM, issue, 0, unroll=ROW_DMA_UNROLL)
    lax.fori_loop(0, TM, drain, 0, unroll=ROW_DMA_UNROLL)


def _dispatch(slot_flat, pe, pd, xm):
    return pl.pallas_call(
        _dispatch_body,
        grid_spec=pltpu.PrefetchScalarGridSpec(
            num_scalar_prefetch=3, grid=(NT,),
            in_specs=[pl.BlockSpec((TM * SUB, LANE), lambda i, s, a, b: (i, 0))],
            out_specs=pl.BlockSpec(memory_space=pl.ANY),
            scratch_shapes=[pltpu.VMEM((MOE_BM * SUB, LANE), F32), pltpu.SemaphoreType.DMA]),
        out_shape=jax.ShapeDtypeStruct((N_SLOTS * SUB, LANE), F32),
        compiler_params=_params(("arbitrary",), VMEM_LIMIT),
        name="moe_dispatch",
    )(slot_flat, pe, pd, xm)


def _expert_body(be_ref, nu_ref, xb_ref, guw_ref, gub_ref, dnw_ref, dnb_ref, yb_ref, gu_scr, dn_scr):
    i = pl.program_id(0)

    @pl.when(i >= nu_ref[0])
    def _():
        yb_ref[...] = jnp.zeros_like(yb_ref)

    @pl.when(i < nu_ref[0])
    def _():
        prev = be_ref[jnp.maximum(i - 1, 0)]

        @pl.when((i == 0) | (be_ref[i] != prev))
        def _():
            gu_scr[...] = guw_ref[...].astype(BF16)
            dn_scr[...] = dnw_ref[...].astype(BF16)

        xb = _load_rows(xb_ref, MOE_BM).astype(BF16)
        hgu = jnp.dot(xb, gu_scr[...], preferred_element_type=F32) + gub_ref[...]
        g = jnp.minimum(hgu[:, :D_FF], SWIGLU_LIMIT)
        lin = jnp.clip(hgu[:, D_FF:], -SWIGLU_LIMIT, SWIGLU_LIMIT)
        act = (lin + 1.0) * (g / (1.0 + jnp.exp(-SWIGLU_ALPHA * g)))
        _store_rows(yb_ref, jnp.dot(act.astype(BF16), dn_scr[...], preferred_element_type=F32) + dnb_ref[...])


def _experts(block_e, n_used, xb, l, gu_w, gu_b, dn_w, dn_b):
    def blk(i, be, nu):
        return jnp.minimum(i, nu[0] - 1)

    return pl.pallas_call(
        _expert_body,
        grid_spec=pltpu.PrefetchScalarGridSpec(
            num_scalar_prefetch=2, grid=(N_SLOT_BLOCKS,),
            in_specs=[pl.BlockSpec((MOE_BM * SUB, LANE), lambda i, be, nu: (blk(i, be, nu), 0)),
                      pl.BlockSpec((None, None, D, 2 * D_FF), lambda i, be, nu: (l, be[blk(i, be, nu)], 0, 0)),
                      pl.BlockSpec((None, None, 1, 2 * D_FF), lambda i, be, nu: (l, be[blk(i, be, nu)], 0, 0)),
                      pl.BlockSpec((None, None, D_FF, D), lambda i, be, nu: (l, be[blk(i, be, nu)], 0, 0)),
                      pl.BlockSpec((None, None, 1, D), lambda i, be, nu: (l, be[blk(i, be, nu)], 0, 0))],
            out_specs=pl.BlockSpec((MOE_BM * SUB, LANE), lambda i, be, nu: (i, 0)),
            scratch_shapes=[pltpu.VMEM((D, 2 * D_FF), BF16), pltpu.VMEM((D_FF, D), BF16)]),
        out_shape=jax.ShapeDtypeStruct((N_SLOTS * SUB, LANE), F32),
        compiler_params=_params(("arbitrary",), VMEM_LIMIT),
        name="moe_experts",
    )(block_e, n_used, xb, gu_w, gu_b.reshape(DEPTH, N_EXPERTS, 1, 2 * D_FF), dn_w,
      dn_b.reshape(DEPTH, N_EXPERTS, 1, D))


def _post2_body(slot_ref, x1_ref, gt_ref, m_ref, g_ref, be_ref, yb_ref, x2_ref, y_scr, sem):
    i = pl.program_id(0)
    buf = i % 2

    def row_copy(t, r, k):
        s = slot_ref[t * (TOP_K * TM) + k * TM + r]
        return pltpu.make_async_copy(yb_ref.at[pl.ds(pl.multiple_of(s * SUB, SUB), SUB)],
                                     y_scr.at[t % 2, k, pl.ds(pl.multiple_of(r * SUB, SUB), SUB)],
                                     sem.at[t % 2])

    def fetch(t):
        def issue(r, c):
            for k in range(TOP_K):
                row_copy(t, r, k).start(priority=k % DMA_PRIORITIES)
            return c
        lax.fori_loop(0, TM, issue, 0, unroll=ROW_DMA_UNROLL)

    @pl.when(i == 0)
    def _():
        fetch(i)

    @pl.when(i + 1 < NT)
    def _():
        fetch(i + 1)

    def drain(r, c):
        for k in range(TOP_K):
            row_copy(i, r, k).wait()
        return c

    lax.fori_loop(0, TM, drain, 0, unroll=ROW_DMA_UNROLL)
    f = _load_rows(y_scr, TM, (buf, 0)) * gt_ref[:, 0:1]
    for k in range(1, TOP_K):
        f = f + _load_rows(y_scr, TM, (buf, k)) * gt_ref[:, k:k + 1]
    x2_ref[...] = _layer_norm(DN_ALPHA * x1_ref[...] + m_ref[5:6, :] * f, g_ref[...], be_ref[...])


def _post2(slot_flat, x1, gates_t, mods, l, g, be, yb):
    full = lambda shape: pl.BlockSpec(shape, lambda i, s: (0,) * len(shape))
    return pl.pallas_call(
        _post2_body,
        grid_spec=pltpu.PrefetchScalarGridSpec(
            num_scalar_prefetch=1, grid=(NT,),
            in_specs=[pl.BlockSpec((TM, D), lambda i, s: (i, 0)),
                      pl.BlockSpec((TM, TOP_K), lambda i, s: (i, 0)),
                      pl.BlockSpec((None, None, 6, D), lambda i, s: (l, _mod_row(i), 0, 0)),
                      full((1, D)), full((1, D)),
                      pl.BlockSpec(memory_space=pl.ANY)],
            out_specs=pl.BlockSpec((TM, D), lambda i, s: (i, 0)),
            scratch_shapes=[pltpu.VMEM((2, TOP_K, TM * SUB, LANE), F32), pltpu.SemaphoreType.DMA((2,))]),
        out_shape=jax.ShapeDtypeStruct((T, D), F32),
        compiler_params=_params(("arbitrary",), VMEM_LIMIT),
        name="post2",
    )(slot_flat, x1, gates_t, mods, g.reshape(1, D), be.reshape(1, D), yb)


def _rope_tables():
    quarter = HEAD // 4
    pos = jnp.arange(DEC_SEQ)
    rows = (pos // GRID_W).astype(F32)[:, None]
    cols = (pos % GRID_W).astype(F32)[:, None]
    lane = jnp.arange(LANE)[None, :]
    d = lane % HEAD
    inv = ROPE_THETA ** (-(d % quarter).astype(F32) / quarter)
    ang = jnp.where(d < HEAD // 2, rows, cols) * inv
    sign = jnp.where((d // quarter) % 2 == 0, -1.0, 1.0)
    cos = jnp.concatenate([jnp.ones((TC, LANE), F32), jnp.tile(jnp.cos(ang), (DEC_BATCH, 1))], axis=0)
    sin = jnp.concatenate([jnp.zeros((TC, LANE), F32), jnp.tile(sign * jnp.sin(ang), (DEC_BATCH, 1))], axis=0)
    return cos, sin


def _moe_and_norm(x1, xm, logits, mods, l, ln_g, ln_b, exp_gu_w, exp_gu_b, exp_dn_w, exp_dn_b):
    slot, gates, pe, pd, be = _route(logits)
    slot_flat = slot.reshape(-1)
    pe1 = pe[:, 0]
    xb = _dispatch(slot_flat, pe1, pd[:, 0], xm)
    n_used = pe1[N_EXPERTS - 1:] // MOE_BM
    yb = _experts(be.reshape(-1), n_used, xb, l, exp_gu_w, exp_gu_b, exp_dn_w, exp_dn_b)
    gates_t = gates.transpose(0, 2, 1).reshape(T, TOP_K)
    return _post2(slot_flat, x1, gates_t, mods, l, ln_g[l, 1], ln_b[l, 1], yb)


def kernel(x_prompt, x_sample, cache_na_k, cache_na_v, cache_swa_k, cache_swa_v, c, c_ctx, ada_w, ada_b, ln_g, ln_b, ev_in_w, ev_in_b, hy_conv_w, hy_conv_b, hy_f_w1, hy_f_b1, hy_f_freq, hy_f_w2, hy_f_b2, hy_f_w3, hy_skip, na_rpb, od_in_w, od_in_b, swa_sink, mix_out_w, mix_out_b, router_w, router_b, exp_gu_w, exp_gu_b, exp_dn_w, exp_dn_b):
    x = jnp.concatenate([x_prompt.reshape(TC, D), x_sample.reshape(TS, D)], axis=0)
    cond8 = jnp.concatenate([c_ctx[None, :], c, jnp.zeros((8 - 1 - DEC_BATCH, D), F32)], axis=0)
    mods = _adaln(cond8, ada_w, ada_b).reshape(DEPTH, 8, 6, D)

    rope_tabs = _rope_tables()
    hy_consts = {L: (_hy_features(L), _dft_mats(L)) for L in (SEQ, DEC_SEQ)}
    ck_na = cache_na_k.reshape(DEC_BATCH, -1, PAST, NA_W)
    cv_na = cache_na_v.reshape(DEC_BATCH, -1, PAST, NA_W)
    ck_swa = cache_swa_k.reshape(DEC_BATCH, -1, PAST, SWA_KV * HEAD)
    cv_swa = cache_swa_v.reshape(DEC_BATCH, -1, PAST, SWA_KV * HEAD)

    na_k, na_v, swa_k, swa_v = [], [], [], []
    for l in range(DEPTH):
        w_out = mix_out_w[l].astype(BF16)
        rw_t = router_w[l].T
        if l % 2 == 0:
            e = l // 2
            hy, q, k, v = _inproj(x, mods, l, ev_in_w[e].astype(BF16), ev_in_b[e],
                                  ((HY_PROJ, False), (NA_W, False), (NA_W, False), (NA_W, False)))
            x0, vg, vgb = _hy_gate(hy, hy_conv_w[e].reshape(3, HY_PROJ), hy_conv_b[e].reshape(1, HY_PROJ))
            convs = []
            for L, nseq, off in ((SEQ, BATCH, 0), (DEC_SEQ, DEC_BATCH, TC)):
                feats, mats = hy_consts[L]
                kre, kim = _hy_filter(L, feats, mats, hy_f_w1[e], hy_f_b1[e], hy_f_freq[e], hy_f_w2[e],
                                      hy_f_b2[e], hy_f_w3[e])
                convs.append(_hy_conv(vgb, mats, kre, kim, L, nseq, off))
            parts = (convs[0], convs[1], vg, x0, hy_skip[e].reshape(1, HY_W), _ctx_even_attn(q, k, v),
                     _na_attn(q, k, v, ck_na, cv_na, _na_biasmask(na_rpb[e]), e))
            na_k.append(k[:TC].reshape(BATCH, SEQ, NA_HEADS, HEAD))
            na_v.append(v[:TC].reshape(BATCH, SEQ, NA_HEADS, HEAD))
        else:
            o = l // 2
            q, k, v = _inproj(x, mods, l, od_in_w[o].astype(BF16), od_in_b[o],
                              ((SWA_HEADS * HEAD, True), (SWA_KV * HEAD, True), (SWA_KV * HEAD, False)),
                              rope_tabs)
            parts = (_ctx_odd_attn(swa_sink[o], q, k, v), _swa_attn(swa_sink[o], q, k, v, ck_swa, cv_swa, o))
            swa_k.append(k[:TC].reshape(BATCH, SEQ, SWA_KV, HEAD))
            swa_v.append(v[:TC].reshape(BATCH, SEQ, SWA_KV, HEAD))
        x1, xm, logits = _post1(x, mods, l, parts, w_out, mix_out_b[l], ln_g[l, 0], ln_b[l, 0], rw_t,
                                router_b[l], l % 2 == 0)
        x = _moe_and_norm(x1, xm, logits, mods, l, ln_g, ln_b, exp_gu_w, exp_gu_b, exp_dn_w, exp_dn_b)

    return (x[:TC].reshape(BATCH, SEQ, D), x[TC:].reshape(DEC_BATCH, DEC_SEQ, D),
            jnp.stack(na_k, axis=1), jnp.stack(na_v, axis=1), jnp.stack(swa_k, axis=1), jnp.stack(swa_v, axis=1))
```

```python
import functools
import math

import jax
import jax.numpy as jnp
from jax import lax
from jax.experimental import pallas as pl
from jax.experimental.pallas import tpu as pltpu

F32 = jnp.float32
BF16 = jnp.bfloat16
I32 = jnp.int32
HI = lax.Precision.HIGHEST

D = 1024
DEPTH = 4
BATCH, SEQ = 16, 256
DEC_BATCH, DEC_SEQ = 4, 2048
PAST = 256
GRID_W = 64
HEAD = 64
HY_W = 512
HY_PROJ = 3 * HY_W
HY_BANDS = 16
HY_EMB = 1 + 2 * HY_BANDS
HY_FF = 64
NA_HEADS = 8
NA_W = NA_HEADS * HEAD
NA_KH, NA_KW = 8, 16
SWA_HEADS, SWA_KV = 16, 4
SWA_WINDOW = 128
N_EXPERTS, TOP_K = 32, 4
D_FF = 1024
SWIGLU_LIMIT = 7.0
SWIGLU_ALPHA = 1.702
DN_ALPHA = (2 * DEPTH) ** 0.25
LN_EPS = 1e-5
NEG = -1e30
ROPE_THETA = 10000.0

TC = BATCH * SEQ
TS = DEC_BATCH * DEC_SEQ
T = TC + TS
TM = 256
NT = T // TM
NCT = TC // TM
TPS = DEC_SEQ // TM
LANE = 128
MOE_BM = 512
MOE_CHAIN = 256
N_SLOT_BLOCKS = T * TOP_K // MOE_BM + N_EXPERTS
N_SLOTS = N_SLOT_BLOCKS * MOE_BM
VMEM_LIMIT = 56 * 1024 * 1024
DMA_PRIORITIES = 2
ROW_DMA_UNROLL = 4


def _params(sem, vmem=None):
    return pltpu.CompilerParams(dimension_semantics=sem, vmem_limit_bytes=vmem)


def _mod_row(i):
    return jnp.where(i < NCT, 0, 1 + (i - NCT) // TPS)


SUB = 8
ROW_TILES = D // LANE


def _load_rows(ref, nrows, lead=(), row0=0):
    parts = [ref[lead + (pl.ds(row0 * ROW_TILES + s, nrows, stride=ROW_TILES), slice(None))]
             for s in range(ROW_TILES)]
    return jnp.concatenate(parts, axis=1)


def _store_rows(ref, val, row0=0):
    nrows = val.shape[0]
    for s in range(ROW_TILES):
        ref[pl.ds(row0 * ROW_TILES + s, nrows, stride=ROW_TILES), :] = val[:, s * LANE:(s + 1) * LANE]


def _adaln_body(c_ref, w_ref, b_ref, o_ref):
    c = c_ref[...]
    s = c / (1.0 + jnp.exp(-c))
    o_ref[...] = jnp.dot(s, w_ref[...], precision=HI, preferred_element_type=F32) + b_ref[...]


def _adaln(cond8, ada_w, ada_b):
    nt = 1536
    return pl.pallas_call(
        _adaln_body,
        grid=(DEPTH, 6 * D // nt),
        in_specs=[pl.BlockSpec((8, D), lambda l, j: (0, 0)),
                  pl.BlockSpec((None, D, nt), lambda l, j: (l, 0, j)),
                  pl.BlockSpec((None, 1, nt), lambda l, j: (l, 0, j))],
        out_specs=pl.BlockSpec((None, 8, nt), lambda l, j: (l, 0, j)),
        out_shape=jax.ShapeDtypeStruct((DEPTH, 8, 6 * D), F32),
        compiler_params=_params(("arbitrary", "arbitrary"), VMEM_LIMIT),
        name="adaln",
    )(cond8, ada_w, ada_b.reshape(DEPTH, 1, 6 * D))


def _rope(x, cos, sin):
    lane = lax.broadcasted_iota(I32, (1, LANE), 1)
    first = (lane & 16) == 0
    outs = []
    for j in range(x.shape[1] // LANE):
        blk = x[:, j * LANE:(j + 1) * LANE]
        partner = jnp.where(first, pltpu.roll(blk, LANE - 16, 1), pltpu.roll(blk, 16, 1))
        outs.append(blk * cos + partner * sin)
    return jnp.concatenate(outs, axis=1) if len(outs) > 1 else outs[0]


def _inproj_body(*refs, splits, rope):
    if rope:
        x_ref, m_ref, w_ref, b_ref, cos_ref, sin_ref = refs[:6]
        outs = refs[6:]
    else:
        x_ref, m_ref, w_ref, b_ref = refs[:4]
        outs = refs[4:]
    h = x_ref[...] * (1.0 + m_ref[1:2, :]) + m_ref[0:1, :]
    y = jnp.dot(h.astype(BF16), w_ref[...], preferred_element_type=F32) + b_ref[...]
    off = 0
    for o_ref, (width, do_rope) in zip(outs, splits):
        part = y[:, off:off + width]
        if do_rope:
            part = _rope(part, cos_ref[...], sin_ref[...])
        o_ref[...] = part.astype(o_ref.dtype)
        off += width


def _inproj(x, mods, l, w_bf, b, splits, rope_tabs=None):
    p = w_bf.shape[1]
    in_specs = [pl.BlockSpec((TM, D), lambda i: (i, 0)),
                pl.BlockSpec((None, None, 6, D), lambda i: (l, _mod_row(i), 0, 0)),
                pl.BlockSpec((D, p), lambda i: (0, 0)),
                pl.BlockSpec((1, p), lambda i: (0, 0))]
    args = [x, mods, w_bf, b.reshape(1, p)]
    if rope_tabs is not None:
        in_specs += [pl.BlockSpec((TM, LANE), lambda i: (i, 0))] * 2
        args += list(rope_tabs)
    return pl.pallas_call(
        functools.partial(_inproj_body, splits=splits, rope=rope_tabs is not None),
        grid=(NT,),
        in_specs=in_specs,
        out_specs=[pl.BlockSpec((TM, w), lambda i: (i, 0)) for w, _ in splits],
        out_shape=[jax.ShapeDtypeStruct((T, w), F32) for w, _ in splits],
        compiler_params=_params(("parallel",), VMEM_LIMIT),
        name="inproj",
    )(*args)


def _lo_mask():
    return lax.broadcasted_iota(I32, (1, LANE), 1) < HEAD


def _dup_half(x2, upper):
    swapped = pltpu.roll(x2, HEAD, 1)
    keep = jnp.logical_xor(_lo_mask(), upper)
    return jnp.where(keep, x2, swapped)


def _pair_attention(q2, segs, sink_lo=None, sink_hi=None):
    m_rows = q2.shape[0]
    lo = _lo_mask()
    q2 = q2 * (HEAD ** -0.5)
    qs = jnp.concatenate([jnp.where(lo, q2, 0.0), jnp.where(lo, 0.0, q2)], axis=0).astype(BF16)
    scores = []
    for kd, _, bias in segs:
        s = lax.dot_general(qs, kd, (((1,), (1,)), ((), ())), preferred_element_type=F32)
        if bias is not None:
            s = s + bias
        scores.append(s)
    mx = functools.reduce(jnp.maximum, [jnp.max(s, axis=-1, keepdims=True) for s in scores])
    sink = None
    if sink_lo is not None:
        row = lax.broadcasted_iota(I32, (2 * m_rows, 1), 0)
        sink = jnp.where(row < m_rows, sink_lo, sink_hi)
        mx = jnp.maximum(mx, sink)
    den = jnp.zeros_like(mx)
    acc = jnp.zeros((2 * m_rows, LANE), F32)
    for s, (_, vd, _) in zip(scores, segs):
        p = jnp.exp(s - mx)
        den = den + jnp.sum(p, axis=-1, keepdims=True)
        acc = acc + jnp.dot(p.astype(BF16), vd, preferred_element_type=F32)
    if sink is not None:
        den = den + jnp.exp(sink - mx)
    o = acc / den
    return jnp.where(lo, o[:m_rows], o[m_rows:])


def _ctx_even_body(q_ref, k_ref, v_ref, o_ref):
    o_ref[...] = _pair_attention(q_ref[...], [(k_ref[...].astype(BF16), v_ref[...].astype(BF16), None)]
                                 ).astype(o_ref.dtype)


def _ctx_even_attn(q, k, v):
    spec = pl.BlockSpec((SEQ, LANE), lambda b, j: (b, j))
    return pl.pallas_call(
        _ctx_even_body,
        grid=(BATCH, NA_W // LANE),
        in_specs=[spec, spec, spec],
        out_specs=spec,
        out_shape=jax.ShapeDtypeStruct((TC, NA_W), BF16),
        compiler_params=_params(("parallel", "parallel")),
        name="ctx_attn_even",
    )(q, k, v)


def _ctx_odd_body(sink_ref, q_ref, k_ref, v_ref, o_ref):
    j = pl.program_id(1)
    upper = ((j // 2) % 2) == 1
    kd = _dup_half(k_ref[...], upper).astype(BF16)
    vd = _dup_half(v_ref[...], upper).astype(BF16)
    o_ref[...] = _pair_attention(q_ref[...], [(kd, vd, None)],
                                 sink_ref[2 * j], sink_ref[2 * j + 1]).astype(o_ref.dtype)


def _ctx_odd_attn(sink, q, k, v):
    qspec = pl.BlockSpec((SEQ, LANE), lambda b, j, s: (b, j))
    kspec = pl.BlockSpec((SEQ, LANE), lambda b, j, s: (b, j // 4))
    return pl.pallas_call(
        _ctx_odd_body,
        grid_spec=pltpu.PrefetchScalarGridSpec(
            num_scalar_prefetch=1, grid=(BATCH, SWA_HEADS * HEAD // LANE),
            in_specs=[qspec, kspec, kspec], out_specs=qspec),
        out_shape=jax.ShapeDtypeStruct((TC, SWA_HEADS * HEAD), BF16),
        compiler_params=_params(("parallel", "parallel")),
        name="ctx_attn_odd",
    )(sink, q, k, v)


SWA_BLK = 128
SWA_NB = DEC_SEQ // SWA_BLK


def _swa_body(sink_ref, q_ref, kp_ref, kc_ref, kn_ref, vp_ref, vc_ref, vn_ref, ck_ref, cv_ref, o_ref):
    n = pl.program_id(1)
    g = pl.program_id(2)
    upper = (g % 2) == 1
    qi = lax.broadcasted_iota(I32, (SWA_BLK, 3 * SWA_BLK), 0)
    kj = lax.broadcasted_iota(I32, (SWA_BLK, 3 * SWA_BLK), 1)
    kpos = (n - 1) * SWA_BLK + kj
    rel = kj - SWA_BLK - qi
    ok = (jnp.abs(rel) <= SWA_WINDOW) & (kpos >= 0) & (kpos < DEC_SEQ)
    mask = jnp.where(ok, 0.0, NEG)
    mask2 = jnp.concatenate([mask, mask], axis=0)
    kloc = jnp.concatenate([kp_ref[...], kc_ref[...], kn_ref[...]], axis=0)
    vloc = jnp.concatenate([vp_ref[...], vc_ref[...], vn_ref[...]], axis=0)
    segs = [(_dup_half(kloc, upper).astype(BF16), _dup_half(vloc, upper).astype(BF16), mask2),
            (_dup_half(ck_ref[...], upper).astype(BF16), _dup_half(cv_ref[...], upper).astype(BF16), None)]
    for jj in range(2):
        h0 = 4 * g + 2 * jj
        o_ref[:, jj * LANE:(jj + 1) * LANE] = _pair_attention(
            q_ref[:, jj * LANE:(jj + 1) * LANE], segs, sink_ref[h0], sink_ref[h0 + 1]).astype(o_ref.dtype)


def _swa_attn(sink, q, k, v, ck, cv, o):
    base = TC // SWA_BLK
    qspec = pl.BlockSpec((SWA_BLK, 2 * LANE), lambda b, n, g, s: (base + b * SWA_NB + n, g))

    def kv(shift):
        return pl.BlockSpec(
            (SWA_BLK, LANE),
            lambda b, n, g, s: (base + b * SWA_NB + jnp.clip(n + shift, 0, SWA_NB - 1), g // 2))

    cspec = pl.BlockSpec((None, None, PAST, LANE), lambda b, n, g, s: (b, o, 0, g // 2))
    ospec = pl.BlockSpec((SWA_BLK, 2 * LANE), lambda b, n, g, s: (b * SWA_NB + n, g))
    return pl.pallas_call(
        _swa_body,
        grid_spec=pltpu.PrefetchScalarGridSpec(
            num_scalar_prefetch=1, grid=(DEC_BATCH, SWA_NB, SWA_KV),
            in_specs=[qspec, kv(-1), kv(0), kv(1), kv(-1), kv(0), kv(1), cspec, cspec],
            out_specs=ospec),
        out_shape=jax.ShapeDtypeStruct((TS, SWA_HEADS * HEAD), BF16),
        compiler_params=_params(("parallel", "parallel", "parallel")),
        name="swa_attn",
    )(sink, q, k, k, k, v, v, v, ck, cv)


NA_QROWS = 4
NA_TILES = DEC_SEQ // TM
NA_KTILES = 3
NA_KSTART_MAX = NA_TILES - NA_KTILES


def _na_body(q_ref, k0_ref, k1_ref, k2_ref, v0_ref, v1_ref, v2_ref, ck_ref, cv_ref, bias_ref, o_ref):
    kloc = jnp.concatenate([k0_ref[...], k1_ref[...], k2_ref[...]], axis=0).astype(BF16)
    vloc = jnp.concatenate([v0_ref[...], v1_ref[...], v2_ref[...]], axis=0).astype(BF16)
    bias = jnp.concatenate([bias_ref[0], bias_ref[1]], axis=0)
    segs = [(kloc, vloc, bias), (ck_ref[...].astype(BF16), cv_ref[...].astype(BF16), None)]
    o_ref[...] = _pair_attention(q_ref[...], segs).astype(o_ref.dtype)


def _na_attn(q, k, v, ck, cv, biasmask, e):
    base = NCT

    def kstart(i):
        return jnp.clip(i - 1, 0, NA_KSTART_MAX)

    qspec = pl.BlockSpec((TM, LANE), lambda i, j, b: (base + b * NA_TILES + i, j))

    def kv(t):
        return pl.BlockSpec((TM, LANE), lambda i, j, b: (base + b * NA_TILES + kstart(i) + t, j))

    cspec = pl.BlockSpec((None, None, PAST, LANE), lambda i, j, b: (b, e, 0, j))
    bspec = pl.BlockSpec((None, 2, TM, NA_KTILES * TM),
                         lambda i, j, b: (jnp.where(i == 0, 0, jnp.where(i == NA_TILES - 1, 2, 1)), j, 0, 0))
    ospec = pl.BlockSpec((TM, LANE), lambda i, j, b: (b * NA_TILES + i, j))
    return pl.pallas_call(
        _na_body,
        grid=(NA_TILES, NA_W // LANE, DEC_BATCH),
        in_specs=[qspec, kv(0), kv(1), kv(2), kv(0), kv(1), kv(2), cspec, cspec, bspec],
        out_specs=ospec,
        out_shape=jax.ShapeDtypeStruct((TS, NA_W), BF16),
        compiler_params=_params(("parallel", "parallel", "parallel"), VMEM_LIMIT),
        name="na_attn",
    )(q, k, k, k, v, v, v, ck, cv, biasmask)


NA_REL_ROWS = 2 * NA_KH - 1
NA_REL_COLS = 2 * NA_KW - 1


def _na_bias_body(rpb_ref, o_ref):
    h = pl.program_id(0)
    rows = DEC_SEQ // GRID_W
    c = lax.broadcasted_iota(I32, (GRID_W, LANE), 0)
    lane = lax.broadcasted_iota(I32, (GRID_W, LANE), 1)
    c2 = lane % GRID_W
    rel = c2 - c + (NA_KW - 1)
    win_lo = jnp.clip(c - NA_KW // 2, 0, GRID_W - NA_KW)
    col_ok = (c2 >= win_lo) & (c2 < win_lo + NA_KW)
    hits = [rel == j for j in range(NA_REL_COLS)]
    outside = jnp.full((GRID_W, LANE), NEG, F32)
    toep = []
    for dr in range(NA_REL_ROWS):
        base = (h * NA_REL_ROWS + dr) * NA_REL_COLS
        t = outside
        for j in range(NA_REL_COLS):
            t = jnp.where(hits[j], rpb_ref[base + j], t)
        toep.append(jnp.where(col_ok, t, NEG))
    first_half = lane < GRID_W
    for pi, i in enumerate((0, 1, NA_TILES - 1)):
        key_row0 = NA_QROWS * min(max(i - 1, 0), NA_KSTART_MAX)
        for a in range(NA_QROWS):
            r = NA_QROWS * i + a
            rs = min(max(r - NA_KH // 2, 0), rows - NA_KH)

            def block(kr):
                r2 = key_row0 + kr
                return toep[r2 - r + NA_KH - 1] if rs <= r2 < rs + NA_KH else outside

            for s in range(NA_KTILES * NA_QROWS // 2):
                o_ref[pi, a * GRID_W:(a + 1) * GRID_W, s * LANE:(s + 1) * LANE] = jnp.where(
                    first_half, block(2 * s), block(2 * s + 1))


def _na_biasmask(rpb):
    return pl.pallas_call(
        _na_bias_body,
        grid_spec=pltpu.PrefetchScalarGridSpec(
            num_scalar_prefetch=1, grid=(NA_HEADS,), in_specs=[],
            out_specs=pl.BlockSpec((3, None, TM, NA_KTILES * TM), lambda h, s: (0, h, 0, 0))),
        out_shape=jax.ShapeDtypeStruct((3, NA_HEADS, TM, NA_KTILES * TM), F32),
        compiler_params=_params(("parallel",), VMEM_LIMIT),
        name="na_bias",
    )(rpb.reshape(-1))


def _hy_gate_body(u_ref, up_ref, un_ref, w_ref, b_ref, x0_ref, vg_ref, vgb_ref):
    i = pl.program_id(0)
    ti = (i - NCT) % TPS
    first = (i < NCT) | (ti == 0)
    last = (i < NCT) | (ti == TPS - 1)
    u = u_ref[...]
    row = lax.broadcasted_iota(I32, (TM, 1), 0)
    prev_row = jnp.where(first, 0.0, up_ref[7:8, :])
    next_row = jnp.where(last, 0.0, un_ref[0:1, :])
    before = jnp.where(row == 0, prev_row, pltpu.roll(u, 1, 0))
    after = jnp.where(row == TM - 1, next_row, pltpu.roll(u, TM - 1, 0))
    uc = w_ref[0:1, :] * before + w_ref[1:2, :] * u + w_ref[2:3, :] * after + b_ref[...]
    x0_ref[...] = uc[:, :HY_W]
    vg = uc[:, 2 * HY_W:] * uc[:, HY_W:2 * HY_W]
    vg_ref[...] = vg
    vgb_ref[...] = vg.astype(BF16)


def _hy_gate(hy, conv_w, conv_b):
    nb8 = TM // 8
    return pl.pallas_call(
        _hy_gate_body,
        grid=(NT,),
        in_specs=[pl.BlockSpec((TM, HY_PROJ), lambda i: (i, 0)),
                  pl.BlockSpec((8, HY_PROJ), lambda i: (jnp.maximum(i * nb8 - 1, 0), 0)),
                  pl.BlockSpec((8, HY_PROJ), lambda i: (jnp.minimum(i * nb8 + nb8, T // 8 - 1), 0)),
                  pl.BlockSpec((3, HY_PROJ), lambda i: (0, 0)),
                  pl.BlockSpec((1, HY_PROJ), lambda i: (0, 0))],
        out_specs=[pl.BlockSpec((TM, HY_W), lambda i: (i, 0))] * 3,
        out_shape=[jax.ShapeDtypeStruct((T, HY_W), F32), jax.ShapeDtypeStruct((T, HY_W), F32),
                   jax.ShapeDtypeStruct((T, HY_W), BF16)],
        compiler_params=_params(("parallel",)),
        name="hy_gate",
    )(hy, hy, hy, conv_w, conv_b)


def _dft_mats(L):
    n = 2 * L
    f = jnp.arange(L, dtype=I32)[:, None]
    t = jnp.arange(L, dtype=I32)[None, :]
    ph = ((f * t) % n).astype(F32) * (2.0 * math.pi / n)
    c = jnp.cos(ph)
    s = jnp.where(f == 0, jnp.where(t % 2 == 0, 1.0, -1.0), -jnp.sin(ph))
    return c.astype(BF16), s.astype(BF16), s.T.astype(BF16)


def _hy_features(L):
    nn = jnp.arange(L, dtype=F32)[:, None]
    t = jnp.linspace(0.0, 1.0, L, dtype=F32)[:, None]
    bands = jnp.linspace(1e-4, HY_BANDS - 1, HY_BANDS, dtype=F32)[None, :]
    ang = (2.0 * math.pi / L) * nn * bands
    z = jnp.concatenate([t, jnp.cos(ang), -jnp.sin(ang)], axis=-1)
    z = jnp.pad(z, ((0, 0), (0, 40 - HY_EMB)))
    deltas = jnp.abs(jnp.linspace(math.log(1e-2) / 1.5, math.log(1e-2) / 0.3, HY_W, dtype=F32))
    decay = jnp.exp(-t * deltas)
    bwd = jnp.where(jnp.arange(L)[:, None] == 0, 0.0, decay)
    return z, jnp.concatenate([decay, bwd], axis=1)


def _hy_filter_body(z_ref, w1_ref, b1_ref, fr_ref, w2_ref, b2_ref, w3_ref, dec_ref, c_ref, s_ref,
                    kre_ref, kim_ref, h_scr, *, L, fb):
    f = pl.program_id(0)

    @pl.when(f == 0)
    def _():
        fr = fr_ref[...]
        h = jnp.sin(fr * (jnp.dot(z_ref[...], w1_ref[...], precision=HI, preferred_element_type=F32)
                          + b1_ref[...]))
        h = jnp.sin(fr * (jnp.dot(h, w2_ref[...], precision=HI, preferred_element_type=F32) + b2_ref[...]))
        h = jnp.dot(h, w3_ref[...], precision=HI, preferred_element_type=F32) * dec_ref[...]
        h_scr[...] = h.astype(BF16)

    hb = h_scr[...]
    a_re = jnp.dot(c_ref[...], hb, preferred_element_type=F32)
    a_im = jnp.dot(s_ref[...], hb, preferred_element_type=F32)
    grow = f * fb + lax.broadcasted_iota(I32, (fb, 1), 0)
    packed = grow == 0
    sc = jnp.where(packed, 1.0 / (2 * L), 2.0 / (2 * L))
    kre_ref[...] = (a_re[:, :HY_W] + a_re[:, HY_W:]) * sc
    kim_ref[...] = (a_im[:, :HY_W] + jnp.where(packed, 1.0, -1.0) * a_im[:, HY_W:]) * sc


def _hy_filter(L, feats, mats, w1, b1, freq, w2, b2, w3):
    z, dec = feats
    c, s, _ = mats
    fb = min(L, 512)
    full = lambda shape: pl.BlockSpec(shape, lambda f: (0,) * len(shape))
    w1p = jnp.pad(w1, ((0, 40 - HY_EMB), (0, 0)))
    return pl.pallas_call(
        functools.partial(_hy_filter_body, L=L, fb=fb),
        grid=(L // fb,),
        in_specs=[full((L, 40)), full((40, HY_FF)), full((1, HY_FF)), full((1, HY_FF)),
                  full((HY_FF, HY_FF)), full((1, HY_FF)), full((HY_FF, 2 * HY_W)), full((L, 2 * HY_W)),
                  pl.BlockSpec((fb, L), lambda f: (f, 0)), pl.BlockSpec((fb, L), lambda f: (f, 0))],
        out_specs=[pl.BlockSpec((fb, HY_W), lambda f: (f, 0))] * 2,
        out_shape=[jax.ShapeDtypeStruct((L, HY_W), F32)] * 2,
        scratch_shapes=[pltpu.VMEM((L, 2 * HY_W), BF16)],
        compiler_params=_params(("arbitrary",), VMEM_LIMIT),
        name="hy_filter",
    )(z, w1p, b1.reshape(1, -1), freq.reshape(1, -1), w2, b2.reshape(1, -1), w3, dec, c, s)


def _hy_conv_body(v_ref, cr_ref, sr_ref, cc_ref, sc_ref, kre_ref, kim_ref, o_ref, *, fb):
    f = pl.program_id(1)
    vb = v_ref[...]
    vre = jnp.dot(cr_ref[...], vb, preferred_element_type=F32)
    vim = jnp.dot(sr_ref[...], vb, preferred_element_type=F32)
    kre = kre_ref[...]
    kim = kim_ref[...]
    grow = f * fb + lax.broadcasted_iota(I32, (fb, 1), 0)
    packed = grow == 0
    yre = vre * kre - jnp.where(packed, 0.0, vim * kim)
    yim = jnp.where(packed, vim * kim, vre * kim + vim * kre)
    part = (jnp.dot(cc_ref[...], yre.astype(BF16), preferred_element_type=F32)
            + jnp.dot(sc_ref[...], yim.astype(BF16), preferred_element_type=F32))

    @pl.when(f == 0)
    def _():
        o_ref[...] = part

    @pl.when(f > 0)
    def _():
        o_ref[...] += part


def _hy_conv(vgb, mats, kre, kim, L, nseq, row_off):
    c, s, st = mats
    fb = min(L, 256)
    off = row_off // L
    return pl.pallas_call(
        functools.partial(_hy_conv_body, fb=fb),
        grid=(nseq, L // fb),
        in_specs=[pl.BlockSpec((L, HY_W), lambda b, f: (off + b, 0)),
                  pl.BlockSpec((fb, L), lambda b, f: (f, 0)),
                  pl.BlockSpec((fb, L), lambda b, f: (f, 0)),
                  pl.BlockSpec((L, fb), lambda b, f: (0, f)),
                  pl.BlockSpec((L, fb), lambda b, f: (0, f)),
                  pl.BlockSpec((fb, HY_W), lambda b, f: (f, 0)),
                  pl.BlockSpec((fb, HY_W), lambda b, f: (f, 0))],
        out_specs=pl.BlockSpec((L, HY_W), lambda b, f: (b, 0)),
        out_shape=jax.ShapeDtypeStruct((nseq * L, HY_W), F32),
        compiler_params=_params(("parallel", "arbitrary"), VMEM_LIMIT),
        name="hy_conv",
    )(vgb, c, s, c, st, kre, kim)


def _layer_norm(z, g, b):
    mu = jnp.mean(z, axis=-1, keepdims=True)
    zc = z - mu
    var = jnp.mean(zc * zc, axis=-1, keepdims=True)
    return zc * lax.rsqrt(var + LN_EPS) * g + b


def _post1_body(*refs, even):
    is_ctx = pl.program_id(0) < NCT
    if even:
        (x_ref, m_ref, convc_ref, convs_ref, vg_ref, x0_ref, skip_ref, atc_ref, ats_ref, w_ref, b_ref,
         g_ref, be_ref, rw_ref, rb_ref, x1_ref, xm_ref, lg_ref) = refs
        conv = jnp.where(is_ctx, convc_ref[...], convs_ref[...])
        yh = ((conv + vg_ref[...] * skip_ref[...]) * x0_ref[...]).astype(BF16)
        at = jnp.where(is_ctx, atc_ref[...], ats_ref[...])
        y = (jnp.dot(yh, w_ref[:HY_W, :], preferred_element_type=F32)
             + jnp.dot(at, w_ref[HY_W:, :], preferred_element_type=F32))
    else:
        (x_ref, m_ref, atc_ref, ats_ref, w_ref, b_ref, g_ref, be_ref, rw_ref, rb_ref, x1_ref, xm_ref,
         lg_ref) = refs
        at = jnp.where(is_ctx, atc_ref[...], ats_ref[...])
        y = jnp.dot(at, w_ref[...], preferred_element_type=F32)
    y = y + b_ref[...]
    x1 = _layer_norm(DN_ALPHA * x_ref[...] + m_ref[2:3, :] * y, g_ref[...], be_ref[...])
    x1_ref[...] = x1
    xm = x1 * (1.0 + m_ref[4:5, :]) + m_ref[3:4, :]
    _store_rows(xm_ref, xm)
    lg_ref[...] = lax.dot_general(rw_ref[...], xm, (((1,), (1,)), ((), ())), precision=HI,
                                  preferred_element_type=F32) + rb_ref[...]


def _post1(x, mods, l, parts, w_bf, b, g, be, rw_t, rb, even):
    row = lambda w: pl.BlockSpec((TM, w), lambda i: (i, 0))
    full = lambda shape: pl.BlockSpec(shape, lambda i: (0,) * len(shape))
    in_specs = [row(D), pl.BlockSpec((None, None, 6, D), lambda i: (l, _mod_row(i), 0, 0))]
    ctx = lambda w: pl.BlockSpec((TM, w), lambda i: (jnp.minimum(i, NCT - 1), 0))
    lat = lambda w: pl.BlockSpec((TM, w), lambda i: (jnp.maximum(i - NCT, 0), 0))
    if even:
        in_specs += [ctx(HY_W), lat(HY_W), row(HY_W), row(HY_W), full((1, HY_W)), ctx(NA_W), lat(NA_W)]
    else:
        in_specs += [ctx(D), lat(D)]
    in_specs += [full((D, D)), full((1, D)), full((1, D)), full((1, D)), full((N_EXPERTS, D)),
                 full((N_EXPERTS, 1))]
    return pl.pallas_call(
        functools.partial(_post1_body, even=even),
        grid=(NT,),
        in_specs=in_specs,
        out_specs=[row(D), pl.BlockSpec((TM * SUB, LANE), lambda i: (i, 0)),
                   pl.BlockSpec((None, N_EXPERTS, TM), lambda i: (i, 0, 0))],
        out_shape=[jax.ShapeDtypeStruct((T, D), F32), jax.ShapeDtypeStruct((T * SUB, LANE), F32),
                   jax.ShapeDtypeStruct((NT, N_EXPERTS, TM), F32)],
        compiler_params=_params(("parallel",), VMEM_LIMIT),
        name="post1",
    )(x, mods, *parts, w_bf, b.reshape(1, D), g.reshape(1, D), be.reshape(1, D), rw_t,
      rb.reshape(N_EXPERTS, 1))


def _route_body(lg_ref, slot_ref, gate_ref, pe_ref, pd_ref, be_ref, rb_ref, eidx_scr, rank_scr):
    eio = lax.broadcasted_iota(I32, (N_EXPERTS, TM), 0)
    ri = lax.broadcasted_iota(I32, (TM, TM), 0)
    ci = lax.broadcasted_iota(I32, (TM, TM), 1)
    upper = jnp.where(ri < ci, 1.0, 0.0).astype(BF16)

    def tile(i, carry):
        lg = lg_ref[i]
        sel = jnp.zeros((N_EXPERTS, TM), F32)
        vals, hots = [], []
        for k in range(TOP_K):
            mx = jnp.max(lg, axis=0, keepdims=True)
            idx = jnp.min(jnp.where(lg == mx, eio, N_EXPERTS), axis=0, keepdims=True)
            hot = eio == idx
            lg = jnp.where(hot, -jnp.inf, lg)
            sel = sel + jnp.where(hot, 1.0, 0.0)
            vals.append(mx)
            hots.append(hot)
            eidx_scr[i, k:k + 1, :] = idx
        rank = jnp.dot(sel.astype(BF16), upper, preferred_element_type=F32) + carry
        ex = [jnp.exp(v - vals[0]) for v in vals]
        den = ex[0] + ex[1] + ex[2] + ex[3]
        for k in range(TOP_K):
            gate_ref[i, k:k + 1, :] = ex[k] / den
            rank_scr[i, k:k + 1, :] = jnp.sum(jnp.where(hots[k], rank, 0.0), axis=0, keepdims=True)
        return carry + jnp.sum(sel, axis=1, keepdims=True)

    counts = lax.fori_loop(0, NT, tile, jnp.zeros((N_EXPERTS, 1), F32))
    padded = jnp.floor((counts + (MOE_BM - 1)) * (1.0 / MOE_BM)) * MOE_BM
    e_r = lax.broadcasted_iota(I32, (N_EXPERTS, N_EXPERTS), 0)
    e_c = lax.broadcasted_iota(I32, (N_EXPERTS, N_EXPERTS), 1)
    incl = jnp.where(e_c <= e_r, 1.0, 0.0)
    padded_b = jnp.broadcast_to(padded, (N_EXPERTS, LANE))
    pad_end = jnp.dot(incl, padded_b, precision=HI, preferred_element_type=F32)
    pad_start = pad_end[:, 0:1] - padded
    pe_ref[...] = pad_end.astype(I32)
    pd_ref[...] = padded_b.astype(I32)
    blk0 = (lax.broadcasted_iota(I32, (1, TM), 1) * MOE_BM).astype(F32)
    nle = jnp.sum(jnp.where(pad_end[:, 0:1] <= blk0, 1.0, 0.0), axis=0, keepdims=True)
    blk_e = jnp.minimum(nle, N_EXPERTS - 1.0)
    be_ref[...] = blk_e.astype(I32)
    mine = eio.astype(F32) == blk_e
    cnt_b = jnp.sum(jnp.where(mine, counts, 0.0), axis=0, keepdims=True)
    start_b = jnp.sum(jnp.where(mine, pad_start, 0.0), axis=0, keepdims=True)
    rb_ref[...] = jnp.clip(cnt_b - (blk0 - start_b), 0.0, float(MOE_BM)).astype(I32)

    def place(i, c):
        for k in range(TOP_K):
            hot = eio == eidx_scr[i, k:k + 1, :]
            start = jnp.sum(jnp.where(hot, pad_start, 0.0), axis=0, keepdims=True)
            slot_ref[i, k:k + 1, :] = (start + rank_scr[i, k:k + 1, :]).astype(I32)
        return c

    lax.fori_loop(0, NT, place, 0)


def _route(logits):
    return pl.pallas_call(
        _route_body,
        out_shape=[jax.ShapeDtypeStruct((NT, TOP_K, TM), I32), jax.ShapeDtypeStruct((NT, TOP_K, TM), F32),
                   jax.ShapeDtypeStruct((N_EXPERTS, LANE), I32), jax.ShapeDtypeStruct((N_EXPERTS, LANE), I32),
                   jax.ShapeDtypeStruct((1, TM), I32), jax.ShapeDtypeStruct((1, TM), I32)],
        scratch_shapes=[pltpu.VMEM((NT, TOP_K, TM), I32), pltpu.VMEM((NT, TOP_K, TM), F32)],
        compiler_params=_params(None, VMEM_LIMIT),
        name="route",
    )(logits)


def _dispatch_body(slot_ref, pe_ref, pd_ref, xm_ref, xb_ref, zero_scr, sem):
    i = pl.program_id(0)

    def fill_block(b):
        start = b * (MOE_BM * SUB)
        if not isinstance(b, int):
            start = pl.multiple_of(start, MOE_BM * SUB)
        return pltpu.make_async_copy(zero_scr, xb_ref.at[pl.ds(start, MOE_BM * SUB)], sem)

    @pl.when(i == 0)
    def _():
        zero_scr[...] = jnp.zeros_like(zero_scr)
        n_used = pe_ref[N_EXPERTS - 1] // MOE_BM
        for start_or_wait in (True, False):
            for e in range(N_EXPERTS):
                @pl.when(pd_ref[e] > 0)
                def _():
                    cp = fill_block(pe_ref[e] // MOE_BM - 1)
                    cp.start() if start_or_wait else cp.wait()
            for b in range(T * TOP_K // MOE_BM, N_SLOT_BLOCKS):
                @pl.when(b >= n_used)
                def _():
                    cp = fill_block(b)
                    cp.start() if start_or_wait else cp.wait()

    def row_copy(r, k):
        s = slot_ref[i * (TOP_K * TM) + k * TM + r]
        return pltpu.make_async_copy(xm_ref.at[pl.ds(pl.multiple_of(r * SUB, SUB), SUB)],
                                     xb_ref.at[pl.ds(pl.multiple_of(s * SUB, SUB), SUB)], sem)

    def issue(r, c):
        for k in range(TOP_K):
            row_copy(r, k).start(priority=k % DMA_PRIORITIES)
        return c

    def drain(r, c):
        for k in range(TOP_K):
            row_copy(r, k).wait()
        return c

    lax.fori_loop(0, TM, issue, 0, unroll=ROW_DMA_UNROLL)
    lax.fori_loop(0, TM, drain, 0, unroll=ROW_DMA_UNROLL)


def _dispatch(slot_flat, pe, pd, xm):
    return pl.pallas_call(
        _dispatch_body,
        grid_spec=pltpu.PrefetchScalarGridSpec(
            num_scalar_prefetch=3, grid=(NT,),
            in_specs=[pl.BlockSpec((TM * SUB, LANE), lambda i, s, a, b: (i, 0))],
            out_specs=pl.BlockSpec(memory_space=pl.ANY),
            scratch_shapes=[pltpu.VMEM((MOE_BM * SUB, LANE), F32), pltpu.SemaphoreType.DMA]),
        out_shape=jax.ShapeDtypeStruct((N_SLOTS * SUB, LANE), F32),
        compiler_params=_params(("arbitrary",), VMEM_LIMIT),
        name="moe_dispatch",
    )(slot_flat, pe, pd, xm)


def _expert_body(be_ref, nu_ref, rb_ref, xb_ref, guw_ref, gub_ref, dnw_ref, dnb_ref, yb_ref, gu_scr, dn_scr):
    i = pl.program_id(0)
    rows = rb_ref[i]

    def chain(r0):
        xb = _load_rows(xb_ref, MOE_CHAIN, row0=r0).astype(BF16)
        hgu = jnp.dot(xb, gu_scr[...], preferred_element_type=F32) + gub_ref[...]
        g = jnp.minimum(hgu[:, :D_FF], SWIGLU_LIMIT)
        lin = jnp.clip(hgu[:, D_FF:], -SWIGLU_LIMIT, SWIGLU_LIMIT)
        act = (lin + 1.0) * (g / (1.0 + jnp.exp(-SWIGLU_ALPHA * g)))
        y = jnp.dot(act.astype(BF16), dn_scr[...], preferred_element_type=F32) + dnb_ref[...]
        _store_rows(yb_ref, y, row0=r0)

    @pl.when(i >= nu_ref[0])
    def _():
        yb_ref[...] = jnp.zeros_like(yb_ref)

    @pl.when(i < nu_ref[0])
    def _():
        prev = be_ref[jnp.maximum(i - 1, 0)]

        @pl.when((i == 0) | (be_ref[i] != prev))
        def _():
            gu_scr[...] = guw_ref[...].astype(BF16)
            dn_scr[...] = dnw_ref[...].astype(BF16)

        @pl.when(rows > MOE_CHAIN)
        def _():
            for r0 in range(0, MOE_BM, MOE_CHAIN):
                chain(r0)

        @pl.when(rows <= MOE_CHAIN)
        def _():
            chain(0)
            yb_ref[MOE_CHAIN * SUB:, :] = jnp.zeros((MOE_BM * SUB - MOE_CHAIN * SUB, LANE), F32)


def _experts(block_e, n_used, block_rows, xb, l, gu_w, gu_b, dn_w, dn_b):
    def blk(i, nu):
        return jnp.minimum(i, nu[0] - 1)

    def weights(shape):
        return pl.BlockSpec((None, None) + shape, lambda i, be, nu, rb: (l, be[blk(i, nu)], 0, 0))

    return pl.pallas_call(
        _expert_body,
        grid_spec=pltpu.PrefetchScalarGridSpec(
            num_scalar_prefetch=3, grid=(N_SLOT_BLOCKS,),
            in_specs=[pl.BlockSpec((MOE_BM * SUB, LANE), lambda i, be, nu, rb: (blk(i, nu), 0)),
                      weights((D, 2 * D_FF)), weights((1, 2 * D_FF)), weights((D_FF, D)), weights((1, D))],
            out_specs=pl.BlockSpec((MOE_BM * SUB, LANE), lambda i, be, nu, rb: (i, 0)),
            scratch_shapes=[pltpu.VMEM((D, 2 * D_FF), BF16), pltpu.VMEM((D_FF, D), BF16)]),
        out_shape=jax.ShapeDtypeStruct((N_SLOTS * SUB, LANE), F32),
        compiler_params=_params(("arbitrary",), VMEM_LIMIT),
        name="moe_experts",
    )(block_e, n_used, block_rows, xb, gu_w, gu_b.reshape(DEPTH, N_EXPERTS, 1, 2 * D_FF), dn_w,
      dn_b.reshape(DEPTH, N_EXPERTS, 1, D))


def _post2_body(slot_ref, x1_ref, gt_ref, m_ref, g_ref, be_ref, yb_ref, x2_ref, y_scr, sem):
    i = pl.program_id(0)
    buf = i % 2

    def row_copy(t, r, k):
        s = slot_ref[t * (TOP_K * TM) + k * TM + r]
        return pltpu.make_async_copy(yb_ref.at[pl.ds(pl.multiple_of(s * SUB, SUB), SUB)],
                                     y_scr.at[t % 2, k, pl.ds(pl.multiple_of(r * SUB, SUB), SUB)],
                                     sem.at[t % 2])

    def fetch(t):
        def issue(r, c):
            for k in range(TOP_K):
                row_copy(t, r, k).start(priority=k % DMA_PRIORITIES)
            return c
        lax.fori_loop(0, TM, issue, 0, unroll=ROW_DMA_UNROLL)

    @pl.when(i == 0)
    def _():
        fetch(i)

    @pl.when(i + 1 < NT)
    def _():
        fetch(i + 1)

    def drain(r, c):
        for k in range(TOP_K):
            row_copy(i, r, k).wait()
        return c

    lax.fori_loop(0, TM, drain, 0, unroll=ROW_DMA_UNROLL)
    f = _load_rows(y_scr, TM, (buf, 0)) * gt_ref[:, 0:1]
    for k in range(1, TOP_K):
        f = f + _load_rows(y_scr, TM, (buf, k)) * gt_ref[:, k:k + 1]
    x2_ref[...] = _layer_norm(DN_ALPHA * x1_ref[...] + m_ref[5:6, :] * f, g_ref[...], be_ref[...])


def _post2(slot_flat, x1, gates_t, mods, l, g, be, yb):
    full = lambda shape: pl.BlockSpec(shape, lambda i, s: (0,) * len(shape))
    return pl.pallas_call(
        _post2_body,
        grid_spec=pltpu.PrefetchScalarGridSpec(
            num_scalar_prefetch=1, grid=(NT,),
            in_specs=[pl.BlockSpec((TM, D), lambda i, s: (i, 0)),
                      pl.BlockSpec((TM, TOP_K), lambda i, s: (i, 0)),
                      pl.BlockSpec((None, None, 6, D), lambda i, s: (l, _mod_row(i), 0, 0)),
                      full((1, D)), full((1, D)),
                      pl.BlockSpec(memory_space=pl.ANY)],
            out_specs=pl.BlockSpec((TM, D), lambda i, s: (i, 0)),
            scratch_shapes=[pltpu.VMEM((2, TOP_K, TM * SUB, LANE), F32), pltpu.SemaphoreType.DMA((2,))]),
        out_shape=jax.ShapeDtypeStruct((T, D), F32),
        compiler_params=_params(("arbitrary",), VMEM_LIMIT),
        name="post2",
    )(slot_flat, x1, gates_t, mods, g.reshape(1, D), be.reshape(1, D), yb)


def _rope_tables():
    quarter = HEAD // 4
    pos = jnp.arange(DEC_SEQ)
    rows = (pos // GRID_W).astype(F32)[:, None]
    cols = (pos % GRID_W).astype(F32)[:, None]
    lane = jnp.arange(LANE)[None, :]
    d = lane % HEAD
    inv = ROPE_THETA ** (-(d % quarter).astype(F32) / quarter)
    ang = jnp.where(d < HEAD // 2, rows, cols) * inv
    sign = jnp.where((d // quarter) % 2 == 0, -1.0, 1.0)
    cos = jnp.concatenate([jnp.ones((TC, LANE), F32), jnp.tile(jnp.cos(ang), (DEC_BATCH, 1))], axis=0)
    sin = jnp.concatenate([jnp.zeros((TC, LANE), F32), jnp.tile(sign * jnp.sin(ang), (DEC_BATCH, 1))], axis=0)
    return cos, sin


def _moe_and_norm(x1, xm, logits, mods, l, ln_g, ln_b, exp_gu_w, exp_gu_b, exp_dn_w, exp_dn_b):
    slot, gates, pe, pd, be, rb = _route(logits)
    slot_flat = slot.reshape(-1)
    pe1 = pe[:, 0]
    xb = _dispatch(slot_flat, pe1, pd[:, 0], xm)
    n_used = pe1[N_EXPERTS - 1:] // MOE_BM
    yb = _experts(be.reshape(-1), n_used, rb.reshape(-1), xb, l, exp_gu_w, exp_gu_b, exp_dn_w, exp_dn_b)
    gates_t = gates.transpose(0, 2, 1).reshape(T, TOP_K)
    return _post2(slot_flat, x1, gates_t, mods, l, ln_g[l, 1], ln_b[l, 1], yb)


def kernel(x_prompt, x_sample, cache_na_k, cache_na_v, cache_swa_k, cache_swa_v, c, c_ctx, ada_w, ada_b, ln_g, ln_b, ev_in_w, ev_in_b, hy_conv_w, hy_conv_b, hy_f_w1, hy_f_b1, hy_f_freq, hy_f_w2, hy_f_b2, hy_f_w3, hy_skip, na_rpb, od_in_w, od_in_b, swa_sink, mix_out_w, mix_out_b, router_w, router_b, exp_gu_w, exp_gu_b, exp_dn_w, exp_dn_b):
    x = jnp.concatenate([x_prompt.reshape(TC, D), x_sample.reshape(TS, D)], axis=0)
    cond8 = jnp.concatenate([c_ctx[None, :], c, jnp.zeros((8 - 1 - DEC_BATCH, D), F32)], axis=0)
    mods = _adaln(cond8, ada_w, ada_b).reshape(DEPTH, 8, 6, D)

    rope_tabs = _rope_tables()
    hy_consts = {L: (_hy_features(L), _dft_mats(L)) for L in (SEQ, DEC_SEQ)}
    ck_na = cache_na_k.reshape(DEC_BATCH, -1, PAST, NA_W)
    cv_na = cache_na_v.reshape(DEC_BATCH, -1, PAST, NA_W)
    ck_swa = cache_swa_k.reshape(DEC_BATCH, -1, PAST, SWA_KV * HEAD)
    cv_swa = cache_swa_v.reshape(DEC_BATCH, -1, PAST, SWA_KV * HEAD)

    na_k, na_v, swa_k, swa_v = [], [], [], []
    for l in range(DEPTH):
        w_out = mix_out_w[l].astype(BF16)
        rw_t = router_w[l].T
        if l % 2 == 0:
            e = l // 2
            hy, q, k, v = _inproj(x, mods, l, ev_in_w[e].astype(BF16), ev_in_b[e],
                                  ((HY_PROJ, False), (NA_W, False), (NA_W, False), (NA_W, False)))
            x0, vg, vgb = _hy_gate(hy, hy_conv_w[e].reshape(3, HY_PROJ), hy_conv_b[e].reshape(1, HY_PROJ))
            convs = []
            for L, nseq, off in ((SEQ, BATCH, 0), (DEC_SEQ, DEC_BATCH, TC)):
                feats, mats = hy_consts[L]
                kre, kim = _hy_filter(L, feats, mats, hy_f_w1[e], hy_f_b1[e], hy_f_freq[e], hy_f_w2[e],
                                      hy_f_b2[e], hy_f_w3[e])
                convs.append(_hy_conv(vgb, mats, kre, kim, L, nseq, off))
            parts = (convs[0], convs[1], vg, x0, hy_skip[e].reshape(1, HY_W), _ctx_even_attn(q, k, v),
                     _na_attn(q, k, v, ck_na, cv_na, _na_biasmask(na_rpb[e]), e))
            na_k.append(k[:TC].reshape(BATCH, SEQ, NA_HEADS, HEAD))
            na_v.append(v[:TC].reshape(BATCH, SEQ, NA_HEADS, HEAD))
        else:
            o = l // 2
            q, k, v = _inproj(x, mods, l, od_in_w[o].astype(BF16), od_in_b[o],
                              ((SWA_HEADS * HEAD, True), (SWA_KV * HEAD, True), (SWA_KV * HEAD, False)),
                              rope_tabs)
            parts = (_ctx_odd_attn(swa_sink[o], q, k, v), _swa_attn(swa_sink[o], q, k, v, ck_swa, cv_swa, o))
            swa_k.append(k[:TC].reshape(BATCH, SEQ, SWA_KV, HEAD))
            swa_v.append(v[:TC].reshape(BATCH, SEQ, SWA_KV, HEAD))
        x1, xm, logits = _post1(x, mods, l, parts, w_out, mix_out_b[l], ln_g[l, 0], ln_b[l, 0], rw_t,
                                router_b[l], l % 2 == 0)
        x = _moe_and_norm(x1, xm, logits, mods, l, ln_g, ln_b, exp_gu_w, exp_gu_b, exp_dn_w, exp_dn_b)

    return (x[:TC].reshape(BATCH, SEQ, D), x[TC:].reshape(DEC_BATCH, DEC_SEQ, D),
            jnp.stack(na_k, axis=1), jnp.stack(na_v, axis=1), jnp.stack(swa_k, axis=1), jnp.stack(swa_v, axis=1))
```

```python
import functools
import math

import jax
import jax.numpy as jnp
from jax import lax
from jax.experimental import pallas as pl
from jax.experimental.pallas import tpu as pltpu

F32 = jnp.float32
BF16 = jnp.bfloat16
I32 = jnp.int32
HI = lax.Precision.HIGHEST

D = 1024
DEPTH = 4
BATCH, SEQ = 16, 256
DEC_BATCH, DEC_SEQ = 4, 2048
PAST = 256
GRID_W = 64
HEAD = 64
HY_W = 512
HY_PROJ = 3 * HY_W
HY_BANDS = 16
HY_EMB = 1 + 2 * HY_BANDS
HY_FF = 64
NA_HEADS = 8
NA_W = NA_HEADS * HEAD
NA_KH, NA_KW = 8, 16
SWA_HEADS, SWA_KV = 16, 4
SWA_WINDOW = 128
N_EXPERTS, TOP_K = 32, 4
D_FF = 1024
SWIGLU_LIMIT = 7.0
SWIGLU_ALPHA = 1.702
DN_ALPHA = (2 * DEPTH) ** 0.25
LN_EPS = 1e-5
NEG = -1e30
ROPE_THETA = 10000.0

TC = BATCH * SEQ
TS = DEC_BATCH * DEC_SEQ
T = TC + TS
TM = 256
NT = T // TM
NCT = TC // TM
TPS = DEC_SEQ // TM
LANE = 128
MOE_BM = 512
MOE_CHAIN = 256
N_SLOT_BLOCKS = T * TOP_K // MOE_BM + N_EXPERTS
N_SLOTS = N_SLOT_BLOCKS * MOE_BM
VMEM_LIMIT = 56 * 1024 * 1024
DMA_PRIORITIES = 2
ROW_DMA_UNROLL = 8


def _params(sem, vmem=None):
    return pltpu.CompilerParams(dimension_semantics=sem, vmem_limit_bytes=vmem)


def _mod_row(i):
    return jnp.where(i < NCT, 0, 1 + (i - NCT) // TPS)


SUB = 8
ROW_TILES = D // LANE


def _load_rows(ref, nrows, lead=(), row0=0):
    parts = [ref[lead + (pl.ds(row0 * ROW_TILES + s, nrows, stride=ROW_TILES), slice(None))]
             for s in range(ROW_TILES)]
    return jnp.concatenate(parts, axis=1)


def _store_rows(ref, val, row0=0):
    nrows = val.shape[0]
    for s in range(ROW_TILES):
        ref[pl.ds(row0 * ROW_TILES + s, nrows, stride=ROW_TILES), :] = val[:, s * LANE:(s + 1) * LANE]


def _adaln_body(c_ref, w_ref, b_ref, o_ref):
    c = c_ref[...]
    s = c / (1.0 + jnp.exp(-c))
    o_ref[...] = jnp.dot(s, w_ref[...], precision=HI, preferred_element_type=F32) + b_ref[...]


def _adaln(cond8, ada_w, ada_b):
    nt = 1536
    return pl.pallas_call(
        _adaln_body,
        grid=(DEPTH, 6 * D // nt),
        in_specs=[pl.BlockSpec((8, D), lambda l, j: (0, 0)),
                  pl.BlockSpec((None, D, nt), lambda l, j: (l, 0, j)),
                  pl.BlockSpec((None, 1, nt), lambda l, j: (l, 0, j))],
        out_specs=pl.BlockSpec((None, 8, nt), lambda l, j: (l, 0, j)),
        out_shape=jax.ShapeDtypeStruct((DEPTH, 8, 6 * D), F32),
        compiler_params=_params(("arbitrary", "arbitrary"), VMEM_LIMIT),
        name="adaln",
    )(cond8, ada_w, ada_b.reshape(DEPTH, 1, 6 * D))


def _rope(x, cos, sin):
    lane = lax.broadcasted_iota(I32, (1, LANE), 1)
    first = (lane & 16) == 0
    outs = []
    for j in range(x.shape[1] // LANE):
        blk = x[:, j * LANE:(j + 1) * LANE]
        partner = jnp.where(first, pltpu.roll(blk, LANE - 16, 1), pltpu.roll(blk, 16, 1))
        outs.append(blk * cos + partner * sin)
    return jnp.concatenate(outs, axis=1) if len(outs) > 1 else outs[0]


def _inproj_body(*refs, splits, rope):
    if rope:
        x_ref, m_ref, w_ref, b_ref, cos_ref, sin_ref = refs[:6]
        outs = refs[6:]
    else:
        x_ref, m_ref, w_ref, b_ref = refs[:4]
        outs = refs[4:]
    h = x_ref[...] * (1.0 + m_ref[1:2, :]) + m_ref[0:1, :]
    y = jnp.dot(h.astype(BF16), w_ref[...], preferred_element_type=F32) + b_ref[...]
    off = 0
    for o_ref, (width, do_rope) in zip(outs, splits):
        part = y[:, off:off + width]
        if do_rope:
            part = _rope(part, cos_ref[...], sin_ref[...])
        o_ref[...] = part.astype(o_ref.dtype)
        off += width


def _inproj(x, mods, l, w_bf, b, splits, rope_tabs=None):
    p = w_bf.shape[1]
    in_specs = [pl.BlockSpec((TM, D), lambda i: (i, 0)),
                pl.BlockSpec((None, None, 6, D), lambda i: (l, _mod_row(i), 0, 0)),
                pl.BlockSpec((D, p), lambda i: (0, 0)),
                pl.BlockSpec((1, p), lambda i: (0, 0))]
    args = [x, mods, w_bf, b.reshape(1, p)]
    if rope_tabs is not None:
        in_specs += [pl.BlockSpec((TM, LANE), lambda i: (i, 0))] * 2
        args += list(rope_tabs)
    return pl.pallas_call(
        functools.partial(_inproj_body, splits=splits, rope=rope_tabs is not None),
        grid=(NT,),
        in_specs=in_specs,
        out_specs=[pl.BlockSpec((TM, w), lambda i: (i, 0)) for w, _ in splits],
        out_shape=[jax.ShapeDtypeStruct((T, w), F32) for w, _ in splits],
        compiler_params=_params(("parallel",), VMEM_LIMIT),
        name="inproj",
    )(*args)


def _lo_mask():
    return lax.broadcasted_iota(I32, (1, LANE), 1) < HEAD


def _dup_half(x2, upper):
    swapped = pltpu.roll(x2, HEAD, 1)
    keep = jnp.logical_xor(_lo_mask(), upper)
    return jnp.where(keep, x2, swapped)


def _pair_attention(q2, segs, sink_lo=None, sink_hi=None):
    m_rows = q2.shape[0]
    lo = _lo_mask()
    q2 = q2 * (HEAD ** -0.5)
    qs = jnp.concatenate([jnp.where(lo, q2, 0.0), jnp.where(lo, 0.0, q2)], axis=0).astype(BF16)
    scores = []
    for kd, _, bias in segs:
        s = lax.dot_general(qs, kd, (((1,), (1,)), ((), ())), preferred_element_type=F32)
        if bias is not None:
            s = s + bias
        scores.append(s)
    mx = functools.reduce(jnp.maximum, [jnp.max(s, axis=-1, keepdims=True) for s in scores])
    sink = None
    if sink_lo is not None:
        row = lax.broadcasted_iota(I32, (2 * m_rows, 1), 0)
        sink = jnp.where(row < m_rows, sink_lo, sink_hi)
        mx = jnp.maximum(mx, sink)
    den = jnp.zeros_like(mx)
    acc = jnp.zeros((2 * m_rows, LANE), F32)
    for s, (_, vd, _) in zip(scores, segs):
        p = jnp.exp(s - mx)
        den = den + jnp.sum(p, axis=-1, keepdims=True)
        acc = acc + jnp.dot(p.astype(BF16), vd, preferred_element_type=F32)
    if sink is not None:
        den = den + jnp.exp(sink - mx)
    o = acc / den
    return jnp.where(lo, o[:m_rows], o[m_rows:])


def _lanes(j, width=LANE):
    return slice(j * width, (j + 1) * width)


def _ctx_even_body(q_ref, k_ref, v_ref, o_ref):
    for j in range(NA_W // LANE):
        segs = [(k_ref[:, _lanes(j)].astype(BF16), v_ref[:, _lanes(j)].astype(BF16), None)]
        o_ref[:, _lanes(j)] = _pair_attention(q_ref[:, _lanes(j)], segs).astype(o_ref.dtype)


def _ctx_even_attn(q, k, v):
    spec = pl.BlockSpec((SEQ, NA_W), lambda b: (b, 0))
    return pl.pallas_call(
        _ctx_even_body,
        grid=(BATCH,),
        in_specs=[spec, spec, spec],
        out_specs=spec,
        out_shape=jax.ShapeDtypeStruct((TC, NA_W), BF16),
        compiler_params=_params(("parallel",)),
        name="ctx_attn_even",
    )(q, k, v)


def _ctx_odd_body(sink_ref, q_ref, k_ref, v_ref, o_ref):
    for g in range(SWA_KV):
        upper = (g % 2) == 1
        kd = _dup_half(k_ref[:, _lanes(g // 2)], upper).astype(BF16)
        vd = _dup_half(v_ref[:, _lanes(g // 2)], upper).astype(BF16)
        for j in (2 * g, 2 * g + 1):
            o_ref[:, _lanes(j)] = _pair_attention(q_ref[:, _lanes(j)], [(kd, vd, None)],
                                                  sink_ref[2 * j], sink_ref[2 * j + 1]).astype(o_ref.dtype)


def _ctx_odd_attn(sink, q, k, v):
    qspec = pl.BlockSpec((SEQ, SWA_HEADS * HEAD), lambda b, s: (b, 0))
    kspec = pl.BlockSpec((SEQ, SWA_KV * HEAD), lambda b, s: (b, 0))
    return pl.pallas_call(
        _ctx_odd_body,
        grid_spec=pltpu.PrefetchScalarGridSpec(
            num_scalar_prefetch=1, grid=(BATCH,),
            in_specs=[qspec, kspec, kspec], out_specs=qspec),
        out_shape=jax.ShapeDtypeStruct((TC, SWA_HEADS * HEAD), BF16),
        compiler_params=_params(("parallel",)),
        name="ctx_attn_odd",
    )(sink, q, k, v)


SWA_BLK = 128
SWA_NB = DEC_SEQ // SWA_BLK


def _swa_body(sink_ref, q_ref, kp_ref, kc_ref, kn_ref, vp_ref, vc_ref, vn_ref, ck_ref, cv_ref, o_ref):
    n = pl.program_id(1)
    qi = lax.broadcasted_iota(I32, (SWA_BLK, 3 * SWA_BLK), 0)
    kj = lax.broadcasted_iota(I32, (SWA_BLK, 3 * SWA_BLK), 1)
    kpos = (n - 1) * SWA_BLK + kj
    rel = kj - SWA_BLK - qi
    ok = (jnp.abs(rel) <= SWA_WINDOW) & (kpos >= 0) & (kpos < DEC_SEQ)
    mask = jnp.where(ok, 0.0, NEG)
    mask2 = jnp.concatenate([mask, mask], axis=0)
    for g in range(SWA_KV):
        upper = (g % 2) == 1
        kv = _lanes(g // 2)
        kloc = jnp.concatenate([kp_ref[:, kv], kc_ref[:, kv], kn_ref[:, kv]], axis=0)
        vloc = jnp.concatenate([vp_ref[:, kv], vc_ref[:, kv], vn_ref[:, kv]], axis=0)
        segs = [(_dup_half(kloc, upper).astype(BF16), _dup_half(vloc, upper).astype(BF16), mask2),
                (_dup_half(ck_ref[:, kv], upper).astype(BF16), _dup_half(cv_ref[:, kv], upper).astype(BF16),
                 None)]
        for j in (2 * g, 2 * g + 1):
            o_ref[:, _lanes(j)] = _pair_attention(q_ref[:, _lanes(j)], segs, sink_ref[2 * j],
                                                  sink_ref[2 * j + 1]).astype(o_ref.dtype)


def _swa_attn(sink, q, k, v, ck, cv, o):
    base = TC // SWA_BLK
    qspec = pl.BlockSpec((SWA_BLK, SWA_HEADS * HEAD), lambda b, n, s: (base + b * SWA_NB + n, 0))

    def kv(shift):
        return pl.BlockSpec(
            (SWA_BLK, SWA_KV * HEAD),
            lambda b, n, s: (base + b * SWA_NB + jnp.clip(n + shift, 0, SWA_NB - 1), 0))

    cspec = pl.BlockSpec((None, None, PAST, SWA_KV * HEAD), lambda b, n, s: (b, o, 0, 0))
    ospec = pl.BlockSpec((SWA_BLK, SWA_HEADS * HEAD), lambda b, n, s: (b * SWA_NB + n, 0))
    return pl.pallas_call(
        _swa_body,
        grid_spec=pltpu.PrefetchScalarGridSpec(
            num_scalar_prefetch=1, grid=(DEC_BATCH, SWA_NB),
            in_specs=[qspec, kv(-1), kv(0), kv(1), kv(-1), kv(0), kv(1), cspec, cspec],
            out_specs=ospec),
        out_shape=jax.ShapeDtypeStruct((TS, SWA_HEADS * HEAD), BF16),
        compiler_params=_params(("parallel", "parallel")),
        name="swa_attn",
    )(sink, q, k, k, k, v, v, v, ck, cv)


NA_QROWS = 4
NA_TILES = DEC_SEQ // TM
NA_KTILES = 3
NA_KSTART_MAX = NA_TILES - NA_KTILES


def _na_body(q_ref, k0_ref, k1_ref, k2_ref, v0_ref, v1_ref, v2_ref, ck_ref, cv_ref, bias_ref, o_ref):
    for j in range(NA_W // LANE):
        ln = _lanes(j)
        kloc = jnp.concatenate([k0_ref[:, ln], k1_ref[:, ln], k2_ref[:, ln]], axis=0).astype(BF16)
        vloc = jnp.concatenate([v0_ref[:, ln], v1_ref[:, ln], v2_ref[:, ln]], axis=0).astype(BF16)
        bias = jnp.concatenate([bias_ref[2 * j], bias_ref[2 * j + 1]], axis=0)
        segs = [(kloc, vloc, bias), (ck_ref[:, ln].astype(BF16), cv_ref[:, ln].astype(BF16), None)]
        o_ref[:, ln] = _pair_attention(q_ref[:, ln], segs).astype(o_ref.dtype)


def _na_attn(q, k, v, ck, cv, biasmask, e):
    base = NCT

    def kstart(i):
        return jnp.clip(i - 1, 0, NA_KSTART_MAX)

    qspec = pl.BlockSpec((TM, NA_W), lambda i, b: (base + b * NA_TILES + i, 0))

    def kv(t):
        return pl.BlockSpec((TM, NA_W), lambda i, b: (base + b * NA_TILES + kstart(i) + t, 0))

    cspec = pl.BlockSpec((None, None, PAST, NA_W), lambda i, b: (b, e, 0, 0))
    bspec = pl.BlockSpec((None, NA_HEADS, TM, NA_KTILES * TM),
                         lambda i, b: (jnp.where(i == 0, 0, jnp.where(i == NA_TILES - 1, 2, 1)), 0, 0, 0))
    ospec = pl.BlockSpec((TM, NA_W), lambda i, b: (b * NA_TILES + i, 0))
    return pl.pallas_call(
        _na_body,
        grid=(NA_TILES, DEC_BATCH),
        in_specs=[qspec, kv(0), kv(1), kv(2), kv(0), kv(1), kv(2), cspec, cspec, bspec],
        out_specs=ospec,
        out_shape=jax.ShapeDtypeStruct((TS, NA_W), BF16),
        compiler_params=_params(("parallel", "parallel"), VMEM_LIMIT),
        name="na_attn",
    )(q, k, k, k, v, v, v, ck, cv, biasmask)


NA_REL_ROWS = 2 * NA_KH - 1
NA_REL_COLS = 2 * NA_KW - 1


def _na_bias_body(rpb_ref, o_ref):
    h = pl.program_id(0)
    rows = DEC_SEQ // GRID_W
    c = lax.broadcasted_iota(I32, (GRID_W, LANE), 0)
    lane = lax.broadcasted_iota(I32, (GRID_W, LANE), 1)
    c2 = lane % GRID_W
    rel = c2 - c + (NA_KW - 1)
    win_lo = jnp.clip(c - NA_KW // 2, 0, GRID_W - NA_KW)
    col_ok = (c2 >= win_lo) & (c2 < win_lo + NA_KW)
    hits = [rel == j for j in range(NA_REL_COLS)]
    outside = jnp.full((GRID_W, LANE), NEG, F32)
    toep = []
    for dr in range(NA_REL_ROWS):
        base = (h * NA_REL_ROWS + dr) * NA_REL_COLS
        t = outside
        for j in range(NA_REL_COLS):
            t = jnp.where(hits[j], rpb_ref[base + j], t)
        toep.append(jnp.where(col_ok, t, NEG))
    first_half = lane < GRID_W
    for pi, i in enumerate((0, 1, NA_TILES - 1)):
        key_row0 = NA_QROWS * min(max(i - 1, 0), NA_KSTART_MAX)
        for a in range(NA_QROWS):
            r = NA_QROWS * i + a
            rs = min(max(r - NA_KH // 2, 0), rows - NA_KH)

            def block(kr):
                r2 = key_row0 + kr
                return toep[r2 - r + NA_KH - 1] if rs <= r2 < rs + NA_KH else outside

            for s in range(NA_KTILES * NA_QROWS // 2):
                o_ref[pi, a * GRID_W:(a + 1) * GRID_W, s * LANE:(s + 1) * LANE] = jnp.where(
                    first_half, block(2 * s), block(2 * s + 1))


def _na_biasmask(rpb):
    return pl.pallas_call(
        _na_bias_body,
        grid_spec=pltpu.PrefetchScalarGridSpec(
            num_scalar_prefetch=1, grid=(NA_HEADS,), in_specs=[],
            out_specs=pl.BlockSpec((3, None, TM, NA_KTILES * TM), lambda h, s: (0, h, 0, 0))),
        out_shape=jax.ShapeDtypeStruct((3, NA_HEADS, TM, NA_KTILES * TM), F32),
        compiler_params=_params(("parallel",), VMEM_LIMIT),
        name="na_bias",
    )(rpb.reshape(-1))


def _hy_gate_body(u_ref, up_ref, un_ref, w_ref, b_ref, x0_ref, vg_ref, vgb_ref):
    i = pl.program_id(0)
    ti = (i - NCT) % TPS
    first = (i < NCT) | (ti == 0)
    last = (i < NCT) | (ti == TPS - 1)
    u = u_ref[...]
    row = lax.broadcasted_iota(I32, (TM, 1), 0)
    prev_row = jnp.where(first, 0.0, up_ref[7:8, :])
    next_row = jnp.where(last, 0.0, un_ref[0:1, :])
    before = jnp.where(row == 0, prev_row, pltpu.roll(u, 1, 0))
    after = jnp.where(row == TM - 1, next_row, pltpu.roll(u, TM - 1, 0))
    uc = w_ref[0:1, :] * before + w_ref[1:2, :] * u + w_ref[2:3, :] * after + b_ref[...]
    x0_ref[...] = uc[:, :HY_W]
    vg = uc[:, 2 * HY_W:] * uc[:, HY_W:2 * HY_W]
    vg_ref[...] = vg
    vgb_ref[...] = vg.astype(BF16)


def _hy_gate(hy, conv_w, conv_b):
    nb8 = TM // 8
    return pl.pallas_call(
        _hy_gate_body,
        grid=(NT,),
        in_specs=[pl.BlockSpec((TM, HY_PROJ), lambda i: (i, 0)),
                  pl.BlockSpec((8, HY_PROJ), lambda i: (jnp.maximum(i * nb8 - 1, 0), 0)),
                  pl.BlockSpec((8, HY_PROJ), lambda i: (jnp.minimum(i * nb8 + nb8, T // 8 - 1), 0)),
                  pl.BlockSpec((3, HY_PROJ), lambda i: (0, 0)),
                  pl.BlockSpec((1, HY_PROJ), lambda i: (0, 0))],
        out_specs=[pl.BlockSpec((TM, HY_W), lambda i: (i, 0))] * 3,
        out_shape=[jax.ShapeDtypeStruct((T, HY_W), F32), jax.ShapeDtypeStruct((T, HY_W), F32),
                   jax.ShapeDtypeStruct((T, HY_W), BF16)],
        compiler_params=_params(("parallel",)),
        name="hy_gate",
    )(hy, hy, hy, conv_w, conv_b)


def _dft_mats(L):
    n = 2 * L
    f = jnp.arange(L, dtype=I32)[:, None]
    t = jnp.arange(L, dtype=I32)[None, :]
    ph = ((f * t) % n).astype(F32) * (2.0 * math.pi / n)
    c = jnp.cos(ph)
    s = jnp.where(f == 0, jnp.where(t % 2 == 0, 1.0, -1.0), -jnp.sin(ph))
    return c.astype(BF16), s.astype(BF16), s.T.astype(BF16)


def _hy_features(L):
    nn = jnp.arange(L, dtype=F32)[:, None]
    t = jnp.linspace(0.0, 1.0, L, dtype=F32)[:, None]
    bands = jnp.linspace(1e-4, HY_BANDS - 1, HY_BANDS, dtype=F32)[None, :]
    ang = (2.0 * math.pi / L) * nn * bands
    z = jnp.concatenate([t, jnp.cos(ang), -jnp.sin(ang)], axis=-1)
    z = jnp.pad(z, ((0, 0), (0, 40 - HY_EMB)))
    deltas = jnp.abs(jnp.linspace(math.log(1e-2) / 1.5, math.log(1e-2) / 0.3, HY_W, dtype=F32))
    decay = jnp.exp(-t * deltas)
    bwd = jnp.where(jnp.arange(L)[:, None] == 0, 0.0, decay)
    return z, jnp.concatenate([decay, bwd], axis=1)


def _hy_filter_body(z_ref, w1_ref, b1_ref, fr_ref, w2_ref, b2_ref, w3_ref, dec_ref, c_ref, s_ref,
                    kre_ref, kim_ref, h_scr, *, L, fb):
    f = pl.program_id(0)

    @pl.when(f == 0)
    def _():
        fr = fr_ref[...]
        h = jnp.sin(fr * (jnp.dot(z_ref[...], w1_ref[...], precision=HI, preferred_element_type=F32)
                          + b1_ref[...]))
        h = jnp.sin(fr * (jnp.dot(h, w2_ref[...], precision=HI, preferred_element_type=F32) + b2_ref[...]))
        h = jnp.dot(h, w3_ref[...], precision=HI, preferred_element_type=F32) * dec_ref[...]
        h_scr[...] = h.astype(BF16)

    hb = h_scr[...]
    a_re = jnp.dot(c_ref[...], hb, preferred_element_type=F32)
    a_im = jnp.dot(s_ref[...], hb, preferred_element_type=F32)
    grow = f * fb + lax.broadcasted_iota(I32, (fb, 1), 0)
    packed = grow == 0
    sc = jnp.where(packed, 1.0 / (2 * L), 2.0 / (2 * L))
    kre_ref[...] = (a_re[:, :HY_W] + a_re[:, HY_W:]) * sc
    kim_ref[...] = (a_im[:, :HY_W] + jnp.where(packed, 1.0, -1.0) * a_im[:, HY_W:]) * sc


def _hy_filter(L, feats, mats, w1, b1, freq, w2, b2, w3):
    z, dec = feats
    c, s, _ = mats
    fb = min(L, 512)
    full = lambda shape: pl.BlockSpec(shape, lambda f: (0,) * len(shape))
    w1p = jnp.pad(w1, ((0, 40 - HY_EMB), (0, 0)))
    return pl.pallas_call(
        functools.partial(_hy_filter_body, L=L, fb=fb),
        grid=(L // fb,),
        in_specs=[full((L, 40)), full((40, HY_FF)), full((1, HY_FF)), full((1, HY_FF)),
                  full((HY_FF, HY_FF)), full((1, HY_FF)), full((HY_FF, 2 * HY_W)), full((L, 2 * HY_W)),
                  pl.BlockSpec((fb, L), lambda f: (f, 0)), pl.BlockSpec((fb, L), lambda f: (f, 0))],
        out_specs=[pl.BlockSpec((fb, HY_W), lambda f: (f, 0))] * 2,
        out_shape=[jax.ShapeDtypeStruct((L, HY_W), F32)] * 2,
        scratch_shapes=[pltpu.VMEM((L, 2 * HY_W), BF16)],
        compiler_params=_params(("arbitrary",), VMEM_LIMIT),
        name="hy_filter",
    )(z, w1p, b1.reshape(1, -1), freq.reshape(1, -1), w2, b2.reshape(1, -1), w3, dec, c, s)


def _hy_conv_body(v_ref, cr_ref, sr_ref, cc_ref, sc_ref, kre_ref, kim_ref, o_ref, *, fb):
    f = pl.program_id(1)
    vb = v_ref[...]
    vre = jnp.dot(cr_ref[...], vb, preferred_element_type=F32)
    vim = jnp.dot(sr_ref[...], vb, preferred_element_type=F32)
    kre = kre_ref[...]
    kim = kim_ref[...]
    grow = f * fb + lax.broadcasted_iota(I32, (fb, 1), 0)
    packed = grow == 0
    yre = vre * kre - jnp.where(packed, 0.0, vim * kim)
    yim = jnp.where(packed, vim * kim, vre * kim + vim * kre)
    part = (jnp.dot(cc_ref[...], yre.astype(BF16), preferred_element_type=F32)
            + jnp.dot(sc_ref[...], yim.astype(BF16), preferred_element_type=F32))

    @pl.when(f == 0)
    def _():
        o_ref[...] = part

    @pl.when(f > 0)
    def _():
        o_ref[...] += part


def _hy_conv(vgb, mats, kre, kim, L, nseq, row_off):
    c, s, st = mats
    fb = min(L, 256)
    off = row_off // L
    return pl.pallas_call(
        functools.partial(_hy_conv_body, fb=fb),
        grid=(nseq, L // fb),
        in_specs=[pl.BlockSpec((L, HY_W), lambda b, f: (off + b, 0)),
                  pl.BlockSpec((fb, L), lambda b, f: (f, 0)),
                  pl.BlockSpec((fb, L), lambda b, f: (f, 0)),
                  pl.BlockSpec((L, fb), lambda b, f: (0, f)),
                  pl.BlockSpec((L, fb), lambda b, f: (0, f)),
                  pl.BlockSpec((fb, HY_W), lambda b, f: (f, 0)),
                  pl.BlockSpec((fb, HY_W), lambda b, f: (f, 0))],
        out_specs=pl.BlockSpec((L, HY_W), lambda b, f: (b, 0)),
        out_shape=jax.ShapeDtypeStruct((nseq * L, HY_W), F32),
        compiler_params=_params(("parallel", "arbitrary"), VMEM_LIMIT),
        name="hy_conv",
    )(vgb, c, s, c, st, kre, kim)


def _layer_norm(z, g, b):
    mu = jnp.mean(z, axis=-1, keepdims=True)
    zc = z - mu
    var = jnp.mean(zc * zc, axis=-1, keepdims=True)
    return zc * lax.rsqrt(var + LN_EPS) * g + b


def _post1_body(*refs, even):
    is_ctx = pl.program_id(0) < NCT
    if even:
        (x_ref, m_ref, convc_ref, convs_ref, vg_ref, x0_ref, skip_ref, atc_ref, ats_ref, w_ref, b_ref,
         g_ref, be_ref, rw_ref, rb_ref, x1_ref, xm_ref, lg_ref) = refs
        conv = jnp.where(is_ctx, convc_ref[...], convs_ref[...])
        yh = ((conv + vg_ref[...] * skip_ref[...]) * x0_ref[...]).astype(BF16)
        at = jnp.where(is_ctx, atc_ref[...], ats_ref[...])
        y = (jnp.dot(yh, w_ref[:HY_W, :], preferred_element_type=F32)
             + jnp.dot(at, w_ref[HY_W:, :], preferred_element_type=F32))
    else:
        (x_ref, m_ref, atc_ref, ats_ref, w_ref, b_ref, g_ref, be_ref, rw_ref, rb_ref, x1_ref, xm_ref,
         lg_ref) = refs
        at = jnp.where(is_ctx, atc_ref[...], ats_ref[...])
        y = jnp.dot(at, w_ref[...], preferred_element_type=F32)
    y = y + b_ref[...]
    x1 = _layer_norm(DN_ALPHA * x_ref[...] + m_ref[2:3, :] * y, g_ref[...], be_ref[...])
    x1_ref[...] = x1
    xm = x1 * (1.0 + m_ref[4:5, :]) + m_ref[3:4, :]
    _store_rows(xm_ref, xm)
    lg_ref[...] = lax.dot_general(rw_ref[...], xm, (((1,), (1,)), ((), ())), precision=HI,
                                  preferred_element_type=F32) + rb_ref[...]


def _post1(x, mods, l, parts, w_bf, b, g, be, rw_t, rb, even):
    row = lambda w: pl.BlockSpec((TM, w), lambda i: (i, 0))
    full = lambda shape: pl.BlockSpec(shape, lambda i: (0,) * len(shape))
    in_specs = [row(D), pl.BlockSpec((None, None, 6, D), lambda i: (l, _mod_row(i), 0, 0))]
    ctx = lambda w: pl.BlockSpec((TM, w), lambda i: (jnp.minimum(i, NCT - 1), 0))
    lat = lambda w: pl.BlockSpec((TM, w), lambda i: (jnp.maximum(i - NCT, 0), 0))
    if even:
        in_specs += [ctx(HY_W), lat(HY_W), row(HY_W), row(HY_W), full((1, HY_W)), ctx(NA_W), lat(NA_W)]
    else:
        in_specs += [ctx(D), lat(D)]
    in_specs += [full((D, D)), full((1, D)), full((1, D)), full((1, D)), full((N_EXPERTS, D)),
                 full((N_EXPERTS, 1))]
    return pl.pallas_call(
        functools.partial(_post1_body, even=even),
        grid=(NT,),
        in_specs=in_specs,
        out_specs=[row(D), pl.BlockSpec((TM * SUB, LANE), lambda i: (i, 0)),
                   pl.BlockSpec((None, N_EXPERTS, TM), lambda i: (i, 0, 0))],
        out_shape=[jax.ShapeDtypeStruct((T, D), F32), jax.ShapeDtypeStruct((T * SUB, LANE), F32),
                   jax.ShapeDtypeStruct((NT, N_EXPERTS, TM), F32)],
        compiler_params=_params(("parallel",), VMEM_LIMIT),
        name="post1",
    )(x, mods, *parts, w_bf, b.reshape(1, D), g.reshape(1, D), be.reshape(1, D), rw_t,
      rb.reshape(N_EXPERTS, 1))


def _route_body(lg_ref, slot_ref, gate_ref, pe_ref, pd_ref, be_ref, rb_ref, nx_ref, par_ref, eidx_scr,
                rank_scr):
    eio = lax.broadcasted_iota(I32, (N_EXPERTS, TM), 0)
    ri = lax.broadcasted_iota(I32, (TM, TM), 0)
    ci = lax.broadcasted_iota(I32, (TM, TM), 1)
    upper = jnp.where(ri < ci, 1.0, 0.0).astype(BF16)

    def tile(i, carry):
        lg = lg_ref[i]
        sel = jnp.zeros((N_EXPERTS, TM), F32)
        vals, hots = [], []
        for k in range(TOP_K):
            mx = jnp.max(lg, axis=0, keepdims=True)
            idx = jnp.min(jnp.where(lg == mx, eio, N_EXPERTS), axis=0, keepdims=True)
            hot = eio == idx
            lg = jnp.where(hot, -jnp.inf, lg)
            sel = sel + jnp.where(hot, 1.0, 0.0)
            vals.append(mx)
            hots.append(hot)
            eidx_scr[i, k:k + 1, :] = idx
        rank = jnp.dot(sel.astype(BF16), upper, preferred_element_type=F32) + carry
        ex = [jnp.exp(v - vals[0]) for v in vals]
        den = ex[0] + ex[1] + ex[2] + ex[3]
        for k in range(TOP_K):
            gate_ref[i, k:k + 1, :] = ex[k] / den
            rank_scr[i, k:k + 1, :] = jnp.sum(jnp.where(hots[k], rank, 0.0), axis=0, keepdims=True)
        return carry + jnp.sum(sel, axis=1, keepdims=True)

    counts = lax.fori_loop(0, NT, tile, jnp.zeros((N_EXPERTS, 1), F32))
    padded = jnp.floor((counts + (MOE_BM - 1)) * (1.0 / MOE_BM)) * MOE_BM
    e_r = lax.broadcasted_iota(I32, (N_EXPERTS, N_EXPERTS), 0)
    e_c = lax.broadcasted_iota(I32, (N_EXPERTS, N_EXPERTS), 1)
    incl = jnp.where(e_c <= e_r, 1.0, 0.0)
    padded_b = jnp.broadcast_to(padded, (N_EXPERTS, LANE))
    pad_end = jnp.dot(incl, padded_b, precision=HI, preferred_element_type=F32)
    pad_start = pad_end[:, 0:1] - padded
    pe_ref[...] = pad_end.astype(I32)
    pd_ref[...] = padded_b.astype(I32)
    blk0 = (lax.broadcasted_iota(I32, (1, TM), 1) * MOE_BM).astype(F32)
    nle = jnp.sum(jnp.where(pad_end[:, 0:1] <= blk0, 1.0, 0.0), axis=0, keepdims=True)
    blk_e = jnp.minimum(nle, N_EXPERTS - 1.0)
    be_ref[...] = blk_e.astype(I32)
    mine = eio.astype(F32) == blk_e
    cnt_b = jnp.sum(jnp.where(mine, counts, 0.0), axis=0, keepdims=True)
    start_b = jnp.sum(jnp.where(mine, pad_start, 0.0), axis=0, keepdims=True)
    rb_ref[...] = jnp.clip(cnt_b - (blk0 - start_b), 0.0, float(MOE_BM)).astype(I32)
    used = padded > 0.0
    e_f = eio.astype(F32)
    nx_ref[...] = jnp.min(jnp.where((e_f > blk_e) & used, e_f, float(N_EXPERTS)), axis=0,
                          keepdims=True).astype(I32)
    ordinal = jnp.sum(jnp.where((e_f < blk_e) & used, 1.0, 0.0), axis=0, keepdims=True)
    par_ref[...] = (ordinal - 2.0 * jnp.floor(ordinal * 0.5)).astype(I32)

    def place(i, c):
        for k in range(TOP_K):
            hot = eio == eidx_scr[i, k:k + 1, :]
            start = jnp.sum(jnp.where(hot, pad_start, 0.0), axis=0, keepdims=True)
            slot_ref[i, k:k + 1, :] = (start + rank_scr[i, k:k + 1, :]).astype(I32)
        return c

    lax.fori_loop(0, NT, place, 0)


def _route(logits):
    return pl.pallas_call(
        _route_body,
        out_shape=[jax.ShapeDtypeStruct((NT, TOP_K, TM), I32), jax.ShapeDtypeStruct((NT, TOP_K, TM), F32),
                   jax.ShapeDtypeStruct((N_EXPERTS, LANE), I32), jax.ShapeDtypeStruct((N_EXPERTS, LANE), I32),
                   ] + [jax.ShapeDtypeStruct((1, TM), I32)] * 4,
        scratch_shapes=[pltpu.VMEM((NT, TOP_K, TM), I32), pltpu.VMEM((NT, TOP_K, TM), F32)],
        compiler_params=_params(None, VMEM_LIMIT),
        name="route",
    )(logits)


def _dispatch_body(slot_ref, pe_ref, pd_ref, xm_ref, xb_ref, zero_scr, sem):
    i = pl.program_id(0)

    def fill_block(b):
        start = b * (MOE_BM * SUB)
        if not isinstance(b, int):
            start = pl.multiple_of(start, MOE_BM * SUB)
        return pltpu.make_async_copy(zero_scr, xb_ref.at[pl.ds(start, MOE_BM * SUB)], sem)

    @pl.when(i == 0)
    def _():
        zero_scr[...] = jnp.zeros_like(zero_scr)
        n_used = pe_ref[N_EXPERTS - 1] // MOE_BM
        for start_or_wait in (True, False):
            for e in range(N_EXPERTS):
                @pl.when(pd_ref[e] > 0)
                def _():
                    cp = fill_block(pe_ref[e] // MOE_BM - 1)
                    cp.start() if start_or_wait else cp.wait()
            for b in range(T * TOP_K // MOE_BM, N_SLOT_BLOCKS):
                @pl.when(b >= n_used)
                def _():
                    cp = fill_block(b)
                    cp.start() if start_or_wait else cp.wait()

    def row_copy(r, k):
        s = slot_ref[i * (TOP_K * TM) + k * TM + r]
        return pltpu.make_async_copy(xm_ref.at[pl.ds(pl.multiple_of(r * SUB, SUB), SUB)],
                                     xb_ref.at[pl.ds(pl.multiple_of(s * SUB, SUB), SUB)], sem)

    def issue(r, c):
        for k in range(TOP_K):
            row_copy(r, k).start(priority=k % DMA_PRIORITIES)
        return c

    def drain(r, c):
        for k in range(TOP_K):
            row_copy(r, k).wait()
        return c

    lax.fori_loop(0, TM, issue, 0, unroll=ROW_DMA_UNROLL)
    lax.fori_loop(0, TM, drain, 0, unroll=ROW_DMA_UNROLL)


def _dispatch(slot_flat, pe, pd, xm):
    return pl.pallas_call(
        _dispatch_body,
        grid_spec=pltpu.PrefetchScalarGridSpec(
            num_scalar_prefetch=3, grid=(NT,),
            in_specs=[pl.BlockSpec((TM * SUB, LANE), lambda i, s, a, b: (i, 0))],
            out_specs=pl.BlockSpec(memory_space=pl.ANY),
            scratch_shapes=[pltpu.VMEM((MOE_BM * SUB, LANE), F32), pltpu.SemaphoreType.DMA]),
        out_shape=jax.ShapeDtypeStruct((N_SLOTS * SUB, LANE), F32),
        compiler_params=_params(("arbitrary",), VMEM_LIMIT),
        name="moe_dispatch",
    )(slot_flat, pe, pd, xm)


def _expert_body(be_ref, nu_ref, rb_ref, nx_ref, par_ref, xb_ref, guw_hbm, gub_ref, dnw_hbm, dnb_ref, yb_ref,
                 gu_f32, dn_f32, gu_scr, dn_scr, sem, *, l):
    i = pl.program_id(0)
    rows = rb_ref[i]

    def weight_copies(e, slot):
        return (pltpu.make_async_copy(guw_hbm.at[l, e], gu_f32.at[slot], sem.at[0, slot]),
                pltpu.make_async_copy(dnw_hbm.at[l, e], dn_f32.at[slot], sem.at[1, slot]))

    def chain(r0):
        xb = _load_rows(xb_ref, MOE_CHAIN, row0=r0).astype(BF16)
        hgu = jnp.dot(xb, gu_scr[...], preferred_element_type=F32) + gub_ref[...]
        g = jnp.minimum(hgu[:, :D_FF], SWIGLU_LIMIT)
        lin = jnp.clip(hgu[:, D_FF:], -SWIGLU_LIMIT, SWIGLU_LIMIT)
        act = (lin + 1.0) * (g / (1.0 + jnp.exp(-SWIGLU_ALPHA * g)))
        y = jnp.dot(act.astype(BF16), dn_scr[...], preferred_element_type=F32) + dnb_ref[...]
        _store_rows(yb_ref, y, row0=r0)

    @pl.when(i >= nu_ref[0])
    def _():
        yb_ref[...] = jnp.zeros_like(yb_ref)

    @pl.when(i < nu_ref[0])
    def _():
        e = be_ref[i]
        slot = par_ref[i]

        @pl.when(i == 0)
        def _():
            for cp in weight_copies(e, slot):
                cp.start()

        @pl.when((i == 0) | (e != be_ref[jnp.maximum(i - 1, 0)]))
        def _():
            nxt = nx_ref[i]

            @pl.when(nxt < N_EXPERTS)
            def _():
                for cp in weight_copies(nxt, 1 - slot):
                    cp.start()

            for cp in weight_copies(e, slot):
                cp.wait()
            gu_scr[...] = gu_f32[slot].astype(BF16)
            dn_scr[...] = dn_f32[slot].astype(BF16)

        @pl.when(rows > MOE_CHAIN)
        def _():
            for r0 in range(0, MOE_BM, MOE_CHAIN):
                chain(r0)

        @pl.when(rows <= MOE_CHAIN)
        def _():
            chain(0)
            yb_ref[MOE_CHAIN * SUB:, :] = jnp.zeros((MOE_BM * SUB - MOE_CHAIN * SUB, LANE), F32)


def _experts(block_e, n_used, block_rows, next_e, parity, xb, l, gu_w, gu_b, dn_w, dn_b):
    def blk(i, nu):
        return jnp.minimum(i, nu[0] - 1)

    def bias(width):
        return pl.BlockSpec((None, None, 1, width), lambda i, be, nu, rb, nx, par: (l, be[blk(i, nu)], 0, 0))

    return pl.pallas_call(
        functools.partial(_expert_body, l=l),
        grid_spec=pltpu.PrefetchScalarGridSpec(
            num_scalar_prefetch=5, grid=(N_SLOT_BLOCKS,),
            in_specs=[pl.BlockSpec((MOE_BM * SUB, LANE), lambda i, be, nu, rb, nx, par: (blk(i, nu), 0)),
                      pl.BlockSpec(memory_space=pl.ANY), bias(2 * D_FF),
                      pl.BlockSpec(memory_space=pl.ANY), bias(D)],
            out_specs=pl.BlockSpec((MOE_BM * SUB, LANE), lambda i, be, nu, rb, nx, par: (i, 0)),
            scratch_shapes=[pltpu.VMEM((2, D, 2 * D_FF), F32), pltpu.VMEM((2, D_FF, D), F32),
                            pltpu.VMEM((D, 2 * D_FF), BF16), pltpu.VMEM((D_FF, D), BF16),
                            pltpu.SemaphoreType.DMA((2, 2))]),
        out_shape=jax.ShapeDtypeStruct((N_SLOTS * SUB, LANE), F32),
        compiler_params=_params(("arbitrary",), VMEM_LIMIT),
        name="moe_experts",
    )(block_e, n_used, block_rows, next_e, parity, xb, gu_w, gu_b.reshape(DEPTH, N_EXPERTS, 1, 2 * D_FF), dn_w,
      dn_b.reshape(DEPTH, N_EXPERTS, 1, D))


def _post2_body(slot_ref, x1_ref, gt_ref, m_ref, g_ref, be_ref, yb_ref, x2_ref, y_scr, sem):
    i = pl.program_id(0)
    buf = i % 2

    def row_copy(t, r, k):
        s = slot_ref[t * (TOP_K * TM) + k * TM + r]
        return pltpu.make_async_copy(yb_ref.at[pl.ds(pl.multiple_of(s * SUB, SUB), SUB)],
                                     y_scr.at[t % 2, k, pl.ds(pl.multiple_of(r * SUB, SUB), SUB)],
                                     sem.at[t % 2])

    def fetch(t):
        def issue(r, c):
            for k in range(TOP_K):
                row_copy(t, r, k).start(priority=k % DMA_PRIORITIES)
            return c
        lax.fori_loop(0, TM, issue, 0, unroll=ROW_DMA_UNROLL)

    @pl.when(i == 0)
    def _():
        fetch(i)

    @pl.when(i + 1 < NT)
    def _():
        fetch(i + 1)

    def drain(r, c):
        for k in range(TOP_K):
            row_copy(i, r, k).wait()
        return c

    lax.fori_loop(0, TM, drain, 0, unroll=ROW_DMA_UNROLL)
    f = _load_rows(y_scr, TM, (buf, 0)) * gt_ref[:, 0:1]
    for k in range(1, TOP_K):
        f = f + _load_rows(y_scr, TM, (buf, k)) * gt_ref[:, k:k + 1]
    x2_ref[...] = _layer_norm(DN_ALPHA * x1_ref[...] + m_ref[5:6, :] * f, g_ref[...], be_ref[...])


def _post2(slot_flat, x1, gates_t, mods, l, g, be, yb):
    full = lambda shape: pl.BlockSpec(shape, lambda i, s: (0,) * len(shape))
    return pl.pallas_call(
        _post2_body,
        grid_spec=pltpu.PrefetchScalarGridSpec(
            num_scalar_prefetch=1, grid=(NT,),
            in_specs=[pl.BlockSpec((TM, D), lambda i, s: (i, 0)),
                      pl.BlockSpec((TM, TOP_K), lambda i, s: (i, 0)),
                      pl.BlockSpec((None, None, 6, D), lambda i, s: (l, _mod_row(i), 0, 0)),
                      full((1, D)), full((1, D)),
                      pl.BlockSpec(memory_space=pl.ANY)],
            out_specs=pl.BlockSpec((TM, D), lambda i, s: (i, 0)),
            scratch_shapes=[pltpu.VMEM((2, TOP_K, TM * SUB, LANE), F32), pltpu.SemaphoreType.DMA((2,))]),
        out_shape=jax.ShapeDtypeStruct((T, D), F32),
        compiler_params=_params(("arbitrary",), VMEM_LIMIT),
        name="post2",
    )(slot_flat, x1, gates_t, mods, g.reshape(1, D), be.reshape(1, D), yb)


def _rope_tables():
    quarter = HEAD // 4
    pos = jnp.arange(DEC_SEQ)
    rows = (pos // GRID_W).astype(F32)[:, None]
    cols = (pos % GRID_W).astype(F32)[:, None]
    lane = jnp.arange(LANE)[None, :]
    d = lane % HEAD
    inv = ROPE_THETA ** (-(d % quarter).astype(F32) / quarter)
    ang = jnp.where(d < HEAD // 2, rows, cols) * inv
    sign = jnp.where((d // quarter) % 2 == 0, -1.0, 1.0)
    cos = jnp.concatenate([jnp.ones((TC, LANE), F32), jnp.tile(jnp.cos(ang), (DEC_BATCH, 1))], axis=0)
    sin = jnp.concatenate([jnp.zeros((TC, LANE), F32), jnp.tile(sign * jnp.sin(ang), (DEC_BATCH, 1))], axis=0)
    return cos, sin


def _moe_and_norm(x1, xm, logits, mods, l, ln_g, ln_b, exp_gu_w, exp_gu_b, exp_dn_w, exp_dn_b):
    slot, gates, pe, pd, be, rb, nx, par = _route(logits)
    slot_flat = slot.reshape(-1)
    pe1 = pe[:, 0]
    xb = _dispatch(slot_flat, pe1, pd[:, 0], xm)
    n_used = pe1[N_EXPERTS - 1:] // MOE_BM
    yb = _experts(be.reshape(-1), n_used, rb.reshape(-1), nx.reshape(-1), par.reshape(-1), xb, l,
                  exp_gu_w, exp_gu_b, exp_dn_w, exp_dn_b)
    gates_t = gates.transpose(0, 2, 1).reshape(T, TOP_K)
    return _post2(slot_flat, x1, gates_t, mods, l, ln_g[l, 1], ln_b[l, 1], yb)


def kernel(x_prompt, x_sample, cache_na_k, cache_na_v, cache_swa_k, cache_swa_v, c, c_ctx, ada_w, ada_b, ln_g, ln_b, ev_in_w, ev_in_b, hy_conv_w, hy_conv_b, hy_f_w1, hy_f_b1, hy_f_freq, hy_f_w2, hy_f_b2, hy_f_w3, hy_skip, na_rpb, od_in_w, od_in_b, swa_sink, mix_out_w, mix_out_b, router_w, router_b, exp_gu_w, exp_gu_b, exp_dn_w, exp_dn_b):
    x = jnp.concatenate([x_prompt.reshape(TC, D), x_sample.reshape(TS, D)], axis=0)
    cond8 = jnp.concatenate([c_ctx[None, :], c, jnp.zeros((8 - 1 - DEC_BATCH, D), F32)], axis=0)
    mods = _adaln(cond8, ada_w, ada_b).reshape(DEPTH, 8, 6, D)

    rope_tabs = _rope_tables()
    hy_consts = {L: (_hy_features(L), _dft_mats(L)) for L in (SEQ, DEC_SEQ)}
    ck_na = cache_na_k.reshape(DEC_BATCH, -1, PAST, NA_W)
    cv_na = cache_na_v.reshape(DEC_BATCH, -1, PAST, NA_W)
    ck_swa = cache_swa_k.reshape(DEC_BATCH, -1, PAST, SWA_KV * HEAD)
    cv_swa = cache_swa_v.reshape(DEC_BATCH, -1, PAST, SWA_KV * HEAD)

    na_k, na_v, swa_k, swa_v = [], [], [], []
    for l in range(DEPTH):
        w_out = mix_out_w[l].astype(BF16)
        rw_t = router_w[l].T
        if l % 2 == 0:
            e = l // 2
            hy, q, k, v = _inproj(x, mods, l, ev_in_w[e].astype(BF16), ev_in_b[e],
                                  ((HY_PROJ, False), (NA_W, False), (NA_W, False), (NA_W, False)))
            x0, vg, vgb = _hy_gate(hy, hy_conv_w[e].reshape(3, HY_PROJ), hy_conv_b[e].reshape(1, HY_PROJ))
            convs = []
            for L, nseq, off in ((SEQ, BATCH, 0), (DEC_SEQ, DEC_BATCH, TC)):
                feats, mats = hy_consts[L]
                kre, kim = _hy_filter(L, feats, mats, hy_f_w1[e], hy_f_b1[e], hy_f_freq[e], hy_f_w2[e],
                                      hy_f_b2[e], hy_f_w3[e])
                convs.append(_hy_conv(vgb, mats, kre, kim, L, nseq, off))
            parts = (convs[0], convs[1], vg, x0, hy_skip[e].reshape(1, HY_W), _ctx_even_attn(q, k, v),
                     _na_attn(q, k, v, ck_na, cv_na, _na_biasmask(na_rpb[e]), e))
            na_k.append(k[:TC].reshape(BATCH, SEQ, NA_HEADS, HEAD))
            na_v.append(v[:TC].reshape(BATCH, SEQ, NA_HEADS, HEAD))
        else:
            o = l // 2
            q, k, v = _inproj(x, mods, l, od_in_w[o].astype(BF16), od_in_b[o],
                              ((SWA_HEADS * HEAD, True), (SWA_KV * HEAD, True), (SWA_KV * HEAD, False)),
                              rope_tabs)
            parts = (_ctx_odd_attn(swa_sink[o], q, k, v), _swa_attn(swa_sink[o], q, k, v, ck_swa, cv_swa, o))
            swa_k.append(k[:TC].reshape(BATCH, SEQ, SWA_KV, HEAD))
            swa_v.append(v[:TC].reshape(BATCH, SEQ, SWA_KV, HEAD))
        x1, xm, logits = _post1(x, mods, l, parts, w_out, mix_out_b[l], ln_g[l, 0], ln_b[l, 0], rw_t,
                                router_b[l], l % 2 == 0)
        x = _moe_and_norm(x1, xm, logits, mods, l, ln_g, ln_b, exp_gu_w, exp_gu_b, exp_dn_w, exp_dn_b)

    return (x[:TC].reshape(BATCH, SEQ, D), x[TC:].reshape(DEC_BATCH, DEC_SEQ, D),
            jnp.stack(na_k, axis=1), jnp.stack(na_v, axis=1), jnp.stack(swa_k, axis=1), jnp.stack(swa_v, axis=1))
```

```python
import functools
import math

import jax
import jax.numpy as jnp
from jax import lax
from jax.experimental import pallas as pl
from jax.experimental.pallas import tpu as pltpu

F32 = jnp.float32
BF16 = jnp.bfloat16
I32 = jnp.int32
HI = lax.Precision.HIGHEST

D = 1024
DEPTH = 4
BATCH, SEQ = 16, 256
DEC_BATCH, DEC_SEQ = 4, 2048
PAST = 256
GRID_W = 64
HEAD = 64
HY_W = 512
HY_PROJ = 3 * HY_W
HY_BANDS = 16
HY_EMB = 1 + 2 * HY_BANDS
HY_FF = 64
NA_HEADS = 8
NA_W = NA_HEADS * HEAD
NA_KH, NA_KW = 8, 16
SWA_HEADS, SWA_KV = 16, 4
SWA_WINDOW = 128
N_EXPERTS, TOP_K = 32, 4
D_FF = 1024
SWIGLU_LIMIT = 7.0
SWIGLU_ALPHA = 1.702
DN_ALPHA = (2 * DEPTH) ** 0.25
LN_EPS = 1e-5
NEG = -1e30
ROPE_THETA = 10000.0

TC = BATCH * SEQ
TS = DEC_BATCH * DEC_SEQ
T = TC + TS
TM = 256
NT = T // TM
NCT = TC // TM
TPS = DEC_SEQ // TM
LANE = 128
MOE_BM = 512
MOE_CHAIN = 256
N_SLOT_BLOCKS = T * TOP_K // MOE_BM + N_EXPERTS
N_SLOTS = N_SLOT_BLOCKS * MOE_BM
VMEM_LIMIT = 56 * 1024 * 1024
DMA_PRIORITIES = 2
ROW_DMA_UNROLL = 8
POST_CHAIN = 128


def _params(sem, vmem=None):
    return pltpu.CompilerParams(dimension_semantics=sem, vmem_limit_bytes=vmem)


def _mod_row(i):
    return jnp.where(i < NCT, 0, 1 + (i - NCT) // TPS)


SUB = 8
ROW_TILES = D // LANE


def _load_rows(ref, nrows, lead=(), row0=0):
    parts = [ref[lead + (pl.ds(row0 * ROW_TILES + s, nrows, stride=ROW_TILES), slice(None))]
             for s in range(ROW_TILES)]
    return jnp.concatenate(parts, axis=1)


def _store_rows(ref, val, row0=0):
    nrows = val.shape[0]
    for s in range(ROW_TILES):
        ref[pl.ds(row0 * ROW_TILES + s, nrows, stride=ROW_TILES), :] = val[:, s * LANE:(s + 1) * LANE]


def _adaln_body(c_ref, w_ref, b_ref, o_ref):
    c = c_ref[...]
    s = c / (1.0 + jnp.exp(-c))
    o_ref[...] = jnp.dot(s, w_ref[...], precision=HI, preferred_element_type=F32) + b_ref[...]


def _adaln(cond8, ada_w, ada_b):
    nt = 1536
    return pl.pallas_call(
        _adaln_body,
        grid=(DEPTH, 6 * D // nt),
        in_specs=[pl.BlockSpec((8, D), lambda l, j: (0, 0)),
                  pl.BlockSpec((None, D, nt), lambda l, j: (l, 0, j)),
                  pl.BlockSpec((None, 1, nt), lambda l, j: (l, 0, j))],
        out_specs=pl.BlockSpec((None, 8, nt), lambda l, j: (l, 0, j)),
        out_shape=jax.ShapeDtypeStruct((DEPTH, 8, 6 * D), F32),
        compiler_params=_params(("arbitrary", "arbitrary"), VMEM_LIMIT),
        name="adaln",
    )(cond8, ada_w, ada_b.reshape(DEPTH, 1, 6 * D))


def _rope(x, cos, sin):
    lane = lax.broadcasted_iota(I32, (1, LANE), 1)
    first = (lane & 16) == 0
    outs = []
    for j in range(x.shape[1] // LANE):
        blk = x[:, j * LANE:(j + 1) * LANE]
        partner = jnp.where(first, pltpu.roll(blk, LANE - 16, 1), pltpu.roll(blk, 16, 1))
        outs.append(blk * cos + partner * sin)
    return jnp.concatenate(outs, axis=1) if len(outs) > 1 else outs[0]


def _inproj_body(*refs, splits, rope):
    if rope:
        x_ref, m_ref, w_ref, b_ref, cos_ref, sin_ref = refs[:6]
        outs = refs[6:]
    else:
        x_ref, m_ref, w_ref, b_ref = refs[:4]
        outs = refs[4:]
    h = x_ref[...] * (1.0 + m_ref[1:2, :]) + m_ref[0:1, :]
    y = jnp.dot(h.astype(BF16), w_ref[...], preferred_element_type=F32) + b_ref[...]
    off = 0
    for o_ref, (width, do_rope) in zip(outs, splits):
        part = y[:, off:off + width]
        if do_rope:
            part = _rope(part, cos_ref[...], sin_ref[...])
        o_ref[...] = part.astype(o_ref.dtype)
        off += width


def _inproj(x, mods, l, w_bf, b, splits, rope_tabs=None):
    p = w_bf.shape[1]
    in_specs = [pl.BlockSpec((TM, D), lambda i: (i, 0)),
                pl.BlockSpec((None, None, 6, D), lambda i: (l, _mod_row(i), 0, 0)),
                pl.BlockSpec((D, p), lambda i: (0, 0)),
                pl.BlockSpec((1, p), lambda i: (0, 0))]
    args = [x, mods, w_bf, b.reshape(1, p)]
    if rope_tabs is not None:
        in_specs += [pl.BlockSpec((TM, LANE), lambda i: (i, 0))] * 2
        args += list(rope_tabs)
    return pl.pallas_call(
        functools.partial(_inproj_body, splits=splits, rope=rope_tabs is not None),
        grid=(NT,),
        in_specs=in_specs,
        out_specs=[pl.BlockSpec((TM, w), lambda i: (i, 0)) for w, _ in splits],
        out_shape=[jax.ShapeDtypeStruct((T, w), F32) for w, _ in splits],
        compiler_params=_params(("parallel",), VMEM_LIMIT),
        name="inproj",
    )(*args)


def _lo_mask():
    return lax.broadcasted_iota(I32, (1, LANE), 1) < HEAD


def _dup_half(x2, upper):
    swapped = pltpu.roll(x2, HEAD, 1)
    keep = jnp.logical_xor(_lo_mask(), upper)
    return jnp.where(keep, x2, swapped)


def _pair_attention(q2, segs, sink_lo=None, sink_hi=None):
    m_rows = q2.shape[0]
    lo = _lo_mask()
    q2 = q2 * (HEAD ** -0.5)
    qs = jnp.concatenate([jnp.where(lo, q2, 0.0), jnp.where(lo, 0.0, q2)], axis=0).astype(BF16)
    scores = []
    for kd, _, bias in segs:
        s = lax.dot_general(qs, kd, (((1,), (1,)), ((), ())), preferred_element_type=F32)
        if bias is not None:
            s = s + bias
        scores.append(s)
    mx = functools.reduce(jnp.maximum, [jnp.max(s, axis=-1, keepdims=True) for s in scores])
    sink = None
    if sink_lo is not None:
        row = lax.broadcasted_iota(I32, (2 * m_rows, 1), 0)
        sink = jnp.where(row < m_rows, sink_lo, sink_hi)
        mx = jnp.maximum(mx, sink)
    den = jnp.zeros_like(mx)
    acc = jnp.zeros((2 * m_rows, LANE), F32)
    for s, (_, vd, _) in zip(scores, segs):
        p = jnp.exp(s - mx)
        den = den + jnp.sum(p, axis=-1, keepdims=True)
        acc = acc + jnp.dot(p.astype(BF16), vd, preferred_element_type=F32)
    if sink is not None:
        den = den + jnp.exp(sink - mx)
    o = acc / den
    return jnp.where(lo, o[:m_rows], o[m_rows:])


def _lanes(j, width=LANE):
    return slice(j * width, (j + 1) * width)


def _ctx_even_body(q_ref, k_ref, v_ref, o_ref):
    for j in range(NA_W // LANE):
        segs = [(k_ref[:, _lanes(j)].astype(BF16), v_ref[:, _lanes(j)].astype(BF16), None)]
        o_ref[:, _lanes(j)] = _pair_attention(q_ref[:, _lanes(j)], segs).astype(o_ref.dtype)


def _ctx_even_attn(q, k, v):
    spec = pl.BlockSpec((SEQ, NA_W), lambda b: (b, 0))
    return pl.pallas_call(
        _ctx_even_body,
        grid=(BATCH,),
        in_specs=[spec, spec, spec],
        out_specs=spec,
        out_shape=jax.ShapeDtypeStruct((TC, NA_W), BF16),
        compiler_params=_params(("parallel",)),
        name="ctx_attn_even",
    )(q, k, v)


def _ctx_odd_body(sink_ref, q_ref, k_ref, v_ref, o_ref):
    for g in range(SWA_KV):
        upper = (g % 2) == 1
        kd = _dup_half(k_ref[:, _lanes(g // 2)], upper).astype(BF16)
        vd = _dup_half(v_ref[:, _lanes(g // 2)], upper).astype(BF16)
        for j in (2 * g, 2 * g + 1):
            o_ref[:, _lanes(j)] = _pair_attention(q_ref[:, _lanes(j)], [(kd, vd, None)],
                                                  sink_ref[2 * j], sink_ref[2 * j + 1]).astype(o_ref.dtype)


def _ctx_odd_attn(sink, q, k, v):
    qspec = pl.BlockSpec((SEQ, SWA_HEADS * HEAD), lambda b, s: (b, 0))
    kspec = pl.BlockSpec((SEQ, SWA_KV * HEAD), lambda b, s: (b, 0))
    return pl.pallas_call(
        _ctx_odd_body,
        grid_spec=pltpu.PrefetchScalarGridSpec(
            num_scalar_prefetch=1, grid=(BATCH,),
            in_specs=[qspec, kspec, kspec], out_specs=qspec),
        out_shape=jax.ShapeDtypeStruct((TC, SWA_HEADS * HEAD), BF16),
        compiler_params=_params(("parallel",)),
        name="ctx_attn_odd",
    )(sink, q, k, v)


SWA_BLK = 128
SWA_NB = DEC_SEQ // SWA_BLK


def _swa_body(sink_ref, q_ref, kp_ref, kc_ref, kn_ref, vp_ref, vc_ref, vn_ref, ck_ref, cv_ref, o_ref):
    n = pl.program_id(1)
    qi = lax.broadcasted_iota(I32, (SWA_BLK, 3 * SWA_BLK), 0)
    kj = lax.broadcasted_iota(I32, (SWA_BLK, 3 * SWA_BLK), 1)
    kpos = (n - 1) * SWA_BLK + kj
    rel = kj - SWA_BLK - qi
    ok = (jnp.abs(rel) <= SWA_WINDOW) & (kpos >= 0) & (kpos < DEC_SEQ)
    mask = jnp.where(ok, 0.0, NEG)
    mask2 = jnp.concatenate([mask, mask], axis=0)
    for g in range(SWA_KV):
        upper = (g % 2) == 1
        kv = _lanes(g // 2)
        kloc = jnp.concatenate([kp_ref[:, kv], kc_ref[:, kv], kn_ref[:, kv]], axis=0)
        vloc = jnp.concatenate([vp_ref[:, kv], vc_ref[:, kv], vn_ref[:, kv]], axis=0)
        segs = [(_dup_half(kloc, upper).astype(BF16), _dup_half(vloc, upper).astype(BF16), mask2),
                (_dup_half(ck_ref[:, kv], upper).astype(BF16), _dup_half(cv_ref[:, kv], upper).astype(BF16),
                 None)]
        for j in (2 * g, 2 * g + 1):
            o_ref[:, _lanes(j)] = _pair_attention(q_ref[:, _lanes(j)], segs, sink_ref[2 * j],
                                                  sink_ref[2 * j + 1]).astype(o_ref.dtype)


def _swa_attn(sink, q, k, v, ck, cv, o):
    base = TC // SWA_BLK
    qspec = pl.BlockSpec((SWA_BLK, SWA_HEADS * HEAD), lambda b, n, s: (base + b * SWA_NB + n, 0))

    def kv(shift):
        return pl.BlockSpec(
            (SWA_BLK, SWA_KV * HEAD),
            lambda b, n, s: (base + b * SWA_NB + jnp.clip(n + shift, 0, SWA_NB - 1), 0))

    cspec = pl.BlockSpec((None, None, PAST, SWA_KV * HEAD), lambda b, n, s: (b, o, 0, 0))
    ospec = pl.BlockSpec((SWA_BLK, SWA_HEADS * HEAD), lambda b, n, s: (b * SWA_NB + n, 0))
    return pl.pallas_call(
        _swa_body,
        grid_spec=pltpu.PrefetchScalarGridSpec(
            num_scalar_prefetch=1, grid=(DEC_BATCH, SWA_NB),
            in_specs=[qspec, kv(-1), kv(0), kv(1), kv(-1), kv(0), kv(1), cspec, cspec],
            out_specs=ospec),
        out_shape=jax.ShapeDtypeStruct((TS, SWA_HEADS * HEAD), BF16),
        compiler_params=_params(("parallel", "parallel")),
        name="swa_attn",
    )(sink, q, k, k, k, v, v, v, ck, cv)


NA_QROWS = 4
NA_TILES = DEC_SEQ // TM
NA_KTILES = 3
NA_KSTART_MAX = NA_TILES - NA_KTILES


def _na_body(q_ref, k0_ref, k1_ref, k2_ref, v0_ref, v1_ref, v2_ref, ck_ref, cv_ref, bias_ref, o_ref):
    for j in range(NA_W // LANE):
        ln = _lanes(j)
        kloc = jnp.concatenate([k0_ref[:, ln], k1_ref[:, ln], k2_ref[:, ln]], axis=0).astype(BF16)
        vloc = jnp.concatenate([v0_ref[:, ln], v1_ref[:, ln], v2_ref[:, ln]], axis=0).astype(BF16)
        bias = jnp.concatenate([bias_ref[2 * j], bias_ref[2 * j + 1]], axis=0)
        segs = [(kloc, vloc, bias), (ck_ref[:, ln].astype(BF16), cv_ref[:, ln].astype(BF16), None)]
        o_ref[:, ln] = _pair_attention(q_ref[:, ln], segs).astype(o_ref.dtype)


def _na_attn(q, k, v, ck, cv, biasmask, e):
    base = NCT

    def kstart(i):
        return jnp.clip(i - 1, 0, NA_KSTART_MAX)

    qspec = pl.BlockSpec((TM, NA_W), lambda i, b: (base + b * NA_TILES + i, 0))

    def kv(t):
        return pl.BlockSpec((TM, NA_W), lambda i, b: (base + b * NA_TILES + kstart(i) + t, 0))

    cspec = pl.BlockSpec((None, None, PAST, NA_W), lambda i, b: (b, e, 0, 0))
    bspec = pl.BlockSpec((None, NA_HEADS, TM, NA_KTILES * TM),
                         lambda i, b: (jnp.where(i == 0, 0, jnp.where(i == NA_TILES - 1, 2, 1)), 0, 0, 0))
    ospec = pl.BlockSpec((TM, NA_W), lambda i, b: (b * NA_TILES + i, 0))
    return pl.pallas_call(
        _na_body,
        grid=(NA_TILES, DEC_BATCH),
        in_specs=[qspec, kv(0), kv(1), kv(2), kv(0), kv(1), kv(2), cspec, cspec, bspec],
        out_specs=ospec,
        out_shape=jax.ShapeDtypeStruct((TS, NA_W), BF16),
        compiler_params=_params(("parallel", "parallel"), VMEM_LIMIT),
        name="na_attn",
    )(q, k, k, k, v, v, v, ck, cv, biasmask)


NA_REL_ROWS = 2 * NA_KH - 1
NA_REL_COLS = 2 * NA_KW - 1


def _na_bias_body(rpb_ref, o_ref):
    h = pl.program_id(0)
    rows = DEC_SEQ // GRID_W
    c = lax.broadcasted_iota(I32, (GRID_W, LANE), 0)
    lane = lax.broadcasted_iota(I32, (GRID_W, LANE), 1)
    c2 = lane % GRID_W
    rel = c2 - c + (NA_KW - 1)
    win_lo = jnp.clip(c - NA_KW // 2, 0, GRID_W - NA_KW)
    col_ok = (c2 >= win_lo) & (c2 < win_lo + NA_KW)
    hits = [rel == j for j in range(NA_REL_COLS)]
    outside = jnp.full((GRID_W, LANE), NEG, F32)
    toep = []
    for dr in range(NA_REL_ROWS):
        base = (h * NA_REL_ROWS + dr) * NA_REL_COLS
        t = outside
        for j in range(NA_REL_COLS):
            t = jnp.where(hits[j], rpb_ref[base + j], t)
        toep.append(jnp.where(col_ok, t, NEG))
    first_half = lane < GRID_W
    for pi, i in enumerate((0, 1, NA_TILES - 1)):
        key_row0 = NA_QROWS * min(max(i - 1, 0), NA_KSTART_MAX)
        for a in range(NA_QROWS):
            r = NA_QROWS * i + a
            rs = min(max(r - NA_KH // 2, 0), rows - NA_KH)

            def block(kr):
                r2 = key_row0 + kr
                return toep[r2 - r + NA_KH - 1] if rs <= r2 < rs + NA_KH else outside

            for s in range(NA_KTILES * NA_QROWS // 2):
                o_ref[pi, a * GRID_W:(a + 1) * GRID_W, s * LANE:(s + 1) * LANE] = jnp.where(
                    first_half, block(2 * s), block(2 * s + 1))


def _na_biasmask(rpb):
    return pl.pallas_call(
        _na_bias_body,
        grid_spec=pltpu.PrefetchScalarGridSpec(
            num_scalar_prefetch=1, grid=(NA_HEADS,), in_specs=[],
            out_specs=pl.BlockSpec((3, None, TM, NA_KTILES * TM), lambda h, s: (0, h, 0, 0))),
        out_shape=jax.ShapeDtypeStruct((3, NA_HEADS, TM, NA_KTILES * TM), F32),
        compiler_params=_params(("parallel",), VMEM_LIMIT),
        name="na_bias",
    )(rpb.reshape(-1))


def _hy_gate_body(u_ref, up_ref, un_ref, w_ref, b_ref, x0_ref, vg_ref, vgb_ref):
    i = pl.program_id(0)
    ti = (i - NCT) % TPS
    first = (i < NCT) | (ti == 0)
    last = (i < NCT) | (ti == TPS - 1)
    u = u_ref[...]
    row = lax.broadcasted_iota(I32, (TM, 1), 0)
    prev_row = jnp.where(first, 0.0, up_ref[7:8, :])
    next_row = jnp.where(last, 0.0, un_ref[0:1, :])
    before = jnp.where(row == 0, prev_row, pltpu.roll(u, 1, 0))
    after = jnp.where(row == TM - 1, next_row, pltpu.roll(u, TM - 1, 0))
    uc = w_ref[0:1, :] * before + w_ref[1:2, :] * u + w_ref[2:3, :] * after + b_ref[...]
    x0_ref[...] = uc[:, :HY_W]
    vg = uc[:, 2 * HY_W:] * uc[:, HY_W:2 * HY_W]
    vg_ref[...] = vg
    vgb_ref[...] = vg.astype(BF16)


def _hy_gate(hy, conv_w, conv_b):
    nb8 = TM // 8
    return pl.pallas_call(
        _hy_gate_body,
        grid=(NT,),
        in_specs=[pl.BlockSpec((TM, HY_PROJ), lambda i: (i, 0)),
                  pl.BlockSpec((8, HY_PROJ), lambda i: (jnp.maximum(i * nb8 - 1, 0), 0)),
                  pl.BlockSpec((8, HY_PROJ), lambda i: (jnp.minimum(i * nb8 + nb8, T // 8 - 1), 0)),
                  pl.BlockSpec((3, HY_PROJ), lambda i: (0, 0)),
                  pl.BlockSpec((1, HY_PROJ), lambda i: (0, 0))],
        out_specs=[pl.BlockSpec((TM, HY_W), lambda i: (i, 0))] * 3,
        out_shape=[jax.ShapeDtypeStruct((T, HY_W), F32), jax.ShapeDtypeStruct((T, HY_W), F32),
                   jax.ShapeDtypeStruct((T, HY_W), BF16)],
        compiler_params=_params(("parallel",)),
        name="hy_gate",
    )(hy, hy, hy, conv_w, conv_b)


DFT_ROWS = 256


def _dft_body(ac_ref, as_ref, bc_ref, bs_ref, c_ref, s_ref, st_ref, *, rb):
    h = pl.program_id(0)
    ac, sa, bc, sb = ac_ref[...], as_ref[...], bc_ref[...], bs_ref[...]
    cosv = ac * bc - sa * sb
    nsin = -(sa * bc + ac * sb)
    row = h * rb + lax.broadcasted_iota(I32, (rb, 1), 0)
    col = lax.broadcasted_iota(I32, (1, cosv.shape[1]), 1)
    c_ref[...] = cosv.astype(BF16)
    s_ref[...] = jnp.where(row == 0, jnp.where(col % 2 == 0, 1.0, -1.0), nsin).astype(BF16)
    st_ref[...] = jnp.where(col == 0, jnp.where(row % 2 == 0, 1.0, -1.0), nsin).astype(BF16)


def _dft_mats(L):
    n = 2 * L
    rb = min(L, DFT_ROWS)
    t = jnp.arange(L, dtype=I32)[None, :]
    hi = jnp.arange(L // rb, dtype=I32)[:, None]
    lo = jnp.arange(rb, dtype=I32)[:, None]
    pa = ((rb * hi * t) % n).astype(F32) * (2.0 * math.pi / n)
    pb = ((lo * t) % n).astype(F32) * (2.0 * math.pi / n)
    coarse = lambda x: x.reshape(L // rb, 1, L)
    full = lambda shape: pl.BlockSpec(shape, lambda h: (0,) * len(shape))
    return pl.pallas_call(
        functools.partial(_dft_body, rb=rb),
        grid=(L // rb,),
        in_specs=[pl.BlockSpec((None, 1, L), lambda h: (h, 0, 0))] * 2 + [full((rb, L))] * 2,
        out_specs=[pl.BlockSpec((rb, L), lambda h: (h, 0))] * 3,
        out_shape=[jax.ShapeDtypeStruct((L, L), BF16)] * 3,
        compiler_params=_params(("parallel",), VMEM_LIMIT),
        name="dft_tables",
    )(coarse(jnp.cos(pa)), coarse(jnp.sin(pa)), jnp.cos(pb), jnp.sin(pb))


def _hy_features(L):
    nn = jnp.arange(L, dtype=F32)[:, None]
    t = jnp.linspace(0.0, 1.0, L, dtype=F32)[:, None]
    bands = jnp.linspace(1e-4, HY_BANDS - 1, HY_BANDS, dtype=F32)[None, :]
    ang = (2.0 * math.pi / L) * nn * bands
    z = jnp.concatenate([t, jnp.cos(ang), -jnp.sin(ang)], axis=-1)
    z = jnp.pad(z, ((0, 0), (0, 40 - HY_EMB)))
    deltas = jnp.abs(jnp.linspace(math.log(1e-2) / 1.5, math.log(1e-2) / 0.3, HY_W, dtype=F32))
    decay = jnp.exp(-t * deltas)
    bwd = jnp.where(jnp.arange(L)[:, None] == 0, 0.0, decay)
    return z, jnp.concatenate([decay, bwd], axis=1)


def _hy_filter_body(z_ref, w1_ref, b1_ref, fr_ref, w2_ref, b2_ref, w3_ref, dec_ref, c_ref, s_ref,
                    kre_ref, kim_ref, h_scr, *, L, fb):
    f = pl.program_id(0)

    @pl.when(f == 0)
    def _():
        fr = fr_ref[...]
        h = jnp.sin(fr * (jnp.dot(z_ref[...], w1_ref[...], precision=HI, preferred_element_type=F32)
                          + b1_ref[...]))
        h = jnp.sin(fr * (jnp.dot(h, w2_ref[...], precision=HI, preferred_element_type=F32) + b2_ref[...]))
        h = jnp.dot(h, w3_ref[...], precision=HI, preferred_element_type=F32) * dec_ref[...]
        h_scr[...] = h.astype(BF16)

    hb = h_scr[...]
    a_re = jnp.dot(c_ref[...], hb, preferred_element_type=F32)
    a_im = jnp.dot(s_ref[...], hb, preferred_element_type=F32)
    grow = f * fb + lax.broadcasted_iota(I32, (fb, 1), 0)
    packed = grow == 0
    sc = jnp.where(packed, 1.0 / (2 * L), 2.0 / (2 * L))
    kre_ref[...] = (a_re[:, :HY_W] + a_re[:, HY_W:]) * sc
    kim_ref[...] = (a_im[:, :HY_W] + jnp.where(packed, 1.0, -1.0) * a_im[:, HY_W:]) * sc


def _hy_filter(L, feats, mats, w1, b1, freq, w2, b2, w3):
    z, dec = feats
    c, s, _ = mats
    fb = min(L, 512)
    full = lambda shape: pl.BlockSpec(shape, lambda f: (0,) * len(shape))
    w1p = jnp.pad(w1, ((0, 40 - HY_EMB), (0, 0)))
    return pl.pallas_call(
        functools.partial(_hy_filter_body, L=L, fb=fb),
        grid=(L // fb,),
        in_specs=[full((L, 40)), full((40, HY_FF)), full((1, HY_FF)), full((1, HY_FF)),
                  full((HY_FF, HY_FF)), full((1, HY_FF)), full((HY_FF, 2 * HY_W)), full((L, 2 * HY_W)),
                  pl.BlockSpec((fb, L), lambda f: (f, 0)), pl.BlockSpec((fb, L), lambda f: (f, 0))],
        out_specs=[pl.BlockSpec((fb, HY_W), lambda f: (f, 0))] * 2,
        out_shape=[jax.ShapeDtypeStruct((L, HY_W), F32)] * 2,
        scratch_shapes=[pltpu.VMEM((L, 2 * HY_W), BF16)],
        compiler_params=_params(("arbitrary",), VMEM_LIMIT),
        name="hy_filter",
    )(z, w1p, b1.reshape(1, -1), freq.reshape(1, -1), w2, b2.reshape(1, -1), w3, dec, c, s)


def _hy_conv_body(v_ref, cr_ref, sr_ref, cc_ref, sc_ref, kre_ref, kim_ref, o_ref, *, fb):
    f = pl.program_id(1)
    vb = v_ref[...]
    vre = jnp.dot(cr_ref[...], vb, preferred_element_type=F32)
    vim = jnp.dot(sr_ref[...], vb, preferred_element_type=F32)
    kre = kre_ref[...]
    kim = kim_ref[...]
    grow = f * fb + lax.broadcasted_iota(I32, (fb, 1), 0)
    packed = grow == 0
    yre = vre * kre - jnp.where(packed, 0.0, vim * kim)
    yim = jnp.where(packed, vim * kim, vre * kim + vim * kre)
    part = (jnp.dot(cc_ref[...], yre.astype(BF16), preferred_element_type=F32)
            + jnp.dot(sc_ref[...], yim.astype(BF16), preferred_element_type=F32))

    @pl.when(f == 0)
    def _():
        o_ref[...] = part

    @pl.when(f > 0)
    def _():
        o_ref[...] += part


def _hy_conv(vgb, mats, kre, kim, L, nseq, row_off):
    c, s, st = mats
    fb = min(L, 256)
    off = row_off // L
    return pl.pallas_call(
        functools.partial(_hy_conv_body, fb=fb),
        grid=(nseq, L // fb),
        in_specs=[pl.BlockSpec((L, HY_W), lambda b, f: (off + b, 0)),
                  pl.BlockSpec((fb, L), lambda b, f: (f, 0)),
                  pl.BlockSpec((fb, L), lambda b, f: (f, 0)),
                  pl.BlockSpec((L, fb), lambda b, f: (0, f)),
                  pl.BlockSpec((L, fb), lambda b, f: (0, f)),
                  pl.BlockSpec((fb, HY_W), lambda b, f: (f, 0)),
                  pl.BlockSpec((fb, HY_W), lambda b, f: (f, 0))],
        out_specs=pl.BlockSpec((L, HY_W), lambda b, f: (b, 0)),
        out_shape=jax.ShapeDtypeStruct((nseq * L, HY_W), F32),
        compiler_params=_params(("parallel", "arbitrary"), VMEM_LIMIT),
        name="hy_conv",
    )(vgb, c, s, c, st, kre, kim)


def _layer_norm(z, g, b):
    mu = jnp.mean(z, axis=-1, keepdims=True)
    zc = z - mu
    var = jnp.mean(zc * zc, axis=-1, keepdims=True)
    return zc * lax.rsqrt(var + LN_EPS) * g + b


def _post1_body(*refs, even):
    is_ctx = pl.program_id(0) < NCT
    if even:
        (x_ref, m_ref, convc_ref, convs_ref, vg_ref, x0_ref, skip_ref, atc_ref, ats_ref, w_ref, b_ref,
         g_ref, be_ref, rw_ref, rb_ref, x1_ref, xm_ref, lg_ref) = refs
    else:
        (x_ref, m_ref, atc_ref, ats_ref, w_ref, b_ref, g_ref, be_ref, rw_ref, rb_ref, x1_ref, xm_ref,
         lg_ref) = refs
    for r0 in range(0, TM, POST_CHAIN):
        rs = slice(r0, r0 + POST_CHAIN)
        at = jnp.where(is_ctx, atc_ref[rs, :], ats_ref[rs, :])
        if even:
            conv = jnp.where(is_ctx, convc_ref[rs, :], convs_ref[rs, :])
            yh = ((conv + vg_ref[rs, :] * skip_ref[...]) * x0_ref[rs, :]).astype(BF16)
            y = (jnp.dot(yh, w_ref[:HY_W, :], preferred_element_type=F32)
                 + jnp.dot(at, w_ref[HY_W:, :], preferred_element_type=F32))
        else:
            y = jnp.dot(at, w_ref[...], preferred_element_type=F32)
        y = y + b_ref[...]
        x1 = _layer_norm(DN_ALPHA * x_ref[rs, :] + m_ref[2:3, :] * y, g_ref[...], be_ref[...])
        x1_ref[rs, :] = x1
        xm = x1 * (1.0 + m_ref[4:5, :]) + m_ref[3:4, :]
        _store_rows(xm_ref, xm, row0=r0)
        lg_ref[:, rs] = lax.dot_general(rw_ref[...], xm, (((1,), (1,)), ((), ())), precision=HI,
                                        preferred_element_type=F32) + rb_ref[...]


def _post1(x, mods, l, parts, w_bf, b, g, be, rw_t, rb, even):
    row = lambda w: pl.BlockSpec((TM, w), lambda i: (i, 0))
    full = lambda shape: pl.BlockSpec(shape, lambda i: (0,) * len(shape))
    in_specs = [row(D), pl.BlockSpec((None, None, 6, D), lambda i: (l, _mod_row(i), 0, 0))]
    ctx = lambda w: pl.BlockSpec((TM, w), lambda i: (jnp.minimum(i, NCT - 1), 0))
    lat = lambda w: pl.BlockSpec((TM, w), lambda i: (jnp.maximum(i - NCT, 0), 0))
    if even:
        in_specs += [ctx(HY_W), lat(HY_W), row(HY_W), row(HY_W), full((1, HY_W)), ctx(NA_W), lat(NA_W)]
    else:
        in_specs += [ctx(D), lat(D)]
    in_specs += [full((D, D)), full((1, D)), full((1, D)), full((1, D)), full((N_EXPERTS, D)),
                 full((N_EXPERTS, 1))]
    return pl.pallas_call(
        functools.partial(_post1_body, even=even),
        grid=(NT,),
        in_specs=in_specs,
        out_specs=[row(D), pl.BlockSpec((TM * SUB, LANE), lambda i: (i, 0)),
                   pl.BlockSpec((None, N_EXPERTS, TM), lambda i: (i, 0, 0))],
        out_shape=[jax.ShapeDtypeStruct((T, D), F32), jax.ShapeDtypeStruct((T * SUB, LANE), F32),
                   jax.ShapeDtypeStruct((NT, N_EXPERTS, TM), F32)],
        compiler_params=_params(("parallel",), VMEM_LIMIT),
        name="post1",
    )(x, mods, *parts, w_bf, b.reshape(1, D), g.reshape(1, D), be.reshape(1, D), rw_t,
      rb.reshape(N_EXPERTS, 1))


def _route_body(lg_ref, slot_ref, gate_ref, pe_ref, pd_ref, be_ref, rb_ref, nx_ref, par_ref, eidx_scr,
                rank_scr):
    eio = lax.broadcasted_iota(I32, (N_EXPERTS, TM), 0)
    ri = lax.broadcasted_iota(I32, (TM, TM), 0)
    ci = lax.broadcasted_iota(I32, (TM, TM), 1)
    upper = jnp.where(ri < ci, 1.0, 0.0).astype(BF16)

    def tile(i, carry):
        lg = lg_ref[i]
        sel = jnp.zeros((N_EXPERTS, TM), F32)
        vals, hots = [], []
        for k in range(TOP_K):
            mx = jnp.max(lg, axis=0, keepdims=True)
            idx = jnp.min(jnp.where(lg == mx, eio, N_EXPERTS), axis=0, keepdims=True)
            hot = eio == idx
            lg = jnp.where(hot, -jnp.inf, lg)
            sel = sel + jnp.where(hot, 1.0, 0.0)
            vals.append(mx)
            hots.append(hot)
            eidx_scr[i, k:k + 1, :] = idx
        rank = jnp.dot(sel.astype(BF16), upper, preferred_element_type=F32) + carry
        ex = [jnp.exp(v - vals[0]) for v in vals]
        den = ex[0] + ex[1] + ex[2] + ex[3]
        for k in range(TOP_K):
            gate_ref[i, k:k + 1, :] = ex[k] / den
            rank_scr[i, k:k + 1, :] = jnp.sum(jnp.where(hots[k], rank, 0.0), axis=0, keepdims=True)
        return carry + jnp.sum(sel, axis=1, keepdims=True)

    counts = lax.fori_loop(0, NT, tile, jnp.zeros((N_EXPERTS, 1), F32))
    padded = jnp.floor((counts + (MOE_BM - 1)) * (1.0 / MOE_BM)) * MOE_BM
    e_r = lax.broadcasted_iota(I32, (N_EXPERTS, N_EXPERTS), 0)
    e_c = lax.broadcasted_iota(I32, (N_EXPERTS, N_EXPERTS), 1)
    incl = jnp.where(e_c <= e_r, 1.0, 0.0)
    padded_b = jnp.broadcast_to(padded, (N_EXPERTS, LANE))
    pad_end = jnp.dot(incl, padded_b, precision=HI, preferred_element_type=F32)
    pad_start = pad_end[:, 0:1] - padded
    pe_ref[...] = pad_end.astype(I32)
    pd_ref[...] = padded_b.astype(I32)
    blk0 = (lax.broadcasted_iota(I32, (1, TM), 1) * MOE_BM).astype(F32)
    nle = jnp.sum(jnp.where(pad_end[:, 0:1] <= blk0, 1.0, 0.0), axis=0, keepdims=True)
    blk_e = jnp.minimum(nle, N_EXPERTS - 1.0)
    be_ref[...] = blk_e.astype(I32)
    mine = eio.astype(F32) == blk_e
    cnt_b = jnp.sum(jnp.where(mine, counts, 0.0), axis=0, keepdims=True)
    start_b = jnp.sum(jnp.where(mine, pad_start, 0.0), axis=0, keepdims=True)
    rb_ref[...] = jnp.clip(cnt_b - (blk0 - start_b), 0.0, float(MOE_BM)).astype(I32)
    used = padded > 0.0
    e_f = eio.astype(F32)
    nx_ref[...] = jnp.min(jnp.where((e_f > blk_e) & used, e_f, float(N_EXPERTS)), axis=0,
                          keepdims=True).astype(I32)
    ordinal = jnp.sum(jnp.where((e_f < blk_e) & used, 1.0, 0.0), axis=0, keepdims=True)
    par_ref[...] = (ordinal - 2.0 * jnp.floor(ordinal * 0.5)).astype(I32)

    def place(i, c):
        for k in range(TOP_K):
            hot = eio == eidx_scr[i, k:k + 1, :]
            start = jnp.sum(jnp.where(hot, pad_start, 0.0), axis=0, keepdims=True)
            slot_ref[i, k:k + 1, :] = (start + rank_scr[i, k:k + 1, :]).astype(I32)
        return c

    lax.fori_loop(0, NT, place, 0)


def _route(logits):
    return pl.pallas_call(
        _route_body,
        out_shape=[jax.ShapeDtypeStruct((NT, TOP_K, TM), I32), jax.ShapeDtypeStruct((NT, TOP_K, TM), F32),
                   jax.ShapeDtypeStruct((N_EXPERTS, LANE), I32), jax.ShapeDtypeStruct((N_EXPERTS, LANE), I32),
                   ] + [jax.ShapeDtypeStruct((1, TM), I32)] * 4,
        scratch_shapes=[pltpu.VMEM((NT, TOP_K, TM), I32), pltpu.VMEM((NT, TOP_K, TM), F32)],
        compiler_params=_params(None, VMEM_LIMIT),
        name="route",
    )(logits)


def _dispatch_body(slot_ref, pe_ref, pd_ref, xm_ref, xb_ref, zero_scr, sem):
    i = pl.program_id(0)

    def fill_block(b):
        start = b * (MOE_BM * SUB)
        if not isinstance(b, int):
            start = pl.multiple_of(start, MOE_BM * SUB)
        return pltpu.make_async_copy(zero_scr, xb_ref.at[pl.ds(start, MOE_BM * SUB)], sem)

    @pl.when(i == 0)
    def _():
        zero_scr[...] = jnp.zeros_like(zero_scr)
        n_used = pe_ref[N_EXPERTS - 1] // MOE_BM
        for start_or_wait in (True, False):
            for e in range(N_EXPERTS):
                @pl.when(pd_ref[e] > 0)
                def _():
                    cp = fill_block(pe_ref[e] // MOE_BM - 1)
                    cp.start() if start_or_wait else cp.wait()
            for b in range(T * TOP_K // MOE_BM, N_SLOT_BLOCKS):
                @pl.when(b >= n_used)
                def _():
                    cp = fill_block(b)
                    cp.start() if start_or_wait else cp.wait()

    def row_copy(r, k):
        s = slot_ref[i * (TOP_K * TM) + k * TM + r]
        return pltpu.make_async_copy(xm_ref.at[pl.ds(pl.multiple_of(r * SUB, SUB), SUB)],
                                     xb_ref.at[pl.ds(pl.multiple_of(s * SUB, SUB), SUB)], sem)

    def issue(r, c):
        for k in range(TOP_K):
            row_copy(r, k).start(priority=k % DMA_PRIORITIES)
        return c

    def drain(r, c):
        for k in range(TOP_K):
            row_copy(r, k).wait()
        return c

    lax.fori_loop(0, TM, issue, 0, unroll=ROW_DMA_UNROLL)
    lax.fori_loop(0, TM, drain, 0, unroll=ROW_DMA_UNROLL)


def _dispatch(slot_flat, pe, pd, xm):
    return pl.pallas_call(
        _dispatch_body,
        grid_spec=pltpu.PrefetchScalarGridSpec(
            num_scalar_prefetch=3, grid=(NT,),
            in_specs=[pl.BlockSpec((TM * SUB, LANE), lambda i, s, a, b: (i, 0))],
            out_specs=pl.BlockSpec(memory_space=pl.ANY),
            scratch_shapes=[pltpu.VMEM((MOE_BM * SUB, LANE), F32), pltpu.SemaphoreType.DMA]),
        out_shape=jax.ShapeDtypeStruct((N_SLOTS * SUB, LANE), F32),
        compiler_params=_params(("arbitrary",), VMEM_LIMIT),
        name="moe_dispatch",
    )(slot_flat, pe, pd, xm)


def _expert_body(be_ref, nu_ref, rb_ref, nx_ref, par_ref, xb_ref, guw_hbm, gub_ref, dnw_hbm, dnb_ref, yb_ref,
                 gu_f32, dn_f32, gu_scr, dn_scr, sem, *, l):
    i = pl.program_id(0)
    rows = rb_ref[i]

    def weight_copies(e, slot):
        return (pltpu.make_async_copy(guw_hbm.at[l, e], gu_f32.at[slot], sem.at[0, slot]),
                pltpu.make_async_copy(dnw_hbm.at[l, e], dn_f32.at[slot], sem.at[1, slot]))

    def chain(r0):
        xb = _load_rows(xb_ref, MOE_CHAIN, row0=r0).astype(BF16)
        hgu = jnp.dot(xb, gu_scr[...], preferred_element_type=F32) + gub_ref[...]
        g = jnp.minimum(hgu[:, :D_FF], SWIGLU_LIMIT)
        lin = jnp.clip(hgu[:, D_FF:], -SWIGLU_LIMIT, SWIGLU_LIMIT)
        act = (lin + 1.0) * (g / (1.0 + jnp.exp(-SWIGLU_ALPHA * g)))
        y = jnp.dot(act.astype(BF16), dn_scr[...], preferred_element_type=F32) + dnb_ref[...]
        _store_rows(yb_ref, y, row0=r0)

    @pl.when(i >= nu_ref[0])
    def _():
        yb_ref[...] = jnp.zeros_like(yb_ref)

    @pl.when(i < nu_ref[0])
    def _():
        e = be_ref[i]
        slot = par_ref[i]

        @pl.when(i == 0)
        def _():
            for cp in weight_copies(e, slot):
                cp.start()

        @pl.when((i == 0) | (e != be_ref[jnp.maximum(i - 1, 0)]))
        def _():
            nxt = nx_ref[i]

            @pl.when(nxt < N_EXPERTS)
            def _():
                for cp in weight_copies(nxt, 1 - slot):
                    cp.start()

            for cp in weight_copies(e, slot):
                cp.wait()
            gu_scr[...] = gu_f32[slot].astype(BF16)
            dn_scr[...] = dn_f32[slot].astype(BF16)

        @pl.when(rows > MOE_CHAIN)
        def _():
            for r0 in range(0, MOE_BM, MOE_CHAIN):
                chain(r0)

        @pl.when(rows <= MOE_CHAIN)
        def _():
            chain(0)
            yb_ref[MOE_CHAIN * SUB:, :] = jnp.zeros((MOE_BM * SUB - MOE_CHAIN * SUB, LANE), F32)


def _experts(block_e, n_used, block_rows, next_e, parity, xb, l, gu_w, gu_b, dn_w, dn_b):
    def blk(i, nu):
        return jnp.minimum(i, nu[0] - 1)

    def bias(width):
        return pl.BlockSpec((None, None, 1, width), lambda i, be, nu, rb, nx, par: (l, be[blk(i, nu)], 0, 0))

    return pl.pallas_call(
        functools.partial(_expert_body, l=l),
        grid_spec=pltpu.PrefetchScalarGridSpec(
            num_scalar_prefetch=5, grid=(N_SLOT_BLOCKS,),
            in_specs=[pl.BlockSpec((MOE_BM * SUB, LANE), lambda i, be, nu, rb, nx, par: (blk(i, nu), 0)),
                      pl.BlockSpec(memory_space=pl.ANY), bias(2 * D_FF),
                      pl.BlockSpec(memory_space=pl.ANY), bias(D)],
            out_specs=pl.BlockSpec((MOE_BM * SUB, LANE), lambda i, be, nu, rb, nx, par: (i, 0)),
            scratch_shapes=[pltpu.VMEM((2, D, 2 * D_FF), F32), pltpu.VMEM((2, D_FF, D), F32),
                            pltpu.VMEM((D, 2 * D_FF), BF16), pltpu.VMEM((D_FF, D), BF16),
                            pltpu.SemaphoreType.DMA((2, 2))]),
        out_shape=jax.ShapeDtypeStruct((N_SLOTS * SUB, LANE), F32),
        compiler_params=_params(("arbitrary",), VMEM_LIMIT),
        name="moe_experts",
    )(block_e, n_used, block_rows, next_e, parity, xb, gu_w, gu_b.reshape(DEPTH, N_EXPERTS, 1, 2 * D_FF), dn_w,
      dn_b.reshape(DEPTH, N_EXPERTS, 1, D))


def _post2_body(slot_ref, x1_ref, gt_ref, m_ref, g_ref, be_ref, yb_ref, x2_ref, y_scr, sem):
    i = pl.program_id(0)
    buf = i % 2

    def row_copy(t, r, k):
        s = slot_ref[t * (TOP_K * TM) + k * TM + r]
        return pltpu.make_async_copy(yb_ref.at[pl.ds(pl.multiple_of(s * SUB, SUB), SUB)],
                                     y_scr.at[t % 2, k, pl.ds(pl.multiple_of(r * SUB, SUB), SUB)],
                                     sem.at[t % 2])

    def fetch(t):
        def issue(r, c):
            for k in range(TOP_K):
                row_copy(t, r, k).start(priority=k % DMA_PRIORITIES)
            return c
        lax.fori_loop(0, TM, issue, 0, unroll=ROW_DMA_UNROLL)

    @pl.when(i == 0)
    def _():
        fetch(i)

    @pl.when(i + 1 < NT)
    def _():
        fetch(i + 1)

    def drain(r, c):
        for k in range(TOP_K):
            row_copy(i, r, k).wait()
        return c

    lax.fori_loop(0, TM, drain, 0, unroll=ROW_DMA_UNROLL)
    f = _load_rows(y_scr, TM, (buf, 0)) * gt_ref[:, 0:1]
    for k in range(1, TOP_K):
        f = f + _load_rows(y_scr, TM, (buf, k)) * gt_ref[:, k:k + 1]
    x2_ref[...] = _layer_norm(DN_ALPHA * x1_ref[...] + m_ref[5:6, :] * f, g_ref[...], be_ref[...])


def _post2(slot_flat, x1, gates_t, mods, l, g, be, yb):
    full = lambda shape: pl.BlockSpec(shape, lambda i, s: (0,) * len(shape))
    return pl.pallas_call(
        _post2_body,
        grid_spec=pltpu.PrefetchScalarGridSpec(
            num_scalar_prefetch=1, grid=(NT,),
            in_specs=[pl.BlockSpec((TM, D), lambda i, s: (i, 0)),
                      pl.BlockSpec((TM, TOP_K), lambda i, s: (i, 0)),
                      pl.BlockSpec((None, None, 6, D), lambda i, s: (l, _mod_row(i), 0, 0)),
                      full((1, D)), full((1, D)),
                      pl.BlockSpec(memory_space=pl.ANY)],
            out_specs=pl.BlockSpec((TM, D), lambda i, s: (i, 0)),
            scratch_shapes=[pltpu.VMEM((2, TOP_K, TM * SUB, LANE), F32), pltpu.SemaphoreType.DMA((2,))]),
        out_shape=jax.ShapeDtypeStruct((T, D), F32),
        compiler_params=_params(("arbitrary",), VMEM_LIMIT),
        name="post2",
    )(slot_flat, x1, gates_t, mods, g.reshape(1, D), be.reshape(1, D), yb)


def _rope_tables():
    quarter = HEAD // 4
    pos = jnp.arange(DEC_SEQ)
    rows = (pos // GRID_W).astype(F32)[:, None]
    cols = (pos % GRID_W).astype(F32)[:, None]
    lane = jnp.arange(LANE)[None, :]
    d = lane % HEAD
    inv = ROPE_THETA ** (-(d % quarter).astype(F32) / quarter)
    ang = jnp.where(d < HEAD // 2, rows, cols) * inv
    sign = jnp.where((d // quarter) % 2 == 0, -1.0, 1.0)
    cos = jnp.concatenate([jnp.ones((TC, LANE), F32), jnp.tile(jnp.cos(ang), (DEC_BATCH, 1))], axis=0)
    sin = jnp.concatenate([jnp.zeros((TC, LANE), F32), jnp.tile(sign * jnp.sin(ang), (DEC_BATCH, 1))], axis=0)
    return cos, sin


def _moe_and_norm(x1, xm, logits, mods, l, ln_g, ln_b, exp_gu_w, exp_gu_b, exp_dn_w, exp_dn_b):
    slot, gates, pe, pd, be, rb, nx, par = _route(logits)
    slot_flat = slot.reshape(-1)
    pe1 = pe[:, 0]
    xb = _dispatch(slot_flat, pe1, pd[:, 0], xm)
    n_used = pe1[N_EXPERTS - 1:] // MOE_BM
    yb = _experts(be.reshape(-1), n_used, rb.reshape(-1), nx.reshape(-1), par.reshape(-1), xb, l,
                  exp_gu_w, exp_gu_b, exp_dn_w, exp_dn_b)
    gates_t = gates.transpose(0, 2, 1).reshape(T, TOP_K)
    return _post2(slot_flat, x1, gates_t, mods, l, ln_g[l, 1], ln_b[l, 1], yb)


def kernel(x_prompt, x_sample, cache_na_k, cache_na_v, cache_swa_k, cache_swa_v, c, c_ctx, ada_w, ada_b, ln_g, ln_b, ev_in_w, ev_in_b, hy_conv_w, hy_conv_b, hy_f_w1, hy_f_b1, hy_f_freq, hy_f_w2, hy_f_b2, hy_f_w3, hy_skip, na_rpb, od_in_w, od_in_b, swa_sink, mix_out_w, mix_out_b, router_w, router_b, exp_gu_w, exp_gu_b, exp_dn_w, exp_dn_b):
    x = jnp.concatenate([x_prompt.reshape(TC, D), x_sample.reshape(TS, D)], axis=0)
    cond8 = jnp.concatenate([c_ctx[None, :], c, jnp.zeros((8 - 1 - DEC_BATCH, D), F32)], axis=0)
    mods = _adaln(cond8, ada_w, ada_b).reshape(DEPTH, 8, 6, D)

    rope_tabs = _rope_tables()
    hy_consts = {L: (_hy_features(L), _dft_mats(L)) for L in (SEQ, DEC_SEQ)}
    ck_na = cache_na_k.reshape(DEC_BATCH, -1, PAST, NA_W)
    cv_na = cache_na_v.reshape(DEC_BATCH, -1, PAST, NA_W)
    ck_swa = cache_swa_k.reshape(DEC_BATCH, -1, PAST, SWA_KV * HEAD)
    cv_swa = cache_swa_v.reshape(DEC_BATCH, -1, PAST, SWA_KV * HEAD)

    na_k, na_v, swa_k, swa_v = [], [], [], []
    for l in range(DEPTH):
        w_out = mix_out_w[l].astype(BF16)
        rw_t = router_w[l].T
        if l % 2 == 0:
            e = l // 2
            hy, q, k, v = _inproj(x, mods, l, ev_in_w[e].astype(BF16), ev_in_b[e],
                                  ((HY_PROJ, False), (NA_W, False), (NA_W, False), (NA_W, False)))
            x0, vg, vgb = _hy_gate(hy, hy_conv_w[e].reshape(3, HY_PROJ), hy_conv_b[e].reshape(1, HY_PROJ))
            convs = []
            for L, nseq, off in ((SEQ, BATCH, 0), (DEC_SEQ, DEC_BATCH, TC)):
                feats, mats = hy_consts[L]
                kre, kim = _hy_filter(L, feats, mats, hy_f_w1[e], hy_f_b1[e], hy_f_freq[e], hy_f_w2[e],
                                      hy_f_b2[e], hy_f_w3[e])
                convs.append(_hy_conv(vgb, mats, kre, kim, L, nseq, off))
            parts = (convs[0], convs[1], vg, x0, hy_skip[e].reshape(1, HY_W), _ctx_even_attn(q, k, v),
                     _na_attn(q, k, v, ck_na, cv_na, _na_biasmask(na_rpb[e]), e))
            na_k.append(k[:TC].reshape(BATCH, SEQ, NA_HEADS, HEAD))
            na_v.append(v[:TC].reshape(BATCH, SEQ, NA_HEADS, HEAD))
        else:
            o = l // 2
            q, k, v = _inproj(x, mods, l, od_in_w[o].astype(BF16), od_in_b[o],
                              ((SWA_HEADS * HEAD, True), (SWA_KV * HEAD, True), (SWA_KV * HEAD, False)),
                              rope_tabs)
            parts = (_ctx_odd_attn(swa_sink[o], q, k, v), _swa_attn(swa_sink[o], q, k, v, ck_swa, cv_swa, o))
            swa_k.append(k[:TC].reshape(BATCH, SEQ, SWA_KV, HEAD))
            swa_v.append(v[:TC].reshape(BATCH, SEQ, SWA_KV, HEAD))
        x1, xm, logits = _post1(x, mods, l, parts, w_out, mix_out_b[l], ln_g[l, 0], ln_b[l, 0], rw_t,
                                router_b[l], l % 2 == 0)
        x = _moe_and_norm(x1, xm, logits, mods, l, ln_g, ln_b, exp_gu_w, exp_gu_b, exp_dn_w, exp_dn_b)

    return (x[:TC].reshape(BATCH, SEQ, D), x[TC:].reshape(DEC_BATCH, DEC_SEQ, D),
            jnp.stack(na_k, axis=1), jnp.stack(na_v, axis=1), jnp.stack(swa_k, axis=1), jnp.stack(swa_v, axis=1))
```

```python
import functools
import math

import jax
import jax.numpy as jnp
from jax import lax
from jax.experimental import pallas as pl
from jax.experimental.pallas import tpu as pltpu

F32 = jnp.float32
BF16 = jnp.bfloat16
I32 = jnp.int32
HI = lax.Precision.HIGHEST

D = 1024
DEPTH = 4
BATCH, SEQ = 16, 256
DEC_BATCH, DEC_SEQ = 4, 2048
PAST = 256
GRID_W = 64
HEAD = 64
HY_W = 512
HY_PROJ = 3 * HY_W
HY_BANDS = 16
HY_EMB = 1 + 2 * HY_BANDS
HY_FF = 64
NA_HEADS = 8
NA_W = NA_HEADS * HEAD
NA_KH, NA_KW = 8, 16
SWA_HEADS, SWA_KV = 16, 4
SWA_WINDOW = 128
N_EXPERTS, TOP_K = 32, 4
D_FF = 1024
SWIGLU_LIMIT = 7.0
SWIGLU_ALPHA = 1.702
DN_ALPHA = (2 * DEPTH) ** 0.25
LN_EPS = 1e-5
NEG = -1e30
ROPE_THETA = 10000.0

TC = BATCH * SEQ
TS = DEC_BATCH * DEC_SEQ
T = TC + TS
TM = 256
NT = T // TM
NCT = TC // TM
TPS = DEC_SEQ // TM
LANE = 128
MOE_BM = 512
MOE_CHAIN = 256
N_SLOT_BLOCKS = T * TOP_K // MOE_BM + N_EXPERTS
N_SLOTS = N_SLOT_BLOCKS * MOE_BM
VMEM_LIMIT = 56 * 1024 * 1024
DMA_PRIORITIES = 2
ROW_DMA_UNROLL = 8
LOG2_TM = TM.bit_length() - 1
LOG2_ASSIGN_TILE = (TOP_K * TM).bit_length() - 1
assert TM == 1 << LOG2_TM and TOP_K * TM == 1 << LOG2_ASSIGN_TILE
POST_CHAIN = 128


def _params(sem, vmem=None):
    return pltpu.CompilerParams(dimension_semantics=sem, vmem_limit_bytes=vmem)


def _mod_row(i):
    return jnp.where(i < NCT, 0, 1 + (i - NCT) // TPS)


SUB = 8
ROW_TILES = D // LANE


def _load_rows(ref, nrows, lead=(), row0=0):
    parts = [ref[lead + (pl.ds(row0 * ROW_TILES + s, nrows, stride=ROW_TILES), slice(None))]
             for s in range(ROW_TILES)]
    return jnp.concatenate(parts, axis=1)


def _store_rows(ref, val, row0=0):
    nrows = val.shape[0]
    for s in range(ROW_TILES):
        ref[pl.ds(row0 * ROW_TILES + s, nrows, stride=ROW_TILES), :] = val[:, s * LANE:(s + 1) * LANE]


def _adaln_body(c_ref, w_ref, b_ref, o_ref):
    c = c_ref[...]
    s = c / (1.0 + jnp.exp(-c))
    o_ref[...] = jnp.dot(s, w_ref[...], precision=HI, preferred_element_type=F32) + b_ref[...]


def _adaln(cond8, ada_w, ada_b):
    nt = 1536
    return pl.pallas_call(
        _adaln_body,
        grid=(DEPTH, 6 * D // nt),
        in_specs=[pl.BlockSpec((8, D), lambda l, j: (0, 0)),
                  pl.BlockSpec((None, D, nt), lambda l, j: (l, 0, j)),
                  pl.BlockSpec((None, 1, nt), lambda l, j: (l, 0, j))],
        out_specs=pl.BlockSpec((None, 8, nt), lambda l, j: (l, 0, j)),
        out_shape=jax.ShapeDtypeStruct((DEPTH, 8, 6 * D), F32),
        compiler_params=_params(("arbitrary", "arbitrary"), VMEM_LIMIT),
        name="adaln",
    )(cond8, ada_w, ada_b.reshape(DEPTH, 1, 6 * D))


def _rope(x, cos, sin):
    lane = lax.broadcasted_iota(I32, (1, LANE), 1)
    first = (lane & 16) == 0
    outs = []
    for j in range(x.shape[1] // LANE):
        blk = x[:, j * LANE:(j + 1) * LANE]
        partner = jnp.where(first, pltpu.roll(blk, LANE - 16, 1), pltpu.roll(blk, 16, 1))
        outs.append(blk * cos + partner * sin)
    return jnp.concatenate(outs, axis=1) if len(outs) > 1 else outs[0]


def _inproj_body(*refs, splits, rope):
    if rope:
        x_ref, m_ref, w_ref, b_ref, cos_ref, sin_ref = refs[:6]
        outs = refs[6:]
    else:
        x_ref, m_ref, w_ref, b_ref = refs[:4]
        outs = refs[4:]
    h = x_ref[...] * (1.0 + m_ref[1:2, :]) + m_ref[0:1, :]
    y = jnp.dot(h.astype(BF16), w_ref[...], preferred_element_type=F32) + b_ref[...]
    off = 0
    for o_ref, (width, do_rope) in zip(outs, splits):
        part = y[:, off:off + width]
        if do_rope:
            part = _rope(part, cos_ref[...], sin_ref[...])
        o_ref[...] = part.astype(o_ref.dtype)
        off += width


def _inproj(x, mods, l, w_bf, b, splits, rope_tabs=None):
    p = w_bf.shape[1]
    in_specs = [pl.BlockSpec((TM, D), lambda i: (i, 0)),
                pl.BlockSpec((None, None, 6, D), lambda i: (l, _mod_row(i), 0, 0)),
                pl.BlockSpec((D, p), lambda i: (0, 0)),
                pl.BlockSpec((1, p), lambda i: (0, 0))]
    args = [x, mods, w_bf, b.reshape(1, p)]
    if rope_tabs is not None:
        in_specs += [pl.BlockSpec((TM, LANE), lambda i: (i, 0))] * 2
        args += list(rope_tabs)
    return pl.pallas_call(
        functools.partial(_inproj_body, splits=splits, rope=rope_tabs is not None),
        grid=(NT,),
        in_specs=in_specs,
        out_specs=[pl.BlockSpec((TM, w), lambda i: (i, 0)) for w, _ in splits],
        out_shape=[jax.ShapeDtypeStruct((T, w), F32) for w, _ in splits],
        compiler_params=_params(("parallel",), VMEM_LIMIT),
        name="inproj",
    )(*args)


def _lo_mask():
    return lax.broadcasted_iota(I32, (1, LANE), 1) < HEAD


def _dup_half(x2, upper):
    swapped = pltpu.roll(x2, HEAD, 1)
    keep = jnp.logical_xor(_lo_mask(), upper)
    return jnp.where(keep, x2, swapped)


def _pair_attention(q2, segs, sink_lo=None, sink_hi=None):
    m_rows = q2.shape[0]
    lo = _lo_mask()
    q2 = q2 * (HEAD ** -0.5)
    qs = jnp.concatenate([jnp.where(lo, q2, 0.0), jnp.where(lo, 0.0, q2)], axis=0).astype(BF16)
    scores = []
    for kd, _, bias in segs:
        s = lax.dot_general(qs, kd, (((1,), (1,)), ((), ())), preferred_element_type=F32)
        if bias is not None:
            s = s + bias
        scores.append(s)
    mx = functools.reduce(jnp.maximum, [jnp.max(s, axis=-1, keepdims=True) for s in scores])
    sink = None
    if sink_lo is not None:
        row = lax.broadcasted_iota(I32, (2 * m_rows, 1), 0)
        sink = jnp.where(row < m_rows, sink_lo, sink_hi)
        mx = jnp.maximum(mx, sink)
    den = jnp.zeros_like(mx)
    acc = jnp.zeros((2 * m_rows, LANE), F32)
    for s, (_, vd, _) in zip(scores, segs):
        p = jnp.exp(s - mx)
        den = den + jnp.sum(p, axis=-1, keepdims=True)
        acc = acc + jnp.dot(p.astype(BF16), vd, preferred_element_type=F32)
    if sink is not None:
        den = den + jnp.exp(sink - mx)
    o = acc / den
    return jnp.where(lo, o[:m_rows], o[m_rows:])


def _lanes(j, width=LANE):
    return slice(j * width, (j + 1) * width)


def _ctx_even_body(q_ref, k_ref, v_ref, o_ref):
    for j in range(NA_W // LANE):
        segs = [(k_ref[:, _lanes(j)].astype(BF16), v_ref[:, _lanes(j)].astype(BF16), None)]
        o_ref[:, _lanes(j)] = _pair_attention(q_ref[:, _lanes(j)], segs).astype(o_ref.dtype)


def _ctx_even_attn(q, k, v):
    spec = pl.BlockSpec((SEQ, NA_W), lambda b: (b, 0))
    return pl.pallas_call(
        _ctx_even_body,
        grid=(BATCH,),
        in_specs=[spec, spec, spec],
        out_specs=spec,
        out_shape=jax.ShapeDtypeStruct((TC, NA_W), BF16),
        compiler_params=_params(("parallel",)),
        name="ctx_attn_even",
    )(q, k, v)


def _ctx_odd_body(sink_ref, q_ref, k_ref, v_ref, o_ref):
    for g in range(SWA_KV):
        upper = (g % 2) == 1
        kd = _dup_half(k_ref[:, _lanes(g // 2)], upper).astype(BF16)
        vd = _dup_half(v_ref[:, _lanes(g // 2)], upper).astype(BF16)
        for j in (2 * g, 2 * g + 1):
            o_ref[:, _lanes(j)] = _pair_attention(q_ref[:, _lanes(j)], [(kd, vd, None)],
                                                  sink_ref[2 * j], sink_ref[2 * j + 1]).astype(o_ref.dtype)


def _ctx_odd_attn(sink, q, k, v):
    qspec = pl.BlockSpec((SEQ, SWA_HEADS * HEAD), lambda b, s: (b, 0))
    kspec = pl.BlockSpec((SEQ, SWA_KV * HEAD), lambda b, s: (b, 0))
    return pl.pallas_call(
        _ctx_odd_body,
        grid_spec=pltpu.PrefetchScalarGridSpec(
            num_scalar_prefetch=1, grid=(BATCH,),
            in_specs=[qspec, kspec, kspec], out_specs=qspec),
        out_shape=jax.ShapeDtypeStruct((TC, SWA_HEADS * HEAD), BF16),
        compiler_params=_params(("parallel",)),
        name="ctx_attn_odd",
    )(sink, q, k, v)


SWA_BLK = 128
SWA_NB = DEC_SEQ // SWA_BLK


def _swa_body(sink_ref, q_ref, kp_ref, kc_ref, kn_ref, vp_ref, vc_ref, vn_ref, ck_ref, cv_ref, o_ref):
    n = pl.program_id(1)
    qi = lax.broadcasted_iota(I32, (SWA_BLK, 3 * SWA_BLK), 0)
    kj = lax.broadcasted_iota(I32, (SWA_BLK, 3 * SWA_BLK), 1)
    kpos = (n - 1) * SWA_BLK + kj
    rel = kj - SWA_BLK - qi
    ok = (jnp.abs(rel) <= SWA_WINDOW) & (kpos >= 0) & (kpos < DEC_SEQ)
    mask = jnp.where(ok, 0.0, NEG)
    mask2 = jnp.concatenate([mask, mask], axis=0)
    for g in range(SWA_KV):
        upper = (g % 2) == 1
        kv = _lanes(g // 2)
        kloc = jnp.concatenate([kp_ref[:, kv], kc_ref[:, kv], kn_ref[:, kv]], axis=0)
        vloc = jnp.concatenate([vp_ref[:, kv], vc_ref[:, kv], vn_ref[:, kv]], axis=0)
        segs = [(_dup_half(kloc, upper).astype(BF16), _dup_half(vloc, upper).astype(BF16), mask2),
                (_dup_half(ck_ref[:, kv], upper).astype(BF16), _dup_half(cv_ref[:, kv], upper).astype(BF16),
                 None)]
        for j in (2 * g, 2 * g + 1):
            o_ref[:, _lanes(j)] = _pair_attention(q_ref[:, _lanes(j)], segs, sink_ref[2 * j],
                                                  sink_ref[2 * j + 1]).astype(o_ref.dtype)


def _swa_attn(sink, q, k, v, ck, cv, o):
    base = TC // SWA_BLK
    qspec = pl.BlockSpec((SWA_BLK, SWA_HEADS * HEAD), lambda b, n, s: (base + b * SWA_NB + n, 0))

    def kv(shift):
        return pl.BlockSpec(
            (SWA_BLK, SWA_KV * HEAD),
            lambda b, n, s: (base + b * SWA_NB + jnp.clip(n + shift, 0, SWA_NB - 1), 0))

    cspec = pl.BlockSpec((None, None, PAST, SWA_KV * HEAD), lambda b, n, s: (b, o, 0, 0))
    ospec = pl.BlockSpec((SWA_BLK, SWA_HEADS * HEAD), lambda b, n, s: (b * SWA_NB + n, 0))
    return pl.pallas_call(
        _swa_body,
        grid_spec=pltpu.PrefetchScalarGridSpec(
            num_scalar_prefetch=1, grid=(DEC_BATCH, SWA_NB),
            in_specs=[qspec, kv(-1), kv(0), kv(1), kv(-1), kv(0), kv(1), cspec, cspec],
            out_specs=ospec),
        out_shape=jax.ShapeDtypeStruct((TS, SWA_HEADS * HEAD), BF16),
        compiler_params=_params(("parallel", "parallel")),
        name="swa_attn",
    )(sink, q, k, k, k, v, v, v, ck, cv)


NA_QROWS = 4
NA_TILES = DEC_SEQ // TM
NA_KTILES = 3
NA_KSTART_MAX = NA_TILES - NA_KTILES


def _na_body(q_ref, k0_ref, k1_ref, k2_ref, v0_ref, v1_ref, v2_ref, ck_ref, cv_ref, bias_ref, o_ref):
    for j in range(NA_W // LANE):
        ln = _lanes(j)
        kloc = jnp.concatenate([k0_ref[:, ln], k1_ref[:, ln], k2_ref[:, ln]], axis=0).astype(BF16)
        vloc = jnp.concatenate([v0_ref[:, ln], v1_ref[:, ln], v2_ref[:, ln]], axis=0).astype(BF16)
        bias = jnp.concatenate([bias_ref[2 * j], bias_ref[2 * j + 1]], axis=0)
        segs = [(kloc, vloc, bias), (ck_ref[:, ln].astype(BF16), cv_ref[:, ln].astype(BF16), None)]
        o_ref[:, ln] = _pair_attention(q_ref[:, ln], segs).astype(o_ref.dtype)


def _na_attn(q, k, v, ck, cv, biasmask, e):
    base = NCT

    def kstart(i):
        return jnp.clip(i - 1, 0, NA_KSTART_MAX)

    qspec = pl.BlockSpec((TM, NA_W), lambda i, b: (base + b * NA_TILES + i, 0))

    def kv(t):
        return pl.BlockSpec((TM, NA_W), lambda i, b: (base + b * NA_TILES + kstart(i) + t, 0))

    cspec = pl.BlockSpec((None, None, PAST, NA_W), lambda i, b: (b, e, 0, 0))
    bspec = pl.BlockSpec((None, NA_HEADS, TM, NA_KTILES * TM),
                         lambda i, b: (jnp.where(i == 0, 0, jnp.where(i == NA_TILES - 1, 2, 1)), 0, 0, 0))
    ospec = pl.BlockSpec((TM, NA_W), lambda i, b: (b * NA_TILES + i, 0))
    return pl.pallas_call(
        _na_body,
        grid=(NA_TILES, DEC_BATCH),
        in_specs=[qspec, kv(0), kv(1), kv(2), kv(0), kv(1), kv(2), cspec, cspec, bspec],
        out_specs=ospec,
        out_shape=jax.ShapeDtypeStruct((TS, NA_W), BF16),
        compiler_params=_params(("parallel", "parallel"), VMEM_LIMIT),
        name="na_attn",
    )(q, k, k, k, v, v, v, ck, cv, biasmask)


NA_REL_ROWS = 2 * NA_KH - 1
NA_REL_COLS = 2 * NA_KW - 1


def _na_bias_body(rpb_ref, o_ref):
    h = pl.program_id(0)
    rows = DEC_SEQ // GRID_W
    c = lax.broadcasted_iota(I32, (GRID_W, LANE), 0)
    lane = lax.broadcasted_iota(I32, (GRID_W, LANE), 1)
    c2 = lane % GRID_W
    rel = c2 - c + (NA_KW - 1)
    win_lo = jnp.clip(c - NA_KW // 2, 0, GRID_W - NA_KW)
    col_ok = (c2 >= win_lo) & (c2 < win_lo + NA_KW)
    hits = [rel == j for j in range(NA_REL_COLS)]
    outside = jnp.full((GRID_W, LANE), NEG, F32)
    toep = []
    for dr in range(NA_REL_ROWS):
        base = (h * NA_REL_ROWS + dr) * NA_REL_COLS
        t = outside
        for j in range(NA_REL_COLS):
            t = jnp.where(hits[j], rpb_ref[base + j], t)
        toep.append(jnp.where(col_ok, t, NEG))
    first_half = lane < GRID_W
    for pi, i in enumerate((0, 1, NA_TILES - 1)):
        key_row0 = NA_QROWS * min(max(i - 1, 0), NA_KSTART_MAX)
        for a in range(NA_QROWS):
            r = NA_QROWS * i + a
            rs = min(max(r - NA_KH // 2, 0), rows - NA_KH)

            def block(kr):
                r2 = key_row0 + kr
                return toep[r2 - r + NA_KH - 1] if rs <= r2 < rs + NA_KH else outside

            for s in range(NA_KTILES * NA_QROWS // 2):
                o_ref[pi, a * GRID_W:(a + 1) * GRID_W, s * LANE:(s + 1) * LANE] = jnp.where(
                    first_half, block(2 * s), block(2 * s + 1))


def _na_biasmask(rpb):
    return pl.pallas_call(
        _na_bias_body,
        grid_spec=pltpu.PrefetchScalarGridSpec(
            num_scalar_prefetch=1, grid=(NA_HEADS,), in_specs=[],
            out_specs=pl.BlockSpec((3, None, TM, NA_KTILES * TM), lambda h, s: (0, h, 0, 0))),
        out_shape=jax.ShapeDtypeStruct((3, NA_HEADS, TM, NA_KTILES * TM), F32),
        compiler_params=_params(("parallel",), VMEM_LIMIT),
        name="na_bias",
    )(rpb.reshape(-1))


def _hy_gate_body(u_ref, up_ref, un_ref, w_ref, b_ref, x0_ref, vg_ref, vgb_ref):
    i = pl.program_id(0)
    ti = (i - NCT) % TPS
    first = (i < NCT) | (ti == 0)
    last = (i < NCT) | (ti == TPS - 1)
    u = u_ref[...]
    row = lax.broadcasted_iota(I32, (TM, 1), 0)
    prev_row = jnp.where(first, 0.0, up_ref[7:8, :])
    next_row = jnp.where(last, 0.0, un_ref[0:1, :])
    before = jnp.where(row == 0, prev_row, pltpu.roll(u, 1, 0))
    after = jnp.where(row == TM - 1, next_row, pltpu.roll(u, TM - 1, 0))
    uc = w_ref[0:1, :] * before + w_ref[1:2, :] * u + w_ref[2:3, :] * after + b_ref[...]
    x0_ref[...] = uc[:, :HY_W]
    vg = uc[:, 2 * HY_W:] * uc[:, HY_W:2 * HY_W]
    vg_ref[...] = vg
    vgb_ref[...] = vg.astype(BF16)


def _hy_gate(hy, conv_w, conv_b):
    nb8 = TM // 8
    return pl.pallas_call(
        _hy_gate_body,
        grid=(NT,),
        in_specs=[pl.BlockSpec((TM, HY_PROJ), lambda i: (i, 0)),
                  pl.BlockSpec((8, HY_PROJ), lambda i: (jnp.maximum(i * nb8 - 1, 0), 0)),
                  pl.BlockSpec((8, HY_PROJ), lambda i: (jnp.minimum(i * nb8 + nb8, T // 8 - 1), 0)),
                  pl.BlockSpec((3, HY_PROJ), lambda i: (0, 0)),
                  pl.BlockSpec((1, HY_PROJ), lambda i: (0, 0))],
        out_specs=[pl.BlockSpec((TM, HY_W), lambda i: (i, 0))] * 3,
        out_shape=[jax.ShapeDtypeStruct((T, HY_W), F32), jax.ShapeDtypeStruct((T, HY_W), F32),
                   jax.ShapeDtypeStruct((T, HY_W), BF16)],
        compiler_params=_params(("parallel",)),
        name="hy_gate",
    )(hy, hy, hy, conv_w, conv_b)


DFT_ROWS = 256


def _dft_body(ac_ref, as_ref, bc_ref, bs_ref, c_ref, s_ref, st_ref, *, rb):
    h = pl.program_id(0)
    ac, sa, bc, sb = ac_ref[...], as_ref[...], bc_ref[...], bs_ref[...]
    cosv = ac * bc - sa * sb
    nsin = -(sa * bc + ac * sb)
    row = h * rb + lax.broadcasted_iota(I32, (rb, 1), 0)
    col = lax.broadcasted_iota(I32, (1, cosv.shape[1]), 1)
    c_ref[...] = cosv.astype(BF16)
    s_ref[...] = jnp.where(row == 0, jnp.where(col % 2 == 0, 1.0, -1.0), nsin).astype(BF16)
    st_ref[...] = jnp.where(col == 0, jnp.where(row % 2 == 0, 1.0, -1.0), nsin).astype(BF16)


def _dft_mats(L):
    n = 2 * L
    rb = min(L, DFT_ROWS)
    t = jnp.arange(L, dtype=I32)[None, :]
    hi = jnp.arange(L // rb, dtype=I32)[:, None]
    lo = jnp.arange(rb, dtype=I32)[:, None]
    pa = ((rb * hi * t) % n).astype(F32) * (2.0 * math.pi / n)
    pb = ((lo * t) % n).astype(F32) * (2.0 * math.pi / n)
    coarse = lambda x: x.reshape(L // rb, 1, L)
    full = lambda shape: pl.BlockSpec(shape, lambda h: (0,) * len(shape))
    return pl.pallas_call(
        functools.partial(_dft_body, rb=rb),
        grid=(L // rb,),
        in_specs=[pl.BlockSpec((None, 1, L), lambda h: (h, 0, 0))] * 2 + [full((rb, L))] * 2,
        out_specs=[pl.BlockSpec((rb, L), lambda h: (h, 0))] * 3,
        out_shape=[jax.ShapeDtypeStruct((L, L), BF16)] * 3,
        compiler_params=_params(("parallel",), VMEM_LIMIT),
        name="dft_tables",
    )(coarse(jnp.cos(pa)), coarse(jnp.sin(pa)), jnp.cos(pb), jnp.sin(pb))


def _hy_features(L):
    nn = jnp.arange(L, dtype=F32)[:, None]
    t = jnp.linspace(0.0, 1.0, L, dtype=F32)[:, None]
    bands = jnp.linspace(1e-4, HY_BANDS - 1, HY_BANDS, dtype=F32)[None, :]
    ang = (2.0 * math.pi / L) * nn * bands
    z = jnp.concatenate([t, jnp.cos(ang), -jnp.sin(ang)], axis=-1)
    z = jnp.pad(z, ((0, 0), (0, 40 - HY_EMB)))
    deltas = jnp.abs(jnp.linspace(math.log(1e-2) / 1.5, math.log(1e-2) / 0.3, HY_W, dtype=F32))
    decay = jnp.exp(-t * deltas)
    bwd = jnp.where(jnp.arange(L)[:, None] == 0, 0.0, decay)
    return z, jnp.concatenate([decay, bwd], axis=1)


def _hy_filter_body(z_ref, w1_ref, b1_ref, fr_ref, w2_ref, b2_ref, w3_ref, dec_ref, c_ref, s_ref,
                    kre_ref, kim_ref, h_scr, *, L, fb):
    f = pl.program_id(0)

    @pl.when(f == 0)
    def _():
        fr = fr_ref[...]
        h = jnp.sin(fr * (jnp.dot(z_ref[...], w1_ref[...], precision=HI, preferred_element_type=F32)
                          + b1_ref[...]))
        h = jnp.sin(fr * (jnp.dot(h, w2_ref[...], precision=HI, preferred_element_type=F32) + b2_ref[...]))
        h = jnp.dot(h, w3_ref[...], precision=HI, preferred_element_type=F32) * dec_ref[...]
        h_scr[...] = h.astype(BF16)

    hb = h_scr[...]
    a_re = jnp.dot(c_ref[...], hb, preferred_element_type=F32)
    a_im = jnp.dot(s_ref[...], hb, preferred_element_type=F32)
    grow = f * fb + lax.broadcasted_iota(I32, (fb, 1), 0)
    packed = grow == 0
    sc = jnp.where(packed, 1.0 / (2 * L), 2.0 / (2 * L))
    kre_ref[...] = (a_re[:, :HY_W] + a_re[:, HY_W:]) * sc
    kim_ref[...] = (a_im[:, :HY_W] + jnp.where(packed, 1.0, -1.0) * a_im[:, HY_W:]) * sc


def _hy_filter(L, feats, mats, w1, b1, freq, w2, b2, w3):
    z, dec = feats
    c, s, _ = mats
    fb = min(L, 512)
    full = lambda shape: pl.BlockSpec(shape, lambda f: (0,) * len(shape))
    w1p = jnp.pad(w1, ((0, 40 - HY_EMB), (0, 0)))
    return pl.pallas_call(
        functools.partial(_hy_filter_body, L=L, fb=fb),
        grid=(L // fb,),
        in_specs=[full((L, 40)), full((40, HY_FF)), full((1, HY_FF)), full((1, HY_FF)),
                  full((HY_FF, HY_FF)), full((1, HY_FF)), full((HY_FF, 2 * HY_W)), full((L, 2 * HY_W)),
                  pl.BlockSpec((fb, L), lambda f: (f, 0)), pl.BlockSpec((fb, L), lambda f: (f, 0))],
        out_specs=[pl.BlockSpec((fb, HY_W), lambda f: (f, 0))] * 2,
        out_shape=[jax.ShapeDtypeStruct((L, HY_W), F32)] * 2,
        scratch_shapes=[pltpu.VMEM((L, 2 * HY_W), BF16)],
        compiler_params=_params(("arbitrary",), VMEM_LIMIT),
        name="hy_filter",
    )(z, w1p, b1.reshape(1, -1), freq.reshape(1, -1), w2, b2.reshape(1, -1), w3, dec, c, s)


def _hy_conv_body(v_ref, cr_ref, sr_ref, cc_ref, sc_ref, kre_ref, kim_ref, o_ref, *, fb):
    f = pl.program_id(1)
    vb = v_ref[...]
    vre = jnp.dot(cr_ref[...], vb, preferred_element_type=F32)
    vim = jnp.dot(sr_ref[...], vb, preferred_element_type=F32)
    kre = kre_ref[...]
    kim = kim_ref[...]
    grow = f * fb + lax.broadcasted_iota(I32, (fb, 1), 0)
    packed = grow == 0
    yre = vre * kre - jnp.where(packed, 0.0, vim * kim)
    yim = jnp.where(packed, vim * kim, vre * kim + vim * kre)
    part = (jnp.dot(cc_ref[...], yre.astype(BF16), preferred_element_type=F32)
            + jnp.dot(sc_ref[...], yim.astype(BF16), preferred_element_type=F32))

    @pl.when(f == 0)
    def _():
        o_ref[...] = part

    @pl.when(f > 0)
    def _():
        o_ref[...] += part


def _hy_conv(vgb, mats, kre, kim, L, nseq, row_off):
    c, s, st = mats
    fb = min(L, 256)
    off = row_off // L
    return pl.pallas_call(
        functools.partial(_hy_conv_body, fb=fb),
        grid=(nseq, L // fb),
        in_specs=[pl.BlockSpec((L, HY_W), lambda b, f: (off + b, 0)),
                  pl.BlockSpec((fb, L), lambda b, f: (f, 0)),
                  pl.BlockSpec((fb, L), lambda b, f: (f, 0)),
                  pl.BlockSpec((L, fb), lambda b, f: (0, f)),
                  pl.BlockSpec((L, fb), lambda b, f: (0, f)),
                  pl.BlockSpec((fb, HY_W), lambda b, f: (f, 0)),
                  pl.BlockSpec((fb, HY_W), lambda b, f: (f, 0))],
        out_specs=pl.BlockSpec((L, HY_W), lambda b, f: (b, 0)),
        out_shape=jax.ShapeDtypeStruct((nseq * L, HY_W), F32),
        compiler_params=_params(("parallel", "arbitrary"), VMEM_LIMIT),
        name="hy_conv",
    )(vgb, c, s, c, st, kre, kim)


def _layer_norm(z, g, b):
    mu = jnp.mean(z, axis=-1, keepdims=True)
    zc = z - mu
    var = jnp.mean(zc * zc, axis=-1, keepdims=True)
    return zc * lax.rsqrt(var + LN_EPS) * g + b


def _post1_body(*refs, even):
    is_ctx = pl.program_id(0) < NCT
    if even:
        (x_ref, m_ref, convc_ref, convs_ref, vg_ref, x0_ref, skip_ref, atc_ref, ats_ref, w_ref, b_ref,
         g_ref, be_ref, rw_ref, rb_ref, x1_ref, xm_ref, lg_ref) = refs
    else:
        (x_ref, m_ref, atc_ref, ats_ref, w_ref, b_ref, g_ref, be_ref, rw_ref, rb_ref, x1_ref, xm_ref,
         lg_ref) = refs
    for r0 in range(0, TM, POST_CHAIN):
        rs = slice(r0, r0 + POST_CHAIN)
        at = jnp.where(is_ctx, atc_ref[rs, :], ats_ref[rs, :])
        if even:
            conv = jnp.where(is_ctx, convc_ref[rs, :], convs_ref[rs, :])
            yh = ((conv + vg_ref[rs, :] * skip_ref[...]) * x0_ref[rs, :]).astype(BF16)
            y = (jnp.dot(yh, w_ref[:HY_W, :], preferred_element_type=F32)
                 + jnp.dot(at, w_ref[HY_W:, :], preferred_element_type=F32))
        else:
            y = jnp.dot(at, w_ref[...], preferred_element_type=F32)
        y = y + b_ref[...]
        x1 = _layer_norm(DN_ALPHA * x_ref[rs, :] + m_ref[2:3, :] * y, g_ref[...], be_ref[...])
        x1_ref[rs, :] = x1
        xm = x1 * (1.0 + m_ref[4:5, :]) + m_ref[3:4, :]
        _store_rows(xm_ref, xm, row0=r0)
        lg_ref[:, rs] = lax.dot_general(rw_ref[...], xm, (((1,), (1,)), ((), ())), precision=HI,
                                        preferred_element_type=F32) + rb_ref[...]


def _post1(x, mods, l, parts, w_bf, b, g, be, rw_t, rb, even):
    row = lambda w: pl.BlockSpec((TM, w), lambda i: (i, 0))
    full = lambda shape: pl.BlockSpec(shape, lambda i: (0,) * len(shape))
    in_specs = [row(D), pl.BlockSpec((None, None, 6, D), lambda i: (l, _mod_row(i), 0, 0))]
    ctx = lambda w: pl.BlockSpec((TM, w), lambda i: (jnp.minimum(i, NCT - 1), 0))
    lat = lambda w: pl.BlockSpec((TM, w), lambda i: (jnp.maximum(i - NCT, 0), 0))
    if even:
        in_specs += [ctx(HY_W), lat(HY_W), row(HY_W), row(HY_W), full((1, HY_W)), ctx(NA_W), lat(NA_W)]
    else:
        in_specs += [ctx(D), lat(D)]
    in_specs += [full((D, D)), full((1, D)), full((1, D)), full((1, D)), full((N_EXPERTS, D)),
                 full((N_EXPERTS, 1))]
    return pl.pallas_call(
        functools.partial(_post1_body, even=even),
        grid=(NT,),
        in_specs=in_specs,
        out_specs=[row(D), pl.BlockSpec((TM * SUB, LANE), lambda i: (i, 0)),
                   pl.BlockSpec((None, N_EXPERTS, TM), lambda i: (i, 0, 0))],
        out_shape=[jax.ShapeDtypeStruct((T, D), F32), jax.ShapeDtypeStruct((T * SUB, LANE), F32),
                   jax.ShapeDtypeStruct((NT, N_EXPERTS, TM), F32)],
        compiler_params=_params(("parallel",), VMEM_LIMIT),
        name="post1",
    )(x, mods, *parts, w_bf, b.reshape(1, D), g.reshape(1, D), be.reshape(1, D), rw_t,
      rb.reshape(N_EXPERTS, 1))


def _route_body(lg_ref, slot_ref, gate_ref, pe_ref, pd_ref, fp_ref, be_ref, rb_ref, nx_ref, par_ref, eidx_scr,
                rank_scr):
    eio = lax.broadcasted_iota(I32, (N_EXPERTS, TM), 0)
    ri = lax.broadcasted_iota(I32, (TM, TM), 0)
    ci = lax.broadcasted_iota(I32, (TM, TM), 1)
    upper = jnp.where(ri < ci, 1.0, 0.0).astype(BF16)

    def tile(i, carry):
        lg = lg_ref[i]
        sel = jnp.zeros((N_EXPERTS, TM), F32)
        vals, hots = [], []
        for k in range(TOP_K):
            mx = jnp.max(lg, axis=0, keepdims=True)
            idx = jnp.min(jnp.where(lg == mx, eio, N_EXPERTS), axis=0, keepdims=True)
            hot = eio == idx
            lg = jnp.where(hot, -jnp.inf, lg)
            sel = sel + jnp.where(hot, 1.0, 0.0)
            vals.append(mx)
            hots.append(hot)
            eidx_scr[i, k:k + 1, :] = idx
        rank = jnp.dot(sel.astype(BF16), upper, preferred_element_type=F32) + carry
        ex = [jnp.exp(v - vals[0]) for v in vals]
        den = ex[0] + ex[1] + ex[2] + ex[3]
        for k in range(TOP_K):
            gate_ref[i, k:k + 1, :] = ex[k] / den
            rank_scr[i, k:k + 1, :] = jnp.sum(jnp.where(hots[k], rank, 0.0), axis=0, keepdims=True)
        return carry + jnp.sum(sel, axis=1, keepdims=True)

    counts = lax.fori_loop(0, NT, tile, jnp.zeros((N_EXPERTS, 1), F32))
    padded = jnp.floor((counts + (MOE_BM - 1)) * (1.0 / MOE_BM)) * MOE_BM
    e_r = lax.broadcasted_iota(I32, (N_EXPERTS, N_EXPERTS), 0)
    e_c = lax.broadcasted_iota(I32, (N_EXPERTS, N_EXPERTS), 1)
    incl = jnp.where(e_c <= e_r, 1.0, 0.0)
    padded_b = jnp.broadcast_to(padded, (N_EXPERTS, LANE))
    pad_end = jnp.dot(incl, padded_b, precision=HI, preferred_element_type=F32)
    pad_start = pad_end[:, 0:1] - padded
    pe_ref[...] = pad_end.astype(I32)
    fp_ref[...] = jnp.broadcast_to(pad_start + counts, (N_EXPERTS, LANE)).astype(I32)
    pd_ref[...] = padded_b.astype(I32)
    blk0 = (lax.broadcasted_iota(I32, (1, TM), 1) * MOE_BM).astype(F32)
    nle = jnp.sum(jnp.where(pad_end[:, 0:1] <= blk0, 1.0, 0.0), axis=0, keepdims=True)
    blk_e = jnp.minimum(nle, N_EXPERTS - 1.0)
    be_ref[...] = blk_e.astype(I32)
    mine = eio.astype(F32) == blk_e
    cnt_b = jnp.sum(jnp.where(mine, counts, 0.0), axis=0, keepdims=True)
    start_b = jnp.sum(jnp.where(mine, pad_start, 0.0), axis=0, keepdims=True)
    rb_ref[...] = jnp.clip(cnt_b - (blk0 - start_b), 0.0, float(MOE_BM)).astype(I32)
    used = padded > 0.0
    e_f = eio.astype(F32)
    nx_ref[...] = jnp.min(jnp.where((e_f > blk_e) & used, e_f, float(N_EXPERTS)), axis=0,
                          keepdims=True).astype(I32)
    ordinal = jnp.sum(jnp.where((e_f < blk_e) & used, 1.0, 0.0), axis=0, keepdims=True)
    par_ref[...] = (ordinal - 2.0 * jnp.floor(ordinal * 0.5)).astype(I32)

    def place(i, c):
        for k in range(TOP_K):
            hot = eio == eidx_scr[i, k:k + 1, :]
            start = jnp.sum(jnp.where(hot, pad_start, 0.0), axis=0, keepdims=True)
            slot_ref[i, k:k + 1, :] = (start + rank_scr[i, k:k + 1, :]).astype(I32)
        return c

    lax.fori_loop(0, NT, place, 0)


def _route(logits):
    return pl.pallas_call(
        _route_body,
        out_shape=[jax.ShapeDtypeStruct((NT, TOP_K, TM), I32), jax.ShapeDtypeStruct((NT, TOP_K, TM), F32),
                   ] + [jax.ShapeDtypeStruct((N_EXPERTS, LANE), I32)] * 3 + [jax.ShapeDtypeStruct((1, TM), I32)] * 4,
        scratch_shapes=[pltpu.VMEM((NT, TOP_K, TM), I32), pltpu.VMEM((NT, TOP_K, TM), F32)],
        compiler_params=_params(None, VMEM_LIMIT),
        name="route",
    )(logits)


def _inverse_body(slot_ref, pe_ref, fp_ref, inv_ref):
    def clear(s, c):
        inv_ref[s] = 0
        return c

    for e in range(N_EXPERTS):
        lax.fori_loop(fp_ref[e], pe_ref[e], clear, 0)

    def clear_block(b, c):
        for r in range(ROW_DMA_UNROLL):
            inv_ref[b * ROW_DMA_UNROLL + r] = 0
        return c

    lax.fori_loop(pe_ref[N_EXPERTS - 1] // ROW_DMA_UNROLL, N_SLOTS // ROW_DMA_UNROLL, clear_block, 0)

    def place(a, c):
        inv_ref[slot_ref[a]] = a
        return c

    lax.fori_loop(0, T * TOP_K, place, 0, unroll=ROW_DMA_UNROLL)


def _inverse_map(slot_flat, pe, fp):
    return pl.pallas_call(
        _inverse_body,
        grid_spec=pltpu.PrefetchScalarGridSpec(
            num_scalar_prefetch=3, grid=(1,), in_specs=[],
            out_specs=pl.BlockSpec(memory_space=pltpu.SMEM)),
        out_shape=jax.ShapeDtypeStruct((N_SLOTS,), I32),
        compiler_params=_params(("arbitrary",)),
        name="moe_inverse",
    )(slot_flat, pe, fp)


def _expert_body(be_ref, nu_ref, rb_ref, nx_ref, par_ref, inv_ref, xm_hbm, guw_hbm, gub_ref, dnw_hbm, dnb_ref,
                 yb_ref, xg0, xg1, gu_f32, dn_f32, gu_scr, dn_scr, sem, gsem, *, l):
    i = pl.program_id(0)
    n_used = nu_ref[0]
    rows = rb_ref[i]
    buf = i % 2

    xgs = (xg0, xg1)

    def row_copy(blk, r, b):
        a = inv_ref[blk * MOE_BM + r]
        tok = (lax.shift_right_logical(a, LOG2_ASSIGN_TILE) << LOG2_TM) | (a & (TM - 1))
        dst = r * SUB if isinstance(r, int) else pl.multiple_of(r * SUB, SUB)
        return pltpu.make_async_copy(xm_hbm.at[pl.ds(pl.multiple_of(tok * SUB, SUB), SUB)],
                                     xgs[b].at[pl.ds(dst, SUB)], gsem.at[b])

    def fetch_unrolled(blk, b):
        for r in range(MOE_BM):
            row_copy(blk, r, b).start(priority=r % DMA_PRIORITIES)

    def drain(blk, b):
        def one(r, c):
            row_copy(blk, r, b).wait()
            return c
        lax.fori_loop(0, MOE_BM, one, 0, unroll=ROW_DMA_UNROLL)

    def weight_copies(e, slot):
        return (pltpu.make_async_copy(guw_hbm.at[l, e], gu_f32.at[slot], sem.at[0, slot]),
                pltpu.make_async_copy(dnw_hbm.at[l, e], dn_f32.at[slot], sem.at[1, slot]))

    def chain(r0, p):
        xb = _load_rows(xgs[p], MOE_CHAIN, row0=r0).astype(BF16)
        hgu = jnp.dot(xb, gu_scr[...], preferred_element_type=F32) + gub_ref[...]
        g = jnp.minimum(hgu[:, :D_FF], SWIGLU_LIMIT)
        lin = jnp.clip(hgu[:, D_FF:], -SWIGLU_LIMIT, SWIGLU_LIMIT)
        act = (lin + 1.0) * (g / (1.0 + jnp.exp(-SWIGLU_ALPHA * g)))
        y = jnp.dot(act.astype(BF16), dn_scr[...], preferred_element_type=F32) + dnb_ref[...]
        _store_rows(yb_ref, y, row0=r0)

    @pl.when(i >= n_used)
    def _():
        yb_ref[...] = jnp.zeros_like(yb_ref)

    @pl.when(i < n_used)
    def _():
        e = be_ref[i]
        slot = par_ref[i]
        nxt_blk = jnp.minimum(i + 1, n_used - 1)

        @pl.when(i == 0)
        def _():
            for cp in weight_copies(e, slot):
                cp.start()

            def first(r, c):
                row_copy(0, r, 0).start()
                return c
            lax.fori_loop(0, MOE_BM, first, 0, unroll=ROW_DMA_UNROLL)

        @pl.when((i == 0) | (e != be_ref[jnp.maximum(i - 1, 0)]))
        def _():
            nxt = nx_ref[i]

            @pl.when(nxt < N_EXPERTS)
            def _():
                for cp in weight_copies(nxt, 1 - slot):
                    cp.start()

            for cp in weight_copies(e, slot):
                cp.wait()
            gu_scr[...] = gu_f32[slot].astype(BF16)
            dn_scr[...] = dn_f32[slot].astype(BF16)

        for p in (0, 1):
            @pl.when(buf == p)
            def _():
                drain(i, p)

                @pl.when(rows > MOE_CHAIN)
                def _():
                    fetch_unrolled(nxt_blk, 1 - p)
                    for r0 in range(0, MOE_BM, MOE_CHAIN):
                        chain(r0, p)

                @pl.when(rows <= MOE_CHAIN)
                def _():
                    fetch_unrolled(nxt_blk, 1 - p)
                    chain(0, p)
                    yb_ref[MOE_CHAIN * SUB:, :] = jnp.zeros((MOE_BM * SUB - MOE_CHAIN * SUB, LANE), F32)

                @pl.when(i == n_used - 1)
                def _():
                    drain(nxt_blk, 1 - p)


def _experts(block_e, n_used, block_rows, next_e, parity, inv, xm, l, gu_w, gu_b, dn_w, dn_b):
    def bias(width):
        return pl.BlockSpec((None, None, 1, width),
                            lambda i, be, nu, rb, nx, par, inv: (l, be[jnp.minimum(i, nu[0] - 1)], 0, 0))

    return pl.pallas_call(
        functools.partial(_expert_body, l=l),
        grid_spec=pltpu.PrefetchScalarGridSpec(
            num_scalar_prefetch=6, grid=(N_SLOT_BLOCKS,),
            in_specs=[pl.BlockSpec(memory_space=pl.ANY),
                      pl.BlockSpec(memory_space=pl.ANY), bias(2 * D_FF),
                      pl.BlockSpec(memory_space=pl.ANY), bias(D)],
            out_specs=pl.BlockSpec((MOE_BM * SUB, LANE), lambda i, be, nu, rb, nx, par, inv: (i, 0)),
            scratch_shapes=[pltpu.VMEM((MOE_BM * SUB, LANE), F32), pltpu.VMEM((MOE_BM * SUB, LANE), F32),
                            pltpu.VMEM((2, D, 2 * D_FF), F32), pltpu.VMEM((2, D_FF, D), F32),
                            pltpu.VMEM((D, 2 * D_FF), BF16), pltpu.VMEM((D_FF, D), BF16),
                            pltpu.SemaphoreType.DMA((2, 2)), pltpu.SemaphoreType.DMA((2,))]),
        out_shape=jax.ShapeDtypeStruct((N_SLOTS * SUB, LANE), F32),
        compiler_params=_params(("arbitrary",), VMEM_LIMIT),
        name="moe_experts",
    )(block_e, n_used, block_rows, next_e, parity, inv, xm, gu_w, gu_b.reshape(DEPTH, N_EXPERTS, 1, 2 * D_FF), dn_w,
      dn_b.reshape(DEPTH, N_EXPERTS, 1, D))


def _post2_body(slot_ref, x1_ref, gt_ref, m_ref, g_ref, be_ref, yb_ref, x2_ref, y_scr, sem):
    i = pl.program_id(0)
    buf = i % 2

    def row_copy(t, r, k):
        s = slot_ref[t * (TOP_K * TM) + k * TM + r]
        return pltpu.make_async_copy(yb_ref.at[pl.ds(pl.multiple_of(s * SUB, SUB), SUB)],
                                     y_scr.at[t % 2, k, pl.ds(pl.multiple_of(r * SUB, SUB), SUB)],
                                     sem.at[t % 2])

    def fetch(t):
        def issue(r, c):
            for k in range(TOP_K):
                row_copy(t, r, k).start(priority=k % DMA_PRIORITIES)
            return c
        lax.fori_loop(0, TM, issue, 0, unroll=ROW_DMA_UNROLL)

    @pl.when(i == 0)
    def _():
        fetch(i)

    @pl.when(i + 1 < NT)
    def _():
        fetch(i + 1)

    def drain(r, c):
        for k in range(TOP_K):
            row_copy(i, r, k).wait()
        return c

    lax.fori_loop(0, TM, drain, 0, unroll=ROW_DMA_UNROLL)
    f = _load_rows(y_scr, TM, (buf, 0)) * gt_ref[:, 0:1]
    for k in range(1, TOP_K):
        f = f + _load_rows(y_scr, TM, (buf, k)) * gt_ref[:, k:k + 1]
    x2_ref[...] = _layer_norm(DN_ALPHA * x1_ref[...] + m_ref[5:6, :] * f, g_ref[...], be_ref[...])


def _post2(slot_flat, x1, gates_t, mods, l, g, be, yb):
    full = lambda shape: pl.BlockSpec(shape, lambda i, s: (0,) * len(shape))
    return pl.pallas_call(
        _post2_body,
        grid_spec=pltpu.PrefetchScalarGridSpec(
            num_scalar_prefetch=1, grid=(NT,),
            in_specs=[pl.BlockSpec((TM, D), lambda i, s: (i, 0)),
                      pl.BlockSpec((TM, TOP_K), lambda i, s: (i, 0)),
                      pl.BlockSpec((None, None, 6, D), lambda i, s: (l, _mod_row(i), 0, 0)),
                      full((1, D)), full((1, D)),
                      pl.BlockSpec(memory_space=pl.ANY)],
            out_specs=pl.BlockSpec((TM, D), lambda i, s: (i, 0)),
            scratch_shapes=[pltpu.VMEM((2, TOP_K, TM * SUB, LANE), F32), pltpu.SemaphoreType.DMA((2,))]),
        out_shape=jax.ShapeDtypeStruct((T, D), F32),
        compiler_params=_params(("arbitrary",), VMEM_LIMIT),
        name="post2",
    )(slot_flat, x1, gates_t, mods, g.reshape(1, D), be.reshape(1, D), yb)


def _rope_tables():
    quarter = HEAD // 4
    pos = jnp.arange(DEC_SEQ)
    rows = (pos // GRID_W).astype(F32)[:, None]
    cols = (pos % GRID_W).astype(F32)[:, None]
    lane = jnp.arange(LANE)[None, :]
    d = lane % HEAD
    inv = ROPE_THETA ** (-(d % quarter).astype(F32) / quarter)
    ang = jnp.where(d < HEAD // 2, rows, cols) * inv
    sign = jnp.where((d // quarter) % 2 == 0, -1.0, 1.0)
    cos = jnp.concatenate([jnp.ones((TC, LANE), F32), jnp.tile(jnp.cos(ang), (DEC_BATCH, 1))], axis=0)
    sin = jnp.concatenate([jnp.zeros((TC, LANE), F32), jnp.tile(sign * jnp.sin(ang), (DEC_BATCH, 1))], axis=0)
    return cos, sin


def _moe_and_norm(x1, xm, logits, mods, l, ln_g, ln_b, exp_gu_w, exp_gu_b, exp_dn_w, exp_dn_b):
    slot, gates, pe, _, fp, be, rb, nx, par = _route(logits)
    slot_flat = slot.reshape(-1)
    pe1 = pe[:, 0]
    inv = _inverse_map(slot_flat, pe1, fp[:, 0])
    n_used = pe1[N_EXPERTS - 1:] // MOE_BM
    yb = _experts(be.reshape(-1), n_used, rb.reshape(-1), nx.reshape(-1), par.reshape(-1), inv, xm, l,
                  exp_gu_w, exp_gu_b, exp_dn_w, exp_dn_b)
    gates_t = gates.transpose(0, 2, 1).reshape(T, TOP_K)
    return _post2(slot_flat, x1, gates_t, mods, l, ln_g[l, 1], ln_b[l, 1], yb)


def kernel(x_prompt, x_sample, cache_na_k, cache_na_v, cache_swa_k, cache_swa_v, c, c_ctx, ada_w, ada_b, ln_g, ln_b, ev_in_w, ev_in_b, hy_conv_w, hy_conv_b, hy_f_w1, hy_f_b1, hy_f_freq, hy_f_w2, hy_f_b2, hy_f_w3, hy_skip, na_rpb, od_in_w, od_in_b, swa_sink, mix_out_w, mix_out_b, router_w, router_b, exp_gu_w, exp_gu_b, exp_dn_w, exp_dn_b):
    x = jnp.concatenate([x_prompt.reshape(TC, D), x_sample.reshape(TS, D)], axis=0)
    cond8 = jnp.concatenate([c_ctx[None, :], c, jnp.zeros((8 - 1 - DEC_BATCH, D), F32)], axis=0)
    mods = _adaln(cond8, ada_w, ada_b).reshape(DEPTH, 8, 6, D)

    rope_tabs = _rope_tables()
    hy_consts = {L: (_hy_features(L), _dft_mats(L)) for L in (SEQ, DEC_SEQ)}
    ck_na = cache_na_k.reshape(DEC_BATCH, -1, PAST, NA_W)
    cv_na = cache_na_v.reshape(DEC_BATCH, -1, PAST, NA_W)
    ck_swa = cache_swa_k.reshape(DEC_BATCH, -1, PAST, SWA_KV * HEAD)
    cv_swa = cache_swa_v.reshape(DEC_BATCH, -1, PAST, SWA_KV * HEAD)

    na_k, na_v, swa_k, swa_v = [], [], [], []
    for l in range(DEPTH):
        w_out = mix_out_w[l].astype(BF16)
        rw_t = router_w[l].T
        if l % 2 == 0:
            e = l // 2
            hy, q, k, v = _inproj(x, mods, l, ev_in_w[e].astype(BF16), ev_in_b[e],
                                  ((HY_PROJ, False), (NA_W, False), (NA_W, False), (NA_W, False)))
            x0, vg, vgb = _hy_gate(hy, hy_conv_w[e].reshape(3, HY_PROJ), hy_conv_b[e].reshape(1, HY_PROJ))
            convs = []
            for L, nseq, off in ((SEQ, BATCH, 0), (DEC_SEQ, DEC_BATCH, TC)):
                feats, mats = hy_consts[L]
                kre, kim = _hy_filter(L, feats, mats, hy_f_w1[e], hy_f_b1[e], hy_f_freq[e], hy_f_w2[e],
                                      hy_f_b2[e], hy_f_w3[e])
                convs.append(_hy_conv(vgb, mats, kre, kim, L, nseq, off))
            parts = (convs[0], convs[1], vg, x0, hy_skip[e].reshape(1, HY_W), _ctx_even_attn(q, k, v),
                     _na_attn(q, k, v, ck_na, cv_na, _na_biasmask(na_rpb[e]), e))
            na_k.append(k[:TC].reshape(BATCH, SEQ, NA_HEADS, HEAD))
            na_v.append(v[:TC].reshape(BATCH, SEQ, NA_HEADS, HEAD))
        else:
            o = l // 2
            q, k, v = _inproj(x, mods, l, od_in_w[o].astype(BF16), od_in_b[o],
                              ((SWA_HEADS * HEAD, True), (SWA_KV * HEAD, True), (SWA_KV * HEAD, False)),
                              rope_tabs)
            parts = (_ctx_odd_attn(swa_sink[o], q, k, v), _swa_attn(swa_sink[o], q, k, v, ck_swa, cv_swa, o))
            swa_k.append(k[:TC].reshape(BATCH, SEQ, SWA_KV, HEAD))
            swa_v.append(v[:TC].reshape(BATCH, SEQ, SWA_KV, HEAD))
        x1, xm, logits = _post1(x, mods, l, parts, w_out, mix_out_b[l], ln_g[l, 0], ln_b[l, 0], rw_t,
                                router_b[l], l % 2 == 0)
        x = _moe_and_norm(x1, xm, logits, mods, l, ln_g, ln_b, exp_gu_w, exp_gu_b, exp_dn_w, exp_dn_b)

    return (x[:TC].reshape(BATCH, SEQ, D), x[TC:].reshape(DEC_BATCH, DEC_SEQ, D),
            jnp.stack(na_k, axis=1), jnp.stack(na_v, axis=1), jnp.stack(swa_k, axis=1), jnp.stack(swa_v, axis=1))
```

```python
import functools
import math

import jax
import jax.numpy as jnp
from jax import lax
from jax.experimental import pallas as pl
from jax.experimental.pallas import tpu as pltpu

F32 = jnp.float32
BF16 = jnp.bfloat16
I32 = jnp.int32
HI = lax.Precision.HIGHEST

D = 1024
DEPTH = 4
BATCH, SEQ = 16, 256
DEC_BATCH, DEC_SEQ = 4, 2048
PAST = 256
GRID_W = 64
HEAD = 64
HY_W = 512
HY_PROJ = 3 * HY_W
HY_BANDS = 16
HY_EMB = 1 + 2 * HY_BANDS
HY_FF = 64
NA_HEADS = 8
NA_W = NA_HEADS * HEAD
NA_KH, NA_KW = 8, 16
SWA_HEADS, SWA_KV = 16, 4
SWA_WINDOW = 128
N_EXPERTS, TOP_K = 32, 4
D_FF = 1024
SWIGLU_LIMIT = 7.0
SWIGLU_ALPHA = 1.702
DN_ALPHA = (2 * DEPTH) ** 0.25
LN_EPS = 1e-5
NEG = -1e30
LOG2E = 1.0 / math.log(2.0)
ROPE_THETA = 10000.0

TC = BATCH * SEQ
TS = DEC_BATCH * DEC_SEQ
T = TC + TS
TM = 256
NT = T // TM
NCT = TC // TM
TPS = DEC_SEQ // TM
LANE = 128
MOE_BM = 512
MOE_CHAIN = 256
N_SLOT_BLOCKS = T * TOP_K // MOE_BM + N_EXPERTS
N_SLOTS = N_SLOT_BLOCKS * MOE_BM
VMEM_LIMIT = 56 * 1024 * 1024
DMA_PRIORITIES = 2
ROW_DMA_UNROLL = 8
POST_CHAIN = 128


def _params(sem, vmem=None):
    return pltpu.CompilerParams(dimension_semantics=sem, vmem_limit_bytes=vmem)


def _mod_row(i):
    return jnp.where(i < NCT, 0, 1 + (i - NCT) // TPS)


SUB = 8
ROW_TILES = D // LANE


def _load_rows(ref, nrows, lead=(), row0=0):
    parts = [ref[lead + (pl.ds(row0 * ROW_TILES + s, nrows, stride=ROW_TILES), slice(None))]
             for s in range(ROW_TILES)]
    return jnp.concatenate(parts, axis=1)


def _store_rows(ref, val, row0=0):
    nrows = val.shape[0]
    for s in range(ROW_TILES):
        ref[pl.ds(row0 * ROW_TILES + s, nrows, stride=ROW_TILES), :] = val[:, s * LANE:(s + 1) * LANE]


def _adaln_body(c_ref, w_ref, b_ref, o_ref):
    c = c_ref[...]
    s = c / (1.0 + jnp.exp(-c))
    o_ref[...] = jnp.dot(s, w_ref[...], precision=HI, preferred_element_type=F32) + b_ref[...]


def _adaln(cond8, ada_w, ada_b):
    nt = 1536
    return pl.pallas_call(
        _adaln_body,
        grid=(DEPTH, 6 * D // nt),
        in_specs=[pl.BlockSpec((8, D), lambda l, j: (0, 0)),
                  pl.BlockSpec((None, D, nt), lambda l, j: (l, 0, j)),
                  pl.BlockSpec((None, 1, nt), lambda l, j: (l, 0, j))],
        out_specs=pl.BlockSpec((None, 8, nt), lambda l, j: (l, 0, j)),
        out_shape=jax.ShapeDtypeStruct((DEPTH, 8, 6 * D), F32),
        compiler_params=_params(("arbitrary", "arbitrary"), VMEM_LIMIT),
        name="adaln",
    )(cond8, ada_w, ada_b.reshape(DEPTH, 1, 6 * D))


def _rope(x, cos, sin):
    lane = lax.broadcasted_iota(I32, (1, LANE), 1)
    first = (lane & 16) == 0
    outs = []
    for j in range(x.shape[1] // LANE):
        blk = x[:, j * LANE:(j + 1) * LANE]
        partner = jnp.where(first, pltpu.roll(blk, LANE - 16, 1), pltpu.roll(blk, 16, 1))
        outs.append(blk * cos + partner * sin)
    return jnp.concatenate(outs, axis=1) if len(outs) > 1 else outs[0]


def _inproj_body(*refs, splits, rope):
    if rope:
        x_ref, m_ref, w_ref, b_ref, cos_ref, sin_ref = refs[:6]
        outs = refs[6:]
    else:
        x_ref, m_ref, w_ref, b_ref = refs[:4]
        outs = refs[4:]
    h = x_ref[...] * (1.0 + m_ref[1:2, :]) + m_ref[0:1, :]
    y = jnp.dot(h.astype(BF16), w_ref[...], preferred_element_type=F32) + b_ref[...]
    off = 0
    for o_ref, (width, do_rope, _) in zip(outs, splits):
        part = y[:, off:off + width]
        if do_rope:
            part = _rope(part, cos_ref[...], sin_ref[...])
        o_ref[...] = part.astype(o_ref.dtype)
        off += width


def _inproj(x, mods, l, w_bf, b, splits, rope_tabs=None):
    p = w_bf.shape[1]
    in_specs = [pl.BlockSpec((TM, D), lambda i: (i, 0)),
                pl.BlockSpec((None, None, 6, D), lambda i: (l, _mod_row(i), 0, 0)),
                pl.BlockSpec((D, p), lambda i: (0, 0)),
                pl.BlockSpec((1, p), lambda i: (0, 0))]
    args = [x, mods, w_bf, b.reshape(1, p)]
    if rope_tabs is not None:
        in_specs += [pl.BlockSpec((TM, LANE), lambda i: (i, 0))] * 2
        args += list(rope_tabs)
    return pl.pallas_call(
        functools.partial(_inproj_body, splits=splits, rope=rope_tabs is not None),
        grid=(NT,),
        in_specs=in_specs,
        out_specs=[pl.BlockSpec((TM, w), lambda i: (i, 0)) for w, _, _ in splits],
        out_shape=[jax.ShapeDtypeStruct((T, w), dt) for w, _, dt in splits],
        compiler_params=_params(("parallel",), VMEM_LIMIT),
        name="inproj",
    )(*args)


HALO = SUB


def _inproj_even_body(xp_ref, x_ref, xn_ref, m_ref, w_ref, b_ref, cw_ref, cb_ref,
                      x0_ref, vg_ref, vgb_ref, q_ref, k_ref, v_ref):
    i = pl.program_id(0)
    ti = (i - NCT) % TPS
    first = (i < NCT) | (ti == 0)
    last = (i < NCT) | (ti == TPS - 1)
    xs = jnp.concatenate([xp_ref[...], x_ref[...], xn_ref[...]], axis=0)
    h = xs * (1.0 + m_ref[1:2, :]) + m_ref[0:1, :]
    y = jnp.dot(h.astype(BF16), w_ref[...], preferred_element_type=F32) + b_ref[...]
    u = y[:, :HY_PROJ]
    row = lax.broadcasted_iota(I32, (TM, 1), 0)
    before = jnp.where((row == 0) & first, 0.0, u[HALO - 1:HALO - 1 + TM])
    after = jnp.where((row == TM - 1) & last, 0.0, u[HALO + 1:HALO + 1 + TM])
    uc = cw_ref[0:1, :] * before + cw_ref[1:2, :] * u[HALO:HALO + TM] + cw_ref[2:3, :] * after + cb_ref[...]
    x0_ref[...] = uc[:, :HY_W]
    vg = uc[:, 2 * HY_W:] * uc[:, HY_W:2 * HY_W]
    vg_ref[...] = vg
    vgb_ref[...] = vg.astype(BF16)
    qkv = y[HALO:HALO + TM, HY_PROJ:]
    q_ref[...] = qkv[:, :NA_W].astype(q_ref.dtype)
    k_ref[...] = qkv[:, NA_W:2 * NA_W]
    v_ref[...] = qkv[:, 2 * NA_W:]


def _inproj_even(x, mods, l, w_bf, b, conv_w, conv_b):
    p = w_bf.shape[1]
    nb = TM // HALO
    full = lambda shape: pl.BlockSpec(shape, lambda i: (0,) * len(shape))
    row = lambda w: pl.BlockSpec((TM, w), lambda i: (i, 0))
    return pl.pallas_call(
        _inproj_even_body,
        grid=(NT,),
        in_specs=[pl.BlockSpec((HALO, D), lambda i: (jnp.maximum(i * nb - 1, 0), 0)),
                  row(D),
                  pl.BlockSpec((HALO, D), lambda i: (jnp.minimum(i * nb + nb, T // HALO - 1), 0)),
                  pl.BlockSpec((None, None, 6, D), lambda i: (l, _mod_row(i), 0, 0)),
                  full((D, p)), full((1, p)), full((3, HY_PROJ)), full((1, HY_PROJ))],
        out_specs=[row(HY_W)] * 3 + [row(NA_W)] * 3,
        out_shape=[jax.ShapeDtypeStruct((T, HY_W), F32), jax.ShapeDtypeStruct((T, HY_W), F32),
                   jax.ShapeDtypeStruct((T, HY_W), BF16), jax.ShapeDtypeStruct((T, NA_W), BF16),
                   jax.ShapeDtypeStruct((T, NA_W), F32), jax.ShapeDtypeStruct((T, NA_W), F32)],
        compiler_params=_params(("parallel",), VMEM_LIMIT),
        name="inproj_even",
    )(x, x, x, mods, w_bf, b.reshape(1, p), conv_w, conv_b)


def _lo_mask():
    return lax.broadcasted_iota(I32, (1, LANE), 1) < HEAD


def _dup_half(x2, upper):
    swapped = pltpu.roll(x2, HEAD, 1)
    keep = jnp.logical_xor(_lo_mask(), upper)
    return jnp.where(keep, x2, swapped)


def _pair_attention(q2, segs, sink_lo=None, sink_hi=None):
    m_rows = q2.shape[0]
    lo = _lo_mask()
    q2 = q2.astype(F32) * (HEAD ** -0.5 * LOG2E)
    qs = jnp.concatenate([jnp.where(lo, q2, 0.0), jnp.where(lo, 0.0, q2)], axis=0).astype(BF16)
    scores = []
    for kd, _, bias in segs:
        s = lax.dot_general(qs, kd, (((1,), (1,)), ((), ())), preferred_element_type=F32)
        if bias is not None:
            s = s + bias
        scores.append(s)
    mx = functools.reduce(jnp.maximum, [jnp.max(s, axis=-1, keepdims=True) for s in scores])
    sink = None
    if sink_lo is not None:
        row = lax.broadcasted_iota(I32, (2 * m_rows, 1), 0)
        sink = jnp.where(row < m_rows, sink_lo, sink_hi) * LOG2E
        mx = jnp.maximum(mx, sink)
    den = jnp.zeros_like(mx)
    acc = jnp.zeros((2 * m_rows, LANE), F32)
    for s, (_, vd, _) in zip(scores, segs):
        p = jnp.exp2(s - mx)
        den = den + jnp.sum(p, axis=-1, keepdims=True)
        acc = acc + jnp.dot(p.astype(BF16), vd, preferred_element_type=F32)
    if sink is not None:
        den = den + jnp.exp2(sink - mx)
    o = acc / den
    return jnp.where(lo, o[:m_rows], o[m_rows:])


def _lanes(j, width=LANE):
    return slice(j * width, (j + 1) * width)


def _ctx_even_body(q_ref, k_ref, v_ref, o_ref):
    for j in range(NA_W // LANE):
        segs = [(k_ref[:, _lanes(j)].astype(BF16), v_ref[:, _lanes(j)].astype(BF16), None)]
        o_ref[:, _lanes(j)] = _pair_attention(q_ref[:, _lanes(j)], segs).astype(o_ref.dtype)


def _ctx_even_attn(q, k, v):
    spec = pl.BlockSpec((SEQ, NA_W), lambda b: (b, 0))
    return pl.pallas_call(
        _ctx_even_body,
        grid=(BATCH,),
        in_specs=[spec, spec, spec],
        out_specs=spec,
        out_shape=jax.ShapeDtypeStruct((TC, NA_W), BF16),
        compiler_params=_params(("parallel",)),
        name="ctx_attn_even",
    )(q, k, v)


def _ctx_odd_body(sink_ref, q_ref, k_ref, v_ref, o_ref):
    for g in range(SWA_KV):
        upper = (g % 2) == 1
        kd = _dup_half(k_ref[:, _lanes(g // 2)], upper).astype(BF16)
        vd = _dup_half(v_ref[:, _lanes(g // 2)], upper).astype(BF16)
        for j in (2 * g, 2 * g + 1):
            o_ref[:, _lanes(j)] = _pair_attention(q_ref[:, _lanes(j)], [(kd, vd, None)],
                                                  sink_ref[2 * j], sink_ref[2 * j + 1]).astype(o_ref.dtype)


def _ctx_odd_attn(sink, q, k, v):
    qspec = pl.BlockSpec((SEQ, SWA_HEADS * HEAD), lambda b, s: (b, 0))
    kspec = pl.BlockSpec((SEQ, SWA_KV * HEAD), lambda b, s: (b, 0))
    return pl.pallas_call(
        _ctx_odd_body,
        grid_spec=pltpu.PrefetchScalarGridSpec(
            num_scalar_prefetch=1, grid=(BATCH,),
            in_specs=[qspec, kspec, kspec], out_specs=qspec),
        out_shape=jax.ShapeDtypeStruct((TC, SWA_HEADS * HEAD), BF16),
        compiler_params=_params(("parallel",)),
        name="ctx_attn_odd",
    )(sink, q, k, v)


SWA_BLK = 128
SWA_NB = DEC_SEQ // SWA_BLK


def _swa_body(sink_ref, q_ref, kp_ref, kc_ref, kn_ref, vp_ref, vc_ref, vn_ref, ck_ref, cv_ref, o_ref):
    n = pl.program_id(1)
    qi = lax.broadcasted_iota(I32, (SWA_BLK, 3 * SWA_BLK), 0)
    kj = lax.broadcasted_iota(I32, (SWA_BLK, 3 * SWA_BLK), 1)
    kpos = (n - 1) * SWA_BLK + kj
    rel = kj - SWA_BLK - qi
    ok = (jnp.abs(rel) <= SWA_WINDOW) & (kpos >= 0) & (kpos < DEC_SEQ)
    mask = jnp.where(ok, 0.0, NEG)
    mask2 = jnp.concatenate([mask, mask], axis=0)
    for g in range(SWA_KV):
        upper = (g % 2) == 1
        kv = _lanes(g // 2)
        kloc = jnp.concatenate([kp_ref[:, kv], kc_ref[:, kv], kn_ref[:, kv]], axis=0)
        vloc = jnp.concatenate([vp_ref[:, kv], vc_ref[:, kv], vn_ref[:, kv]], axis=0)
        segs = [(_dup_half(kloc, upper).astype(BF16), _dup_half(vloc, upper).astype(BF16), mask2),
                (_dup_half(ck_ref[:, kv], upper).astype(BF16), _dup_half(cv_ref[:, kv], upper).astype(BF16),
                 None)]
        for j in (2 * g, 2 * g + 1):
            o_ref[:, _lanes(j)] = _pair_attention(q_ref[:, _lanes(j)], segs, sink_ref[2 * j],
                                                  sink_ref[2 * j + 1]).astype(o_ref.dtype)


def _swa_attn(sink, q, k, v, ck, cv, o):
    base = TC // SWA_BLK
    qspec = pl.BlockSpec((SWA_BLK, SWA_HEADS * HEAD), lambda b, n, s: (base + b * SWA_NB + n, 0))

    def kv(shift):
        return pl.BlockSpec(
            (SWA_BLK, SWA_KV * HEAD),
            lambda b, n, s: (base + b * SWA_NB + jnp.clip(n + shift, 0, SWA_NB - 1), 0))

    cspec = pl.BlockSpec((None, None, PAST, SWA_KV * HEAD), lambda b, n, s: (b, o, 0, 0))
    ospec = pl.BlockSpec((SWA_BLK, SWA_HEADS * HEAD), lambda b, n, s: (b * SWA_NB + n, 0))
    return pl.pallas_call(
        _swa_body,
        grid_spec=pltpu.PrefetchScalarGridSpec(
            num_scalar_prefetch=1, grid=(DEC_BATCH, SWA_NB),
            in_specs=[qspec, kv(-1), kv(0), kv(1), kv(-1), kv(0), kv(1), cspec, cspec],
            out_specs=ospec),
        out_shape=jax.ShapeDtypeStruct((TS, SWA_HEADS * HEAD), BF16),
        compiler_params=_params(("parallel", "parallel")),
        name="swa_attn",
    )(sink, q, k, k, k, v, v, v, ck, cv)


NA_QROWS = 4
NA_TILES = DEC_SEQ // TM
NA_KTILES = 3
NA_KSTART_MAX = NA_TILES - NA_KTILES


def _na_body(q_ref, k0_ref, k1_ref, k2_ref, v0_ref, v1_ref, v2_ref, ck_ref, cv_ref, bias_ref, o_ref):
    for j in range(NA_W // LANE):
        ln = _lanes(j)
        kloc = jnp.concatenate([k0_ref[:, ln], k1_ref[:, ln], k2_ref[:, ln]], axis=0).astype(BF16)
        vloc = jnp.concatenate([v0_ref[:, ln], v1_ref[:, ln], v2_ref[:, ln]], axis=0).astype(BF16)
        bias = jnp.concatenate([bias_ref[2 * j], bias_ref[2 * j + 1]], axis=0)
        segs = [(kloc, vloc, bias), (ck_ref[:, ln].astype(BF16), cv_ref[:, ln].astype(BF16), None)]
        o_ref[:, ln] = _pair_attention(q_ref[:, ln], segs).astype(o_ref.dtype)


def _na_attn(q, k, v, ck, cv, biasmask, e):
    base = NCT

    def kstart(i):
        return jnp.clip(i - 1, 0, NA_KSTART_MAX)

    qspec = pl.BlockSpec((TM, NA_W), lambda i, b: (base + b * NA_TILES + i, 0))

    def kv(t):
        return pl.BlockSpec((TM, NA_W), lambda i, b: (base + b * NA_TILES + kstart(i) + t, 0))

    cspec = pl.BlockSpec((None, None, PAST, NA_W), lambda i, b: (b, e, 0, 0))
    bspec = pl.BlockSpec((None, NA_HEADS, TM, NA_KTILES * TM),
                         lambda i, b: (jnp.where(i == 0, 0, jnp.where(i == NA_TILES - 1, 2, 1)), 0, 0, 0))
    ospec = pl.BlockSpec((TM, NA_W), lambda i, b: (b * NA_TILES + i, 0))
    return pl.pallas_call(
        _na_body,
        grid=(NA_TILES, DEC_BATCH),
        in_specs=[qspec, kv(0), kv(1), kv(2), kv(0), kv(1), kv(2), cspec, cspec, bspec],
        out_specs=ospec,
        out_shape=jax.ShapeDtypeStruct((TS, NA_W), BF16),
        compiler_params=_params(("parallel", "parallel"), VMEM_LIMIT),
        name="na_attn",
    )(q, k, k, k, v, v, v, ck, cv, biasmask)


NA_REL_ROWS = 2 * NA_KH - 1
NA_REL_COLS = 2 * NA_KW - 1


def _na_bias_body(rpb_ref, o_ref):
    h = pl.program_id(0)
    rows = DEC_SEQ // GRID_W
    c = lax.broadcasted_iota(I32, (GRID_W, LANE), 0)
    lane = lax.broadcasted_iota(I32, (GRID_W, LANE), 1)
    c2 = lane % GRID_W
    rel = c2 - c + (NA_KW - 1)
    win_lo = jnp.clip(c - NA_KW // 2, 0, GRID_W - NA_KW)
    col_ok = (c2 >= win_lo) & (c2 < win_lo + NA_KW)
    hits = [rel == j for j in range(NA_REL_COLS)]
    outside = jnp.full((GRID_W, LANE), NEG, F32)
    toep = []
    for dr in range(NA_REL_ROWS):
        base = (h * NA_REL_ROWS + dr) * NA_REL_COLS
        t = outside
        for j in range(NA_REL_COLS):
            t = jnp.where(hits[j], rpb_ref[base + j] * LOG2E, t)
        toep.append(jnp.where(col_ok, t, NEG))
    first_half = lane < GRID_W
    for pi, i in enumerate((0, 1, NA_TILES - 1)):
        key_row0 = NA_QROWS * min(max(i - 1, 0), NA_KSTART_MAX)
        for a in range(NA_QROWS):
            r = NA_QROWS * i + a
            rs = min(max(r - NA_KH // 2, 0), rows - NA_KH)

            def block(kr):
                r2 = key_row0 + kr
                return toep[r2 - r + NA_KH - 1] if rs <= r2 < rs + NA_KH else outside

            for s in range(NA_KTILES * NA_QROWS // 2):
                o_ref[pi, a * GRID_W:(a + 1) * GRID_W, s * LANE:(s + 1) * LANE] = jnp.where(
                    first_half, block(2 * s), block(2 * s + 1))


def _na_biasmask(rpb):
    return pl.pallas_call(
        _na_bias_body,
        grid_spec=pltpu.PrefetchScalarGridSpec(
            num_scalar_prefetch=1, grid=(NA_HEADS,), in_specs=[],
            out_specs=pl.BlockSpec((3, None, TM, NA_KTILES * TM), lambda h, s: (0, h, 0, 0))),
        out_shape=jax.ShapeDtypeStruct((3, NA_HEADS, TM, NA_KTILES * TM), F32),
        compiler_params=_params(("parallel",), VMEM_LIMIT),
        name="na_bias",
    )(rpb.reshape(-1))


DFT_ROWS = 256


def _dft_body(ac_ref, as_ref, bc_ref, bs_ref, c_ref, s_ref, st_ref, *, rb):
    h = pl.program_id(0)
    ac, sa, bc, sb = ac_ref[...], as_ref[...], bc_ref[...], bs_ref[...]
    cosv = ac * bc - sa * sb
    nsin = -(sa * bc + ac * sb)
    row = h * rb + lax.broadcasted_iota(I32, (rb, 1), 0)
    col = lax.broadcasted_iota(I32, (1, cosv.shape[1]), 1)
    c_ref[...] = cosv.astype(BF16)
    s_ref[...] = jnp.where(row == 0, jnp.where(col % 2 == 0, 1.0, -1.0), nsin).astype(BF16)
    st_ref[...] = jnp.where(col == 0, jnp.where(row % 2 == 0, 1.0, -1.0), nsin).astype(BF16)


def _dft_mats(L):
    n = 2 * L
    rb = min(L, DFT_ROWS)
    t = jnp.arange(L, dtype=I32)[None, :]
    hi = jnp.arange(L // rb, dtype=I32)[:, None]
    lo = jnp.arange(rb, dtype=I32)[:, None]
    pa = ((rb * hi * t) % n).astype(F32) * (2.0 * math.pi / n)
    pb = ((lo * t) % n).astype(F32) * (2.0 * math.pi / n)
    coarse = lambda x: x.reshape(L // rb, 1, L)
    full = lambda shape: pl.BlockSpec(shape, lambda h: (0,) * len(shape))
    return pl.pallas_call(
        functools.partial(_dft_body, rb=rb),
        grid=(L // rb,),
        in_specs=[pl.BlockSpec((None, 1, L), lambda h: (h, 0, 0))] * 2 + [full((rb, L))] * 2,
        out_specs=[pl.BlockSpec((rb, L), lambda h: (h, 0))] * 3,
        out_shape=[jax.ShapeDtypeStruct((L, L), BF16)] * 3,
        compiler_params=_params(("parallel",), VMEM_LIMIT),
        name="dft_tables",
    )(coarse(jnp.cos(pa)), coarse(jnp.sin(pa)), jnp.cos(pb), jnp.sin(pb))


def _hy_features(L):
    nn = jnp.arange(L, dtype=F32)[:, None]
    t = jnp.linspace(0.0, 1.0, L, dtype=F32)[:, None]
    bands = jnp.linspace(1e-4, HY_BANDS - 1, HY_BANDS, dtype=F32)[None, :]
    ang = (2.0 * math.pi / L) * nn * bands
    z = jnp.concatenate([t, jnp.cos(ang), -jnp.sin(ang)], axis=-1)
    z = jnp.pad(z, ((0, 0), (0, 40 - HY_EMB)))
    deltas = jnp.abs(jnp.linspace(math.log(1e-2) / 1.5, math.log(1e-2) / 0.3, HY_W, dtype=F32))
    decay = jnp.exp(-t * deltas)
    bwd = jnp.where(jnp.arange(L)[:, None] == 0, 0.0, decay)
    return z, jnp.concatenate([decay, bwd], axis=1)


def _hy_filter_body(z_ref, w1_ref, b1_ref, fr_ref, w2_ref, b2_ref, w3_ref, dec_ref, c_ref, s_ref,
                    kre_ref, kim_ref, h_scr, *, L, fb):
    f = pl.program_id(0)

    @pl.when(f == 0)
    def _():
        fr = fr_ref[...]
        h = jnp.sin(fr * (jnp.dot(z_ref[...], w1_ref[...], precision=HI, preferred_element_type=F32)
                          + b1_ref[...]))
        h = jnp.sin(fr * (jnp.dot(h, w2_ref[...], precision=HI, preferred_element_type=F32) + b2_ref[...]))
        h = jnp.dot(h, w3_ref[...], precision=HI, preferred_element_type=F32) * dec_ref[...]
        h_scr[...] = h.astype(BF16)

    hb = h_scr[...]
    a_re = jnp.dot(c_ref[...], hb, preferred_element_type=F32)
    a_im = jnp.dot(s_ref[...], hb, preferred_element_type=F32)
    grow = f * fb + lax.broadcasted_iota(I32, (fb, 1), 0)
    packed = grow == 0
    sc = jnp.where(packed, 1.0 / (2 * L), 2.0 / (2 * L))
    kre_ref[...] = (a_re[:, :HY_W] + a_re[:, HY_W:]) * sc
    kim_ref[...] = (a_im[:, :HY_W] + jnp.where(packed, 1.0, -1.0) * a_im[:, HY_W:]) * sc


def _hy_filter(L, feats, mats, w1, b1, freq, w2, b2, w3):
    z, dec = feats
    c, s, _ = mats
    fb = min(L, 512)
    full = lambda shape: pl.BlockSpec(shape, lambda f: (0,) * len(shape))
    w1p = jnp.pad(w1, ((0, 40 - HY_EMB), (0, 0)))
    return pl.pallas_call(
        functools.partial(_hy_filter_body, L=L, fb=fb),
        grid=(L // fb,),
        in_specs=[full((L, 40)), full((40, HY_FF)), full((1, HY_FF)), full((1, HY_FF)),
                  full((HY_FF, HY_FF)), full((1, HY_FF)), full((HY_FF, 2 * HY_W)), full((L, 2 * HY_W)),
                  pl.BlockSpec((fb, L), lambda f: (f, 0)), pl.BlockSpec((fb, L), lambda f: (f, 0))],
        out_specs=[pl.BlockSpec((fb, HY_W), lambda f: (f, 0))] * 2,
        out_shape=[jax.ShapeDtypeStruct((L, HY_W), F32)] * 2,
        scratch_shapes=[pltpu.VMEM((L, 2 * HY_W), BF16)],
        compiler_params=_params(("arbitrary",), VMEM_LIMIT),
        name="hy_filter",
    )(z, w1p, b1.reshape(1, -1), freq.reshape(1, -1), w2, b2.reshape(1, -1), w3, dec, c, s)


def _hy_conv_body(v_ref, cr_ref, sr_ref, cc_ref, sc_ref, kre_ref, kim_ref, o_ref, *, fb):
    f = pl.program_id(1)
    vb = v_ref[...]
    vre = jnp.dot(cr_ref[...], vb, preferred_element_type=F32)
    vim = jnp.dot(sr_ref[...], vb, preferred_element_type=F32)
    kre = kre_ref[...]
    kim = kim_ref[...]
    grow = f * fb + lax.broadcasted_iota(I32, (fb, 1), 0)
    packed = grow == 0
    yre = vre * kre - jnp.where(packed, 0.0, vim * kim)
    yim = jnp.where(packed, vim * kim, vre * kim + vim * kre)
    part = (jnp.dot(cc_ref[...], yre.astype(BF16), preferred_element_type=F32)
            + jnp.dot(sc_ref[...], yim.astype(BF16), preferred_element_type=F32))

    @pl.when(f == 0)
    def _():
        o_ref[...] = part

    @pl.when(f > 0)
    def _():
        o_ref[...] += part


def _hy_conv(vgb, mats, kre, kim, L, nseq, row_off):
    c, s, st = mats
    fb = min(L, 256)
    off = row_off // L
    return pl.pallas_call(
        functools.partial(_hy_conv_body, fb=fb),
        grid=(nseq, L // fb),
        in_specs=[pl.BlockSpec((L, HY_W), lambda b, f: (off + b, 0)),
                  pl.BlockSpec((fb, L), lambda b, f: (f, 0)),
                  pl.BlockSpec((fb, L), lambda b, f: (f, 0)),
                  pl.BlockSpec((L, fb), lambda b, f: (0, f)),
                  pl.BlockSpec((L, fb), lambda b, f: (0, f)),
                  pl.BlockSpec((fb, HY_W), lambda b, f: (f, 0)),
                  pl.BlockSpec((fb, HY_W), lambda b, f: (f, 0))],
        out_specs=pl.BlockSpec((L, HY_W), lambda b, f: (b, 0)),
        out_shape=jax.ShapeDtypeStruct((nseq * L, HY_W), F32),
        compiler_params=_params(("parallel", "arbitrary"), VMEM_LIMIT),
        name="hy_conv",
    )(vgb, c, s, c, st, kre, kim)


def _layer_norm(z, g, b):
    mu = jnp.mean(z, axis=-1, keepdims=True)
    zc = z - mu
    var = jnp.mean(zc * zc, axis=-1, keepdims=True)
    return zc * lax.rsqrt(var + LN_EPS) * g + b


def _post1_body(*refs, even):
    is_ctx = pl.program_id(0) < NCT
    if even:
        (x_ref, m_ref, convc_ref, convs_ref, vg_ref, x0_ref, skip_ref, atc_ref, ats_ref, w_ref, b_ref,
         g_ref, be_ref, rw_ref, rb_ref, x1_ref, xm_ref, lg_ref) = refs
    else:
        (x_ref, m_ref, atc_ref, ats_ref, w_ref, b_ref, g_ref, be_ref, rw_ref, rb_ref, x1_ref, xm_ref,
         lg_ref) = refs
    for r0 in range(0, TM, POST_CHAIN):
        rs = slice(r0, r0 + POST_CHAIN)
        at = jnp.where(is_ctx, atc_ref[rs, :], ats_ref[rs, :])
        if even:
            conv = jnp.where(is_ctx, convc_ref[rs, :], convs_ref[rs, :])
            yh = ((conv + vg_ref[rs, :] * skip_ref[...]) * x0_ref[rs, :]).astype(BF16)
            y = (jnp.dot(yh, w_ref[:HY_W, :], preferred_element_type=F32)
                 + jnp.dot(at, w_ref[HY_W:, :], preferred_element_type=F32))
        else:
            y = jnp.dot(at, w_ref[...], preferred_element_type=F32)
        y = y + b_ref[...]
        x1 = _layer_norm(DN_ALPHA * x_ref[rs, :] + m_ref[2:3, :] * y, g_ref[...], be_ref[...])
        x1_ref[rs, :] = x1
        xm = x1 * (1.0 + m_ref[4:5, :]) + m_ref[3:4, :]
        _store_rows(xm_ref, xm, row0=r0)
        lg_ref[:, rs] = lax.dot_general(rw_ref[...], xm, (((1,), (1,)), ((), ())), precision=HI,
                                        preferred_element_type=F32) + rb_ref[...]


def _post1(x, mods, l, parts, w_bf, b, g, be, rw_t, rb, even):
    row = lambda w: pl.BlockSpec((TM, w), lambda i: (i, 0))
    full = lambda shape: pl.BlockSpec(shape, lambda i: (0,) * len(shape))
    in_specs = [row(D), pl.BlockSpec((None, None, 6, D), lambda i: (l, _mod_row(i), 0, 0))]
    ctx = lambda w: pl.BlockSpec((TM, w), lambda i: (jnp.minimum(i, NCT - 1), 0))
    lat = lambda w: pl.BlockSpec((TM, w), lambda i: (jnp.maximum(i - NCT, 0), 0))
    if even:
        in_specs += [ctx(HY_W), lat(HY_W), row(HY_W), row(HY_W), full((1, HY_W)), ctx(NA_W), lat(NA_W)]
    else:
        in_specs += [ctx(D), lat(D)]
    in_specs += [full((D, D)), full((1, D)), full((1, D)), full((1, D)), full((N_EXPERTS, D)),
                 full((N_EXPERTS, 1))]
    return pl.pallas_call(
        functools.partial(_post1_body, even=even),
        grid=(NT,),
        in_specs=in_specs,
        out_specs=[row(D), pl.BlockSpec((TM * SUB, LANE), lambda i: (i, 0)),
                   pl.BlockSpec((None, N_EXPERTS, TM), lambda i: (i, 0, 0))],
        out_shape=[jax.ShapeDtypeStruct((T, D), F32), jax.ShapeDtypeStruct((T * SUB, LANE), F32),
                   jax.ShapeDtypeStruct((NT, N_EXPERTS, TM), F32)],
        compiler_params=_params(("parallel",), VMEM_LIMIT),
        name="post1",
    )(x, mods, *parts, w_bf, b.reshape(1, D), g.reshape(1, D), be.reshape(1, D), rw_t,
      rb.reshape(N_EXPERTS, 1))


def _route_body(lg_ref, slot_ref, gate_ref, pe_ref, pd_ref, be_ref, rb_ref, nx_ref, par_ref, eidx_scr,
                rank_scr):
    eio = lax.broadcasted_iota(I32, (N_EXPERTS, TM), 0)
    ri = lax.broadcasted_iota(I32, (TM, TM), 0)
    ci = lax.broadcasted_iota(I32, (TM, TM), 1)
    upper = jnp.where(ri < ci, 1.0, 0.0).astype(BF16)

    def tile(i, carry):
        lg = lg_ref[i]
        sel = jnp.zeros((N_EXPERTS, TM), F32)
        vals, hots = [], []
        for k in range(TOP_K):
            mx = jnp.max(lg, axis=0, keepdims=True)
            idx = jnp.min(jnp.where(lg == mx, eio, N_EXPERTS), axis=0, keepdims=True)
            hot = eio == idx
            lg = jnp.where(hot, -jnp.inf, lg)
            sel = sel + jnp.where(hot, 1.0, 0.0)
            vals.append(mx)
            hots.append(hot)
            eidx_scr[i, k:k + 1, :] = idx
        rank = jnp.dot(sel.astype(BF16), upper, preferred_element_type=F32) + carry
        ex = [jnp.exp(v - vals[0]) for v in vals]
        den = ex[0] + ex[1] + ex[2] + ex[3]
        for k in range(TOP_K):
            gate_ref[i, k:k + 1, :] = ex[k] / den
            rank_scr[i, k:k + 1, :] = jnp.sum(jnp.where(hots[k], rank, 0.0), axis=0, keepdims=True)
        return carry + jnp.sum(sel, axis=1, keepdims=True)

    counts = lax.fori_loop(0, NT, tile, jnp.zeros((N_EXPERTS, 1), F32))
    padded = jnp.floor((counts + (MOE_BM - 1)) * (1.0 / MOE_BM)) * MOE_BM
    e_r = lax.broadcasted_iota(I32, (N_EXPERTS, N_EXPERTS), 0)
    e_c = lax.broadcasted_iota(I32, (N_EXPERTS, N_EXPERTS), 1)
    incl = jnp.where(e_c <= e_r, 1.0, 0.0)
    padded_b = jnp.broadcast_to(padded, (N_EXPERTS, LANE))
    pad_end = jnp.dot(incl, padded_b, precision=HI, preferred_element_type=F32)
    pad_start = pad_end[:, 0:1] - padded
    pe_ref[...] = pad_end.astype(I32)
    pd_ref[...] = padded_b.astype(I32)
    blk0 = (lax.broadcasted_iota(I32, (1, TM), 1) * MOE_BM).astype(F32)
    nle = jnp.sum(jnp.where(pad_end[:, 0:1] <= blk0, 1.0, 0.0), axis=0, keepdims=True)
    blk_e = jnp.minimum(nle, N_EXPERTS - 1.0)
    be_ref[...] = blk_e.astype(I32)
    mine = eio.astype(F32) == blk_e
    cnt_b = jnp.sum(jnp.where(mine, counts, 0.0), axis=0, keepdims=True)
    start_b = jnp.sum(jnp.where(mine, pad_start, 0.0), axis=0, keepdims=True)
    rb_ref[...] = jnp.clip(cnt_b - (blk0 - start_b), 0.0, float(MOE_BM)).astype(I32)
    used = padded > 0.0
    e_f = eio.astype(F32)
    nx_ref[...] = jnp.min(jnp.where((e_f > blk_e) & used, e_f, float(N_EXPERTS)), axis=0,
                          keepdims=True).astype(I32)
    ordinal = jnp.sum(jnp.where((e_f < blk_e) & used, 1.0, 0.0), axis=0, keepdims=True)
    par_ref[...] = (ordinal - 2.0 * jnp.floor(ordinal * 0.5)).astype(I32)

    def place(i, c):
        for k in range(TOP_K):
            hot = eio == eidx_scr[i, k:k + 1, :]
            start = jnp.sum(jnp.where(hot, pad_start, 0.0), axis=0, keepdims=True)
            slot_ref[i, k:k + 1, :] = (start + rank_scr[i, k:k + 1, :]).astype(I32)
        return c

    lax.fori_loop(0, NT, place, 0)


def _route(logits):
    return pl.pallas_call(
        _route_body,
        out_shape=[jax.ShapeDtypeStruct((NT, TOP_K, TM), I32), jax.ShapeDtypeStruct((NT, TOP_K, TM), F32),
                   jax.ShapeDtypeStruct((N_EXPERTS, LANE), I32), jax.ShapeDtypeStruct((N_EXPERTS, LANE), I32),
                   ] + [jax.ShapeDtypeStruct((1, TM), I32)] * 4,
        scratch_shapes=[pltpu.VMEM((NT, TOP_K, TM), I32), pltpu.VMEM((NT, TOP_K, TM), F32)],
        compiler_params=_params(None, VMEM_LIMIT),
        name="route",
    )(logits)


def _dispatch_body(slot_ref, pe_ref, pd_ref, xm_ref, xb_ref, zero_scr, sem):
    i = pl.program_id(0)

    def fill_block(b):
        start = b * (MOE_BM * SUB)
        if not isinstance(b, int):
            start = pl.multiple_of(start, MOE_BM * SUB)
        return pltpu.make_async_copy(zero_scr, xb_ref.at[pl.ds(start, MOE_BM * SUB)], sem)

    @pl.when(i == 0)
    def _():
        zero_scr[...] = jnp.zeros_like(zero_scr)
        n_used = pe_ref[N_EXPERTS - 1] // MOE_BM
        for start_or_wait in (True, False):
            for e in range(N_EXPERTS):
                @pl.when(pd_ref[e] > 0)
                def _():
                    cp = fill_block(pe_ref[e] // MOE_BM - 1)
                    cp.start() if start_or_wait else cp.wait()
            for b in range(T * TOP_K // MOE_BM, N_SLOT_BLOCKS):
                @pl.when(b >= n_used)
                def _():
                    cp = fill_block(b)
                    cp.start() if start_or_wait else cp.wait()

    def row_copy(r, k):
        s = slot_ref[i * (TOP_K * TM) + k * TM + r]
        return pltpu.make_async_copy(xm_ref.at[pl.ds(pl.multiple_of(r * SUB, SUB), SUB)],
                                     xb_ref.at[pl.ds(pl.multiple_of(s * SUB, SUB), SUB)], sem)

    def issue(r, c):
        for k in range(TOP_K):
            row_copy(r, k).start(priority=k % DMA_PRIORITIES)
        return c

    def drain(r, c):
        for k in range(TOP_K):
            row_copy(r, k).wait()
        return c

    lax.fori_loop(0, TM, issue, 0, unroll=ROW_DMA_UNROLL)
    lax.fori_loop(0, TM, drain, 0, unroll=ROW_DMA_UNROLL)


def _dispatch(slot_flat, pe, pd, xm):
    return pl.pallas_call(
        _dispatch_body,
        grid_spec=pltpu.PrefetchScalarGridSpec(
            num_scalar_prefetch=3, grid=(NT,),
            in_specs=[pl.BlockSpec((TM * SUB, LANE), lambda i, s, a, b: (i, 0))],
            out_specs=pl.BlockSpec(memory_space=pl.ANY),
            scratch_shapes=[pltpu.VMEM((MOE_BM * SUB, LANE), F32), pltpu.SemaphoreType.DMA]),
        out_shape=jax.ShapeDtypeStruct((N_SLOTS * SUB, LANE), F32),
        compiler_params=_params(("arbitrary",), VMEM_LIMIT),
        name="moe_dispatch",
    )(slot_flat, pe, pd, xm)


def _expert_body(be_ref, nu_ref, rb_ref, nx_ref, par_ref, xb_ref, guw_hbm, gub_ref, dnw_hbm, dnb_ref, yb_ref,
                 gu_f32, dn_f32, gu_scr, dn_scr, sem, *, l):
    i = pl.program_id(0)
    rows = rb_ref[i]

    def weight_copies(e, slot):
        return (pltpu.make_async_copy(guw_hbm.at[l, e], gu_f32.at[slot], sem.at[0, slot]),
                pltpu.make_async_copy(dnw_hbm.at[l, e], dn_f32.at[slot], sem.at[1, slot]))

    def chain(r0):
        xb = _load_rows(xb_ref, MOE_CHAIN, row0=r0).astype(BF16)
        hgu = jnp.dot(xb, gu_scr[...], preferred_element_type=F32) + gub_ref[...]
        g = jnp.minimum(hgu[:, :D_FF], SWIGLU_LIMIT)
        lin = jnp.clip(hgu[:, D_FF:], -SWIGLU_LIMIT, SWIGLU_LIMIT)
        act = (lin + 1.0) * (g / (1.0 + jnp.exp(-SWIGLU_ALPHA * g)))
        y = jnp.dot(act.astype(BF16), dn_scr[...], preferred_element_type=F32) + dnb_ref[...]
        _store_rows(yb_ref, y, row0=r0)

    @pl.when(i >= nu_ref[0])
    def _():
        yb_ref[...] = jnp.zeros_like(yb_ref)

    @pl.when(i < nu_ref[0])
    def _():
        e = be_ref[i]
        slot = par_ref[i]

        @pl.when(i == 0)
        def _():
            for cp in weight_copies(e, slot):
                cp.start()

        @pl.when((i == 0) | (e != be_ref[jnp.maximum(i - 1, 0)]))
        def _():
            nxt = nx_ref[i]

            @pl.when(nxt < N_EXPERTS)
            def _():
                for cp in weight_copies(nxt, 1 - slot):
                    cp.start()

            for cp in weight_copies(e, slot):
                cp.wait()
            gu_scr[...] = gu_f32[slot].astype(BF16)
            dn_scr[...] = dn_f32[slot].astype(BF16)

        @pl.when(rows > MOE_CHAIN)
        def _():
            for r0 in range(0, MOE_BM, MOE_CHAIN):
                chain(r0)

        @pl.when(rows <= MOE_CHAIN)
        def _():
            chain(0)
            yb_ref[MOE_CHAIN * SUB:, :] = jnp.zeros((MOE_BM * SUB - MOE_CHAIN * SUB, LANE), F32)


def _experts(block_e, n_used, block_rows, next_e, parity, xb, l, gu_w, gu_b, dn_w, dn_b):
    def blk(i, nu):
        return jnp.minimum(i, nu[0] - 1)

    def bias(width):
        return pl.BlockSpec((None, None, 1, width), lambda i, be, nu, rb, nx, par: (l, be[blk(i, nu)], 0, 0))

    return pl.pallas_call(
        functools.partial(_expert_body, l=l),
        grid_spec=pltpu.PrefetchScalarGridSpec(
            num_scalar_prefetch=5, grid=(N_SLOT_BLOCKS,),
            in_specs=[pl.BlockSpec((MOE_BM * SUB, LANE), lambda i, be, nu, rb, nx, par: (blk(i, nu), 0)),
                      pl.BlockSpec(memory_space=pl.ANY), bias(2 * D_FF),
                      pl.BlockSpec(memory_space=pl.ANY), bias(D)],
            out_specs=pl.BlockSpec((MOE_BM * SUB, LANE), lambda i, be, nu, rb, nx, par: (i, 0)),
            scratch_shapes=[pltpu.VMEM((2, D, 2 * D_FF), F32), pltpu.VMEM((2, D_FF, D), F32),
                            pltpu.VMEM((D, 2 * D_FF), BF16), pltpu.VMEM((D_FF, D), BF16),
                            pltpu.SemaphoreType.DMA((2, 2))]),
        out_shape=jax.ShapeDtypeStruct((N_SLOTS * SUB, LANE), F32),
        compiler_params=_params(("arbitrary",), VMEM_LIMIT),
        name="moe_experts",
    )(block_e, n_used, block_rows, next_e, parity, xb, gu_w, gu_b.reshape(DEPTH, N_EXPERTS, 1, 2 * D_FF), dn_w,
      dn_b.reshape(DEPTH, N_EXPERTS, 1, D))


def _post2_body(slot_ref, x1_ref, gt_ref, m_ref, g_ref, be_ref, yb_ref, x2_ref, y_scr, sem):
    i = pl.program_id(0)
    buf = i % 2

    def row_copy(t, r, k):
        s = slot_ref[t * (TOP_K * TM) + k * TM + r]
        return pltpu.make_async_copy(yb_ref.at[pl.ds(pl.multiple_of(s * SUB, SUB), SUB)],
                                     y_scr.at[t % 2, k, pl.ds(pl.multiple_of(r * SUB, SUB), SUB)],
                                     sem.at[t % 2])

    def fetch(t):
        def issue(r, c):
            for k in range(TOP_K):
                row_copy(t, r, k).start(priority=k % DMA_PRIORITIES)
            return c
        lax.fori_loop(0, TM, issue, 0, unroll=ROW_DMA_UNROLL)

    @pl.when(i == 0)
    def _():
        fetch(i)

    @pl.when(i + 1 < NT)
    def _():
        fetch(i + 1)

    def drain(r, c):
        for k in range(TOP_K):
            row_copy(i, r, k).wait()
        return c

    lax.fori_loop(0, TM, drain, 0, unroll=ROW_DMA_UNROLL)
    f = _load_rows(y_scr, TM, (buf, 0)) * gt_ref[:, 0:1]
    for k in range(1, TOP_K):
        f = f + _load_rows(y_scr, TM, (buf, k)) * gt_ref[:, k:k + 1]
    x2_ref[...] = _layer_norm(DN_ALPHA * x1_ref[...] + m_ref[5:6, :] * f, g_ref[...], be_ref[...])


def _post2(slot_flat, x1, gates_t, mods, l, g, be, yb):
    full = lambda shape: pl.BlockSpec(shape, lambda i, s: (0,) * len(shape))
    return pl.pallas_call(
        _post2_body,
        grid_spec=pltpu.PrefetchScalarGridSpec(
            num_scalar_prefetch=1, grid=(NT,),
            in_specs=[pl.BlockSpec((TM, D), lambda i, s: (i, 0)),
                      pl.BlockSpec((TM, TOP_K), lambda i, s: (i, 0)),
                      pl.BlockSpec((None, None, 6, D), lambda i, s: (l, _mod_row(i), 0, 0)),
                      full((1, D)), full((1, D)),
                      pl.BlockSpec(memory_space=pl.ANY)],
            out_specs=pl.BlockSpec((TM, D), lambda i, s: (i, 0)),
            scratch_shapes=[pltpu.VMEM((2, TOP_K, TM * SUB, LANE), F32), pltpu.SemaphoreType.DMA((2,))]),
        out_shape=jax.ShapeDtypeStruct((T, D), F32),
        compiler_params=_params(("arbitrary",), VMEM_LIMIT),
        name="post2",
    )(slot_flat, x1, gates_t, mods, g.reshape(1, D), be.reshape(1, D), yb)


def _rope_tables():
    quarter = HEAD // 4
    pos = jnp.arange(DEC_SEQ)
    rows = (pos // GRID_W).astype(F32)[:, None]
    cols = (pos % GRID_W).astype(F32)[:, None]
    lane = jnp.arange(LANE)[None, :]
    d = lane % HEAD
    inv = ROPE_THETA ** (-(d % quarter).astype(F32) / quarter)
    ang = jnp.where(d < HEAD // 2, rows, cols) * inv
    sign = jnp.where((d // quarter) % 2 == 0, -1.0, 1.0)
    cos = jnp.concatenate([jnp.ones((TC, LANE), F32), jnp.tile(jnp.cos(ang), (DEC_BATCH, 1))], axis=0)
    sin = jnp.concatenate([jnp.zeros((TC, LANE), F32), jnp.tile(sign * jnp.sin(ang), (DEC_BATCH, 1))], axis=0)
    return cos, sin


def _moe_and_norm(x1, xm, logits, mods, l, ln_g, ln_b, exp_gu_w, exp_gu_b, exp_dn_w, exp_dn_b):
    slot, gates, pe, pd, be, rb, nx, par = _route(logits)
    slot_flat = slot.reshape(-1)
    pe1 = pe[:, 0]
    xb = _dispatch(slot_flat, pe1, pd[:, 0], xm)
    n_used = pe1[N_EXPERTS - 1:] // MOE_BM
    yb = _experts(be.reshape(-1), n_used, rb.reshape(-1), nx.reshape(-1), par.reshape(-1), xb, l,
                  exp_gu_w, exp_gu_b, exp_dn_w, exp_dn_b)
    gates_t = gates.transpose(0, 2, 1).reshape(T, TOP_K)
    return _post2(slot_flat, x1, gates_t, mods, l, ln_g[l, 1], ln_b[l, 1], yb)


def kernel(x_prompt, x_sample, cache_na_k, cache_na_v, cache_swa_k, cache_swa_v, c, c_ctx, ada_w, ada_b, ln_g, ln_b, ev_in_w, ev_in_b, hy_conv_w, hy_conv_b, hy_f_w1, hy_f_b1, hy_f_freq, hy_f_w2, hy_f_b2, hy_f_w3, hy_skip, na_rpb, od_in_w, od_in_b, swa_sink, mix_out_w, mix_out_b, router_w, router_b, exp_gu_w, exp_gu_b, exp_dn_w, exp_dn_b):
    x = jnp.concatenate([x_prompt.reshape(TC, D), x_sample.reshape(TS, D)], axis=0)
    cond8 = jnp.concatenate([c_ctx[None, :], c, jnp.zeros((8 - 1 - DEC_BATCH, D), F32)], axis=0)
    mods = _adaln(cond8, ada_w, ada_b).reshape(DEPTH, 8, 6, D)

    rope_tabs = _rope_tables()
    hy_consts = {L: (_hy_features(L), _dft_mats(L)) for L in (SEQ, DEC_SEQ)}
    ck_na = cache_na_k.reshape(DEC_BATCH, -1, PAST, NA_W)
    cv_na = cache_na_v.reshape(DEC_BATCH, -1, PAST, NA_W)
    ck_swa = cache_swa_k.reshape(DEC_BATCH, -1, PAST, SWA_KV * HEAD)
    cv_swa = cache_swa_v.reshape(DEC_BATCH, -1, PAST, SWA_KV * HEAD)

    na_k, na_v, swa_k, swa_v = [], [], [], []
    for l in range(DEPTH):
        w_out = mix_out_w[l].astype(BF16)
        rw_t = router_w[l].T
        if l % 2 == 0:
            e = l // 2
            x0, vg, vgb, q, k, v = _inproj_even(x, mods, l, ev_in_w[e].astype(BF16), ev_in_b[e],
                                                hy_conv_w[e].reshape(3, HY_PROJ), hy_conv_b[e].reshape(1, HY_PROJ))
            convs = []
            for L, nseq, off in ((SEQ, BATCH, 0), (DEC_SEQ, DEC_BATCH, TC)):
                feats, mats = hy_consts[L]
                kre, kim = _hy_filter(L, feats, mats, hy_f_w1[e], hy_f_b1[e], hy_f_freq[e], hy_f_w2[e],
                                      hy_f_b2[e], hy_f_w3[e])
                convs.append(_hy_conv(vgb, mats, kre, kim, L, nseq, off))
            parts = (convs[0], convs[1], vg, x0, hy_skip[e].reshape(1, HY_W), _ctx_even_attn(q, k, v),
                     _na_attn(q, k, v, ck_na, cv_na, _na_biasmask(na_rpb[e]), e))
            na_k.append(k[:TC].reshape(BATCH, SEQ, NA_HEADS, HEAD))
            na_v.append(v[:TC].reshape(BATCH, SEQ, NA_HEADS, HEAD))
        else:
            o = l // 2
            q, k, v = _inproj(x, mods, l, od_in_w[o].astype(BF16), od_in_b[o],
                              ((SWA_HEADS * HEAD, True, BF16), (SWA_KV * HEAD, True, F32),
                               (SWA_KV * HEAD, False, F32)),
                              rope_tabs)
            parts = (_ctx_odd_attn(swa_sink[o], q, k, v), _swa_attn(swa_sink[o], q, k, v, ck_swa, cv_swa, o))
            swa_k.append(k[:TC].reshape(BATCH, SEQ, SWA_KV, HEAD))
            swa_v.append(v[:TC].reshape(BATCH, SEQ, SWA_KV, HEAD))
        x1, xm, logits = _post1(x, mods, l, parts, w_out, mix_out_b[l], ln_g[l, 0], ln_b[l, 0], rw_t,
                                router_b[l], l % 2 == 0)
        x = _moe_and_norm(x1, xm, logits, mods, l, ln_g, ln_b, exp_gu_w, exp_gu_b, exp_dn_w, exp_dn_b)

    return (x[:TC].reshape(BATCH, SEQ, D), x[TC:].reshape(DEC_BATCH, DEC_SEQ, D),
            jnp.stack(na_k, axis=1), jnp.stack(na_v, axis=1), jnp.stack(swa_k, axis=1), jnp.stack(swa_v, axis=1))
```

```python
import functools
import math

import jax
import jax.numpy as jnp
from jax import lax
from jax.experimental import pallas as pl
from jax.experimental.pallas import tpu as pltpu

F32 = jnp.float32
BF16 = jnp.bfloat16
I32 = jnp.int32
HI = lax.Precision.HIGHEST

D = 1024
DEPTH = 4
BATCH, SEQ = 16, 256
DEC_BATCH, DEC_SEQ = 4, 2048
PAST = 256
GRID_W = 64
HEAD = 64
HY_W = 512
HY_PROJ = 3 * HY_W
HY_BANDS = 16
HY_EMB = 1 + 2 * HY_BANDS
HY_FF = 64
NA_HEADS = 8
NA_W = NA_HEADS * HEAD
NA_KH, NA_KW = 8, 16
SWA_HEADS, SWA_KV = 16, 4
SWA_WINDOW = 128
N_EXPERTS, TOP_K = 32, 4
D_FF = 1024
SWIGLU_LIMIT = 7.0
SWIGLU_ALPHA = 1.702
DN_ALPHA = (2 * DEPTH) ** 0.25
LN_EPS = 1e-5
NEG = -1e30
LOG2E = 1.0 / math.log(2.0)
ROPE_THETA = 10000.0

TC = BATCH * SEQ
TS = DEC_BATCH * DEC_SEQ
T = TC + TS
TM = 256
NT = T // TM
NCT = TC // TM
TPS = DEC_SEQ // TM
LANE = 128
MOE_BM = 512
MOE_CHAIN = 256
N_SLOT_BLOCKS = T * TOP_K // MOE_BM + N_EXPERTS
N_SLOTS = N_SLOT_BLOCKS * MOE_BM
VMEM_LIMIT = 56 * 1024 * 1024
DMA_PRIORITIES = 2
ROW_DMA_UNROLL = 8
POST_CHAIN = 128


def _params(sem, vmem=None):
    return pltpu.CompilerParams(dimension_semantics=sem, vmem_limit_bytes=vmem)


def _mod_row(i):
    return jnp.where(i < NCT, 0, 1 + (i - NCT) // TPS)


SUB = 8
ROW_TILES = D // LANE


def _load_rows(ref, nrows, lead=(), row0=0):
    parts = [ref[lead + (pl.ds(row0 * ROW_TILES + s, nrows, stride=ROW_TILES), slice(None))]
             for s in range(ROW_TILES)]
    return jnp.concatenate(parts, axis=1)


def _store_rows(ref, val, row0=0):
    nrows = val.shape[0]
    for s in range(ROW_TILES):
        ref[pl.ds(row0 * ROW_TILES + s, nrows, stride=ROW_TILES), :] = val[:, s * LANE:(s + 1) * LANE]


def _adaln_body(c_ref, w_ref, b_ref, o_ref):
    c = c_ref[...]
    s = c / (1.0 + jnp.exp(-c))
    o_ref[...] = jnp.dot(s, w_ref[...], precision=HI, preferred_element_type=F32) + b_ref[...]


def _adaln(cond8, ada_w, ada_b):
    nt = 3 * D
    return pl.pallas_call(
        _adaln_body,
        grid=(DEPTH, 6 * D // nt),
        in_specs=[pl.BlockSpec((8, D), lambda l, j: (0, 0)),
                  pl.BlockSpec((None, D, nt), lambda l, j: (l, 0, j)),
                  pl.BlockSpec((None, 1, nt), lambda l, j: (l, 0, j))],
        out_specs=pl.BlockSpec((None, 8, nt), lambda l, j: (l, 0, j)),
        out_shape=jax.ShapeDtypeStruct((DEPTH, 8, 6 * D), F32),
        compiler_params=_params(("arbitrary", "arbitrary"), VMEM_LIMIT),
        name="adaln",
    )(cond8, ada_w, ada_b.reshape(DEPTH, 1, 6 * D))


def _rope(x, cos, sin):
    lane = lax.broadcasted_iota(I32, (1, LANE), 1)
    first = (lane & 16) == 0
    outs = []
    for j in range(x.shape[1] // LANE):
        blk = x[:, j * LANE:(j + 1) * LANE]
        partner = jnp.where(first, pltpu.roll(blk, LANE - 16, 1), pltpu.roll(blk, 16, 1))
        outs.append(blk * cos + partner * sin)
    return jnp.concatenate(outs, axis=1) if len(outs) > 1 else outs[0]


def _inproj_body(*refs, splits, rope):
    if rope:
        x_ref, m_ref, w_ref, b_ref, cos_ref, sin_ref = refs[:6]
        outs = refs[6:]
    else:
        x_ref, m_ref, w_ref, b_ref = refs[:4]
        outs = refs[4:]
    h = x_ref[...] * (1.0 + m_ref[1:2, :]) + m_ref[0:1, :]
    y = jnp.dot(h.astype(BF16), w_ref[...], preferred_element_type=F32) + b_ref[...]
    off = 0
    for o_ref, (width, do_rope, _) in zip(outs, splits):
        part = y[:, off:off + width]
        if do_rope:
            part = _rope(part, cos_ref[...], sin_ref[...])
        o_ref[...] = part.astype(o_ref.dtype)
        off += width


def _inproj(x, mods, l, w_bf, b, splits, rope_tabs=None):
    p = w_bf.shape[1]
    in_specs = [pl.BlockSpec((TM, D), lambda i: (i, 0)),
                pl.BlockSpec((None, None, 6, D), lambda i: (l, _mod_row(i), 0, 0)),
                pl.BlockSpec((D, p), lambda i: (0, 0)),
                pl.BlockSpec((1, p), lambda i: (0, 0))]
    args = [x, mods, w_bf, b.reshape(1, p)]
    if rope_tabs is not None:
        in_specs += [pl.BlockSpec((TM, LANE), lambda i: (i, 0))] * 2
        args += list(rope_tabs)
    return pl.pallas_call(
        functools.partial(_inproj_body, splits=splits, rope=rope_tabs is not None),
        grid=(NT,),
        in_specs=in_specs,
        out_specs=[pl.BlockSpec((TM, w), lambda i: (i, 0)) for w, _, _ in splits],
        out_shape=[jax.ShapeDtypeStruct((T, w), dt) for w, _, dt in splits],
        compiler_params=_params(("parallel",), VMEM_LIMIT),
        name="inproj",
    )(*args)


HALO = SUB


def _inproj_even_body(xp_ref, x_ref, xn_ref, m_ref, w_ref, b_ref, cw_ref, cb_ref,
                      x0_ref, vg_ref, vgb_ref, q_ref, k_ref, v_ref):
    i = pl.program_id(0)
    ti = (i - NCT) % TPS
    first = (i < NCT) | (ti == 0)
    last = (i < NCT) | (ti == TPS - 1)
    xs = jnp.concatenate([xp_ref[...], x_ref[...], xn_ref[...]], axis=0)
    h = xs * (1.0 + m_ref[1:2, :]) + m_ref[0:1, :]
    y = jnp.dot(h.astype(BF16), w_ref[...], preferred_element_type=F32) + b_ref[...]
    u = y[:, :HY_PROJ]
    row = lax.broadcasted_iota(I32, (TM, 1), 0)
    before = jnp.where((row == 0) & first, 0.0, u[HALO - 1:HALO - 1 + TM])
    after = jnp.where((row == TM - 1) & last, 0.0, u[HALO + 1:HALO + 1 + TM])
    uc = cw_ref[0:1, :] * before + cw_ref[1:2, :] * u[HALO:HALO + TM] + cw_ref[2:3, :] * after + cb_ref[...]
    x0_ref[...] = uc[:, :HY_W]
    vg = uc[:, 2 * HY_W:] * uc[:, HY_W:2 * HY_W]
    vg_ref[...] = vg
    vgb_ref[...] = vg.astype(BF16)
    qkv = y[HALO:HALO + TM, HY_PROJ:]
    q_ref[...] = qkv[:, :NA_W].astype(q_ref.dtype)
    k_ref[...] = qkv[:, NA_W:2 * NA_W]
    v_ref[...] = qkv[:, 2 * NA_W:]


def _inproj_even(x, mods, l, w_bf, b, conv_w, conv_b):
    p = w_bf.shape[1]
    nb = TM // HALO
    full = lambda shape: pl.BlockSpec(shape, lambda i: (0,) * len(shape))
    row = lambda w: pl.BlockSpec((TM, w), lambda i: (i, 0))
    return pl.pallas_call(
        _inproj_even_body,
        grid=(NT,),
        in_specs=[pl.BlockSpec((HALO, D), lambda i: (jnp.maximum(i * nb - 1, 0), 0)),
                  row(D),
                  pl.BlockSpec((HALO, D), lambda i: (jnp.minimum(i * nb + nb, T // HALO - 1), 0)),
                  pl.BlockSpec((None, None, 6, D), lambda i: (l, _mod_row(i), 0, 0)),
                  full((D, p)), full((1, p)), full((3, HY_PROJ)), full((1, HY_PROJ))],
        out_specs=[row(HY_W)] * 3 + [row(NA_W)] * 3,
        out_shape=[jax.ShapeDtypeStruct((T, HY_W), F32), jax.ShapeDtypeStruct((T, HY_W), F32),
                   jax.ShapeDtypeStruct((T, HY_W), BF16), jax.ShapeDtypeStruct((T, NA_W), BF16),
                   jax.ShapeDtypeStruct((T, NA_W), F32), jax.ShapeDtypeStruct((T, NA_W), F32)],
        compiler_params=_params(("parallel",), VMEM_LIMIT),
        name="inproj_even",
    )(x, x, x, mods, w_bf, b.reshape(1, p), conv_w, conv_b)


def _lo_mask():
    return lax.broadcasted_iota(I32, (1, LANE), 1) < HEAD


def _dup_half(x2, upper):
    swapped = pltpu.roll(x2, HEAD, 1)
    keep = jnp.logical_xor(_lo_mask(), upper)
    return jnp.where(keep, x2, swapped)


def _pair_attention(q2, segs, sink_lo=None, sink_hi=None):
    m_rows = q2.shape[0]
    lo = _lo_mask()
    q2 = q2.astype(F32) * (HEAD ** -0.5 * LOG2E)
    qs = jnp.concatenate([jnp.where(lo, q2, 0.0), jnp.where(lo, 0.0, q2)], axis=0).astype(BF16)
    scores = []
    for kd, _, bias in segs:
        s = lax.dot_general(qs, kd, (((1,), (1,)), ((), ())), preferred_element_type=F32)
        if bias is not None:
            s = s + bias
        scores.append(s)
    mx = functools.reduce(jnp.maximum, [jnp.max(s, axis=-1, keepdims=True) for s in scores])
    sink = None
    if sink_lo is not None:
        row = lax.broadcasted_iota(I32, (2 * m_rows, 1), 0)
        sink = jnp.where(row < m_rows, sink_lo, sink_hi) * LOG2E
        mx = jnp.maximum(mx, sink)
    den = jnp.zeros_like(mx)
    acc = jnp.zeros((2 * m_rows, LANE), F32)
    for s, (_, vd, _) in zip(scores, segs):
        p = jnp.exp2(s - mx)
        den = den + jnp.sum(p, axis=-1, keepdims=True)
        acc = acc + jnp.dot(p.astype(BF16), vd, preferred_element_type=F32)
    if sink is not None:
        den = den + jnp.exp2(sink - mx)
    o = acc / den
    return jnp.where(lo, o[:m_rows], o[m_rows:])


def _lanes(j, width=LANE):
    return slice(j * width, (j + 1) * width)


def _ctx_even_body(q_ref, k_ref, v_ref, o_ref):
    for j in range(NA_W // LANE):
        segs = [(k_ref[:, _lanes(j)].astype(BF16), v_ref[:, _lanes(j)].astype(BF16), None)]
        o_ref[:, _lanes(j)] = _pair_attention(q_ref[:, _lanes(j)], segs).astype(o_ref.dtype)


def _ctx_even_attn(q, k, v):
    spec = pl.BlockSpec((SEQ, NA_W), lambda b: (b, 0))
    return pl.pallas_call(
        _ctx_even_body,
        grid=(BATCH,),
        in_specs=[spec, spec, spec],
        out_specs=spec,
        out_shape=jax.ShapeDtypeStruct((TC, NA_W), BF16),
        compiler_params=_params(("parallel",)),
        name="ctx_attn_even",
    )(q, k, v)


def _ctx_odd_body(sink_ref, q_ref, k_ref, v_ref, o_ref):
    for g in range(SWA_KV):
        upper = (g % 2) == 1
        kd = _dup_half(k_ref[:, _lanes(g // 2)], upper).astype(BF16)
        vd = _dup_half(v_ref[:, _lanes(g // 2)], upper).astype(BF16)
        for j in (2 * g, 2 * g + 1):
            o_ref[:, _lanes(j)] = _pair_attention(q_ref[:, _lanes(j)], [(kd, vd, None)],
                                                  sink_ref[2 * j], sink_ref[2 * j + 1]).astype(o_ref.dtype)


def _ctx_odd_attn(sink, q, k, v):
    qspec = pl.BlockSpec((SEQ, SWA_HEADS * HEAD), lambda b, s: (b, 0))
    kspec = pl.BlockSpec((SEQ, SWA_KV * HEAD), lambda b, s: (b, 0))
    return pl.pallas_call(
        _ctx_odd_body,
        grid_spec=pltpu.PrefetchScalarGridSpec(
            num_scalar_prefetch=1, grid=(BATCH,),
            in_specs=[qspec, kspec, kspec], out_specs=qspec),
        out_shape=jax.ShapeDtypeStruct((TC, SWA_HEADS * HEAD), BF16),
        compiler_params=_params(("parallel",)),
        name="ctx_attn_odd",
    )(sink, q, k, v)


SWA_BLK = 128
SWA_NB = DEC_SEQ // SWA_BLK


def _swa_body(sink_ref, q_ref, kp_ref, kc_ref, kn_ref, vp_ref, vc_ref, vn_ref, ck_ref, cv_ref, o_ref):
    n = pl.program_id(1)
    qi = lax.broadcasted_iota(I32, (SWA_BLK, 3 * SWA_BLK), 0)
    kj = lax.broadcasted_iota(I32, (SWA_BLK, 3 * SWA_BLK), 1)
    kpos = (n - 1) * SWA_BLK + kj
    rel = kj - SWA_BLK - qi
    ok = (jnp.abs(rel) <= SWA_WINDOW) & (kpos >= 0) & (kpos < DEC_SEQ)
    mask = jnp.where(ok, 0.0, NEG)
    mask2 = jnp.concatenate([mask, mask], axis=0)
    for g in range(SWA_KV):
        upper = (g % 2) == 1
        kv = _lanes(g // 2)
        kloc = jnp.concatenate([kp_ref[:, kv], kc_ref[:, kv], kn_ref[:, kv]], axis=0)
        vloc = jnp.concatenate([vp_ref[:, kv], vc_ref[:, kv], vn_ref[:, kv]], axis=0)
        segs = [(_dup_half(kloc, upper).astype(BF16), _dup_half(vloc, upper).astype(BF16), mask2),
                (_dup_half(ck_ref[:, kv], upper).astype(BF16), _dup_half(cv_ref[:, kv], upper).astype(BF16),
                 None)]
        for j in (2 * g, 2 * g + 1):
            o_ref[:, _lanes(j)] = _pair_attention(q_ref[:, _lanes(j)], segs, sink_ref[2 * j],
                                                  sink_ref[2 * j + 1]).astype(o_ref.dtype)


def _swa_attn(sink, q, k, v, ck, cv, o):
    base = TC // SWA_BLK
    qspec = pl.BlockSpec((SWA_BLK, SWA_HEADS * HEAD), lambda b, n, s: (base + b * SWA_NB + n, 0))

    def kv(shift):
        return pl.BlockSpec(
            (SWA_BLK, SWA_KV * HEAD),
            lambda b, n, s: (base + b * SWA_NB + jnp.clip(n + shift, 0, SWA_NB - 1), 0))

    cspec = pl.BlockSpec((None, None, PAST, SWA_KV * HEAD), lambda b, n, s: (b, o, 0, 0))
    ospec = pl.BlockSpec((SWA_BLK, SWA_HEADS * HEAD), lambda b, n, s: (b * SWA_NB + n, 0))
    return pl.pallas_call(
        _swa_body,
        grid_spec=pltpu.PrefetchScalarGridSpec(
            num_scalar_prefetch=1, grid=(DEC_BATCH, SWA_NB),
            in_specs=[qspec, kv(-1), kv(0), kv(1), kv(-1), kv(0), kv(1), cspec, cspec],
            out_specs=ospec),
        out_shape=jax.ShapeDtypeStruct((TS, SWA_HEADS * HEAD), BF16),
        compiler_params=_params(("parallel", "parallel")),
        name="swa_attn",
    )(sink, q, k, k, k, v, v, v, ck, cv)


NA_QROWS = 4
NA_TILES = DEC_SEQ // TM
NA_KTILES = 3
NA_KSTART_MAX = NA_TILES - NA_KTILES


def _na_body(q_ref, k0_ref, k1_ref, k2_ref, v0_ref, v1_ref, v2_ref, ck_ref, cv_ref, bias_ref, o_ref):
    for j in range(NA_W // LANE):
        ln = _lanes(j)
        kloc = jnp.concatenate([k0_ref[:, ln], k1_ref[:, ln], k2_ref[:, ln]], axis=0).astype(BF16)
        vloc = jnp.concatenate([v0_ref[:, ln], v1_ref[:, ln], v2_ref[:, ln]], axis=0).astype(BF16)
        bias = jnp.concatenate([bias_ref[2 * j], bias_ref[2 * j + 1]], axis=0)
        segs = [(kloc, vloc, bias), (ck_ref[:, ln].astype(BF16), cv_ref[:, ln].astype(BF16), None)]
        o_ref[:, ln] = _pair_attention(q_ref[:, ln], segs).astype(o_ref.dtype)


def _na_attn(q, k, v, ck, cv, biasmask, e):
    base = NCT

    def kstart(i):
        return jnp.clip(i - 1, 0, NA_KSTART_MAX)

    qspec = pl.BlockSpec((TM, NA_W), lambda i, b: (base + b * NA_TILES + i, 0))

    def kv(t):
        return pl.BlockSpec((TM, NA_W), lambda i, b: (base + b * NA_TILES + kstart(i) + t, 0))

    cspec = pl.BlockSpec((None, None, PAST, NA_W), lambda i, b: (b, e, 0, 0))
    bspec = pl.BlockSpec((None, NA_HEADS, TM, NA_KTILES * TM),
                         lambda i, b: (jnp.where(i == 0, 0, jnp.where(i == NA_TILES - 1, 2, 1)), 0, 0, 0))
    ospec = pl.BlockSpec((TM, NA_W), lambda i, b: (b * NA_TILES + i, 0))
    return pl.pallas_call(
        _na_body,
        grid=(NA_TILES, DEC_BATCH),
        in_specs=[qspec, kv(0), kv(1), kv(2), kv(0), kv(1), kv(2), cspec, cspec, bspec],
        out_specs=ospec,
        out_shape=jax.ShapeDtypeStruct((TS, NA_W), BF16),
        compiler_params=_params(("parallel", "parallel"), VMEM_LIMIT),
        name="na_attn",
    )(q, k, k, k, v, v, v, ck, cv, biasmask)


NA_REL_ROWS = 2 * NA_KH - 1
NA_REL_COLS = 2 * NA_KW - 1


def _na_bias_body(rpb_ref, o_ref):
    h = pl.program_id(0)
    rows = DEC_SEQ // GRID_W
    c = lax.broadcasted_iota(I32, (GRID_W, LANE), 0)
    lane = lax.broadcasted_iota(I32, (GRID_W, LANE), 1)
    c2 = lane % GRID_W
    rel = c2 - c + (NA_KW - 1)
    win_lo = jnp.clip(c - NA_KW // 2, 0, GRID_W - NA_KW)
    col_ok = (c2 >= win_lo) & (c2 < win_lo + NA_KW)
    hits = [rel == j for j in range(NA_REL_COLS)]
    outside = jnp.full((GRID_W, LANE), NEG, F32)
    toep = []
    for dr in range(NA_REL_ROWS):
        base = (h * NA_REL_ROWS + dr) * NA_REL_COLS
        t = outside
        for j in range(NA_REL_COLS):
            t = jnp.where(hits[j], rpb_ref[base + j] * LOG2E, t)
        toep.append(jnp.where(col_ok, t, NEG))
    first_half = lane < GRID_W
    for pi, i in enumerate((0, 1, NA_TILES - 1)):
        key_row0 = NA_QROWS * min(max(i - 1, 0), NA_KSTART_MAX)
        for a in range(NA_QROWS):
            r = NA_QROWS * i + a
            rs = min(max(r - NA_KH // 2, 0), rows - NA_KH)

            def block(kr):
                r2 = key_row0 + kr
                return toep[r2 - r + NA_KH - 1] if rs <= r2 < rs + NA_KH else outside

            for s in range(NA_KTILES * NA_QROWS // 2):
                o_ref[pi, a * GRID_W:(a + 1) * GRID_W, s * LANE:(s + 1) * LANE] = jnp.where(
                    first_half, block(2 * s), block(2 * s + 1))


def _na_biasmask(rpb):
    return pl.pallas_call(
        _na_bias_body,
        grid_spec=pltpu.PrefetchScalarGridSpec(
            num_scalar_prefetch=1, grid=(NA_HEADS,), in_specs=[],
            out_specs=pl.BlockSpec((3, None, TM, NA_KTILES * TM), lambda h, s: (0, h, 0, 0))),
        out_shape=jax.ShapeDtypeStruct((3, NA_HEADS, TM, NA_KTILES * TM), F32),
        compiler_params=_params(("parallel",), VMEM_LIMIT),
        name="na_bias",
    )(rpb.reshape(-1))


DFT_ROWS = 256


def _dft_body(ac_ref, as_ref, bc_ref, bs_ref, c_ref, s_ref, st_ref, *, rb):
    h = pl.program_id(0)
    ac, sa, bc, sb = ac_ref[...], as_ref[...], bc_ref[...], bs_ref[...]
    cosv = ac * bc - sa * sb
    nsin = -(sa * bc + ac * sb)
    row = h * rb + lax.broadcasted_iota(I32, (rb, 1), 0)
    col = lax.broadcasted_iota(I32, (1, cosv.shape[1]), 1)
    c_ref[...] = cosv.astype(BF16)
    s_ref[...] = jnp.where(row == 0, jnp.where(col % 2 == 0, 1.0, -1.0), nsin).astype(BF16)
    st_ref[...] = jnp.where(col == 0, jnp.where(row % 2 == 0, 1.0, -1.0), nsin).astype(BF16)


def _dft_mats(L):
    n = 2 * L
    rb = min(L, DFT_ROWS)
    t = jnp.arange(L, dtype=I32)[None, :]
    hi = jnp.arange(L // rb, dtype=I32)[:, None]
    lo = jnp.arange(rb, dtype=I32)[:, None]
    pa = ((rb * hi * t) % n).astype(F32) * (2.0 * math.pi / n)
    pb = ((lo * t) % n).astype(F32) * (2.0 * math.pi / n)
    coarse = lambda x: x.reshape(L // rb, 1, L)
    full = lambda shape: pl.BlockSpec(shape, lambda h: (0,) * len(shape))
    return pl.pallas_call(
        functools.partial(_dft_body, rb=rb),
        grid=(L // rb,),
        in_specs=[pl.BlockSpec((None, 1, L), lambda h: (h, 0, 0))] * 2 + [full((rb, L))] * 2,
        out_specs=[pl.BlockSpec((rb, L), lambda h: (h, 0))] * 3,
        out_shape=[jax.ShapeDtypeStruct((L, L), BF16)] * 3,
        compiler_params=_params(("parallel",), VMEM_LIMIT),
        name="dft_tables",
    )(coarse(jnp.cos(pa)), coarse(jnp.sin(pa)), jnp.cos(pb), jnp.sin(pb))


def _hy_features(L):
    nn = jnp.arange(L, dtype=F32)[:, None]
    t = jnp.linspace(0.0, 1.0, L, dtype=F32)[:, None]
    bands = jnp.linspace(1e-4, HY_BANDS - 1, HY_BANDS, dtype=F32)[None, :]
    ang = (2.0 * math.pi / L) * nn * bands
    z = jnp.concatenate([t, jnp.cos(ang), -jnp.sin(ang)], axis=-1)
    z = jnp.pad(z, ((0, 0), (0, 40 - HY_EMB)))
    deltas = jnp.abs(jnp.linspace(math.log(1e-2) / 1.5, math.log(1e-2) / 0.3, HY_W, dtype=F32))
    decay = jnp.exp(-t * deltas)
    bwd = jnp.where(jnp.arange(L)[:, None] == 0, 0.0, decay)
    return z, jnp.concatenate([decay, bwd], axis=1)


def _hy_filter_body(z_ref, w1_ref, b1_ref, fr_ref, w2_ref, b2_ref, w3_ref, dec_ref, c_ref, s_ref,
                    kre_ref, kim_ref, h_scr, *, L, fb):
    f = pl.program_id(0)

    @pl.when(f == 0)
    def _():
        fr = fr_ref[...]
        h = jnp.sin(fr * (jnp.dot(z_ref[...], w1_ref[...], precision=HI, preferred_element_type=F32)
                          + b1_ref[...]))
        h = jnp.sin(fr * (jnp.dot(h, w2_ref[...], precision=HI, preferred_element_type=F32) + b2_ref[...]))
        h = jnp.dot(h, w3_ref[...], precision=HI, preferred_element_type=F32) * dec_ref[...]
        h_scr[...] = h.astype(BF16)

    hb = h_scr[...]
    a_re = jnp.dot(c_ref[...], hb, preferred_element_type=F32)
    a_im = jnp.dot(s_ref[...], hb, preferred_element_type=F32)
    grow = f * fb + lax.broadcasted_iota(I32, (fb, 1), 0)
    packed = grow == 0
    sc = jnp.where(packed, 1.0 / (2 * L), 2.0 / (2 * L))
    kre_ref[...] = (a_re[:, :HY_W] + a_re[:, HY_W:]) * sc
    kim_ref[...] = (a_im[:, :HY_W] + jnp.where(packed, 1.0, -1.0) * a_im[:, HY_W:]) * sc


def _hy_filter(L, feats, mats, w1, b1, freq, w2, b2, w3):
    z, dec = feats
    c, s, _ = mats
    fb = min(L, 512)
    full = lambda shape: pl.BlockSpec(shape, lambda f: (0,) * len(shape))
    w1p = jnp.pad(w1, ((0, 40 - HY_EMB), (0, 0)))
    return pl.pallas_call(
        functools.partial(_hy_filter_body, L=L, fb=fb),
        grid=(L // fb,),
        in_specs=[full((L, 40)), full((40, HY_FF)), full((1, HY_FF)), full((1, HY_FF)),
                  full((HY_FF, HY_FF)), full((1, HY_FF)), full((HY_FF, 2 * HY_W)), full((L, 2 * HY_W)),
                  pl.BlockSpec((fb, L), lambda f: (f, 0)), pl.BlockSpec((fb, L), lambda f: (f, 0))],
        out_specs=[pl.BlockSpec((fb, HY_W), lambda f: (f, 0))] * 2,
        out_shape=[jax.ShapeDtypeStruct((L, HY_W), F32)] * 2,
        scratch_shapes=[pltpu.VMEM((L, 2 * HY_W), BF16)],
        compiler_params=_params(("arbitrary",), VMEM_LIMIT),
        name="hy_filter",
    )(z, w1p, b1.reshape(1, -1), freq.reshape(1, -1), w2, b2.reshape(1, -1), w3, dec, c, s)


HY_FREQ_BLOCK = 512


def _hy_conv_body(v_ref, cr_ref, sr_ref, cc_ref, sc_ref, kre_ref, kim_ref, o_ref, *, fb):
    f = pl.program_id(1)
    vb = v_ref[...]
    vre = jnp.dot(cr_ref[...], vb, preferred_element_type=F32)
    vim = jnp.dot(sr_ref[...], vb, preferred_element_type=F32)
    kre = kre_ref[...]
    kim = kim_ref[...]
    grow = f * fb + lax.broadcasted_iota(I32, (fb, 1), 0)
    packed = grow == 0
    yre = vre * kre - jnp.where(packed, 0.0, vim * kim)
    yim = jnp.where(packed, vim * kim, vre * kim + vim * kre)
    part = (jnp.dot(cc_ref[...], yre.astype(BF16), preferred_element_type=F32)
            + jnp.dot(sc_ref[...], yim.astype(BF16), preferred_element_type=F32))

    @pl.when(f == 0)
    def _():
        o_ref[...] = part

    @pl.when(f > 0)
    def _():
        o_ref[...] += part


def _hy_conv(vgb, mats, kre, kim, L, nseq, row_off):
    c, s, st = mats
    fb = min(L, HY_FREQ_BLOCK)
    off = row_off // L
    return pl.pallas_call(
        functools.partial(_hy_conv_body, fb=fb),
        grid=(nseq, L // fb),
        in_specs=[pl.BlockSpec((L, HY_W), lambda b, f: (off + b, 0)),
                  pl.BlockSpec((fb, L), lambda b, f: (f, 0)),
                  pl.BlockSpec((fb, L), lambda b, f: (f, 0)),
                  pl.BlockSpec((L, fb), lambda b, f: (0, f)),
                  pl.BlockSpec((L, fb), lambda b, f: (0, f)),
                  pl.BlockSpec((fb, HY_W), lambda b, f: (f, 0)),
                  pl.BlockSpec((fb, HY_W), lambda b, f: (f, 0))],
        out_specs=pl.BlockSpec((L, HY_W), lambda b, f: (b, 0)),
        out_shape=jax.ShapeDtypeStruct((nseq * L, HY_W), F32),
        compiler_params=_params(("parallel", "arbitrary"), VMEM_LIMIT),
        name="hy_conv",
    )(vgb, c, s, c, st, kre, kim)


def _layer_norm(z, g, b):
    mu = jnp.mean(z, axis=-1, keepdims=True)
    zc = z - mu
    var = jnp.mean(zc * zc, axis=-1, keepdims=True)
    return zc * lax.rsqrt(var + LN_EPS) * g + b


def _post1_body(*refs, even):
    is_ctx = pl.program_id(0) < NCT
    if even:
        (x_ref, m_ref, convc_ref, convs_ref, vg_ref, x0_ref, skip_ref, atc_ref, ats_ref, w_ref, b_ref,
         g_ref, be_ref, rw_ref, rb_ref, x1_ref, xm_ref, lg_ref) = refs
    else:
        (x_ref, m_ref, atc_ref, ats_ref, w_ref, b_ref, g_ref, be_ref, rw_ref, rb_ref, x1_ref, xm_ref,
         lg_ref) = refs
    for r0 in range(0, TM, POST_CHAIN):
        rs = slice(r0, r0 + POST_CHAIN)
        at = jnp.where(is_ctx, atc_ref[rs, :], ats_ref[rs, :])
        if even:
            conv = jnp.where(is_ctx, convc_ref[rs, :], convs_ref[rs, :])
            yh = ((conv + vg_ref[rs, :] * skip_ref[...]) * x0_ref[rs, :]).astype(BF16)
            y = (jnp.dot(yh, w_ref[:HY_W, :], preferred_element_type=F32)
                 + jnp.dot(at, w_ref[HY_W:, :], preferred_element_type=F32))
        else:
            y = jnp.dot(at, w_ref[...], preferred_element_type=F32)
        y = y + b_ref[...]
        x1 = _layer_norm(DN_ALPHA * x_ref[rs, :] + m_ref[2:3, :] * y, g_ref[...], be_ref[...])
        x1_ref[rs, :] = x1
        xm = x1 * (1.0 + m_ref[4:5, :]) + m_ref[3:4, :]
        _store_rows(xm_ref, xm, row0=r0)
        lg_ref[:, rs] = lax.dot_general(rw_ref[...], xm, (((1,), (1,)), ((), ())), precision=HI,
                                        preferred_element_type=F32) + rb_ref[...]


def _post1(x, mods, l, parts, w_bf, b, g, be, rw_t, rb, even):
    row = lambda w: pl.BlockSpec((TM, w), lambda i: (i, 0))
    full = lambda shape: pl.BlockSpec(shape, lambda i: (0,) * len(shape))
    in_specs = [row(D), pl.BlockSpec((None, None, 6, D), lambda i: (l, _mod_row(i), 0, 0))]
    ctx = lambda w: pl.BlockSpec((TM, w), lambda i: (jnp.minimum(i, NCT - 1), 0))
    lat = lambda w: pl.BlockSpec((TM, w), lambda i: (jnp.maximum(i - NCT, 0), 0))
    if even:
        in_specs += [ctx(HY_W), lat(HY_W), row(HY_W), row(HY_W), full((1, HY_W)), ctx(NA_W), lat(NA_W)]
    else:
        in_specs += [ctx(D), lat(D)]
    in_specs += [full((D, D)), full((1, D)), full((1, D)), full((1, D)), full((N_EXPERTS, D)),
                 full((N_EXPERTS, 1))]
    return pl.pallas_call(
        functools.partial(_post1_body, even=even),
        grid=(NT,),
        in_specs=in_specs,
        out_specs=[row(D), pl.BlockSpec((TM * SUB, LANE), lambda i: (i, 0)),
                   pl.BlockSpec((None, N_EXPERTS, TM), lambda i: (i, 0, 0))],
        out_shape=[jax.ShapeDtypeStruct((T, D), F32), jax.ShapeDtypeStruct((T * SUB, LANE), F32),
                   jax.ShapeDtypeStruct((NT, N_EXPERTS, TM), F32)],
        compiler_params=_params(("parallel",), VMEM_LIMIT),
        name="post1",
    )(x, mods, *parts, w_bf, b.reshape(1, D), g.reshape(1, D), be.reshape(1, D), rw_t,
      rb.reshape(N_EXPERTS, 1))


def _route_body(lg_ref, slot_ref, gate_ref, pe_ref, pd_ref, be_ref, rb_ref, nx_ref, par_ref, eidx_scr,
                rank_scr):
    eio = lax.broadcasted_iota(I32, (N_EXPERTS, TM), 0)
    ri = lax.broadcasted_iota(I32, (TM, TM), 0)
    ci = lax.broadcasted_iota(I32, (TM, TM), 1)
    upper = jnp.where(ri < ci, 1.0, 0.0).astype(BF16)

    def tile(i, carry):
        lg = lg_ref[i]
        sel = jnp.zeros((N_EXPERTS, TM), F32)
        vals, hots = [], []
        for k in range(TOP_K):
            mx = jnp.max(lg, axis=0, keepdims=True)
            idx = jnp.min(jnp.where(lg == mx, eio, N_EXPERTS), axis=0, keepdims=True)
            hot = eio == idx
            lg = jnp.where(hot, -jnp.inf, lg)
            sel = sel + jnp.where(hot, 1.0, 0.0)
            vals.append(mx)
            hots.append(hot)
            eidx_scr[i, k:k + 1, :] = idx
        rank = jnp.dot(sel.astype(BF16), upper, preferred_element_type=F32) + carry
        ex = [jnp.exp(v - vals[0]) for v in vals]
        den = ex[0] + ex[1] + ex[2] + ex[3]
        for k in range(TOP_K):
            gate_ref[i, k:k + 1, :] = ex[k] / den
            rank_scr[i, k:k + 1, :] = jnp.sum(jnp.where(hots[k], rank, 0.0), axis=0, keepdims=True)
        return carry + jnp.sum(sel, axis=1, keepdims=True)

    counts = lax.fori_loop(0, NT, tile, jnp.zeros((N_EXPERTS, 1), F32))
    padded = jnp.floor((counts + (MOE_BM - 1)) * (1.0 / MOE_BM)) * MOE_BM
    e_r = lax.broadcasted_iota(I32, (N_EXPERTS, N_EXPERTS), 0)
    e_c = lax.broadcasted_iota(I32, (N_EXPERTS, N_EXPERTS), 1)
    incl = jnp.where(e_c <= e_r, 1.0, 0.0)
    padded_b = jnp.broadcast_to(padded, (N_EXPERTS, LANE))
    pad_end = jnp.dot(incl, padded_b, precision=HI, preferred_element_type=F32)
    pad_start = pad_end[:, 0:1] - padded
    pe_ref[...] = pad_end.astype(I32)
    pd_ref[...] = padded_b.astype(I32)
    blk0 = (lax.broadcasted_iota(I32, (1, TM), 1) * MOE_BM).astype(F32)
    nle = jnp.sum(jnp.where(pad_end[:, 0:1] <= blk0, 1.0, 0.0), axis=0, keepdims=True)
    blk_e = jnp.minimum(nle, N_EXPERTS - 1.0)
    be_ref[...] = blk_e.astype(I32)
    mine = eio.astype(F32) == blk_e
    cnt_b = jnp.sum(jnp.where(mine, counts, 0.0), axis=0, keepdims=True)
    start_b = jnp.sum(jnp.where(mine, pad_start, 0.0), axis=0, keepdims=True)
    rb_ref[...] = jnp.clip(cnt_b - (blk0 - start_b), 0.0, float(MOE_BM)).astype(I32)
    used = padded > 0.0
    e_f = eio.astype(F32)
    nx_ref[...] = jnp.min(jnp.where((e_f > blk_e) & used, e_f, float(N_EXPERTS)), axis=0,
                          keepdims=True).astype(I32)
    ordinal = jnp.sum(jnp.where((e_f < blk_e) & used, 1.0, 0.0), axis=0, keepdims=True)
    par_ref[...] = (ordinal - 2.0 * jnp.floor(ordinal * 0.5)).astype(I32)

    def place(i, c):
        for k in range(TOP_K):
            hot = eio == eidx_scr[i, k:k + 1, :]
            start = jnp.sum(jnp.where(hot, pad_start, 0.0), axis=0, keepdims=True)
            slot_ref[i, k:k + 1, :] = (start + rank_scr[i, k:k + 1, :]).astype(I32)
        return c

    lax.fori_loop(0, NT, place, 0)


def _route(logits):
    return pl.pallas_call(
        _route_body,
        out_shape=[jax.ShapeDtypeStruct((NT, TOP_K, TM), I32), jax.ShapeDtypeStruct((NT, TOP_K, TM), F32),
                   jax.ShapeDtypeStruct((N_EXPERTS, LANE), I32), jax.ShapeDtypeStruct((N_EXPERTS, LANE), I32),
                   ] + [jax.ShapeDtypeStruct((1, TM), I32)] * 4,
        scratch_shapes=[pltpu.VMEM((NT, TOP_K, TM), I32), pltpu.VMEM((NT, TOP_K, TM), F32)],
        compiler_params=_params(None, VMEM_LIMIT),
        name="route",
    )(logits)


def _dispatch_body(slot_ref, pe_ref, pd_ref, xm_ref, xb_ref, zero_scr, sem):
    i = pl.program_id(0)

    def fill_block(b):
        start = b * (MOE_BM * SUB)
        if not isinstance(b, int):
            start = pl.multiple_of(start, MOE_BM * SUB)
        return pltpu.make_async_copy(zero_scr, xb_ref.at[pl.ds(start, MOE_BM * SUB)], sem)

    @pl.when(i == 0)
    def _():
        zero_scr[...] = jnp.zeros_like(zero_scr)
        n_used = pe_ref[N_EXPERTS - 1] // MOE_BM
        for start_or_wait in (True, False):
            for e in range(N_EXPERTS):
                @pl.when(pd_ref[e] > 0)
                def _():
                    cp = fill_block(pe_ref[e] // MOE_BM - 1)
                    cp.start() if start_or_wait else cp.wait()
            for b in range(T * TOP_K // MOE_BM, N_SLOT_BLOCKS):
                @pl.when(b >= n_used)
                def _():
                    cp = fill_block(b)
                    cp.start() if start_or_wait else cp.wait()

    def row_copy(r, k):
        s = slot_ref[i * (TOP_K * TM) + k * TM + r]
        return pltpu.make_async_copy(xm_ref.at[pl.ds(pl.multiple_of(r * SUB, SUB), SUB)],
                                     xb_ref.at[pl.ds(pl.multiple_of(s * SUB, SUB), SUB)], sem)

    def issue(r, c):
        for k in range(TOP_K):
            row_copy(r, k).start(priority=k % DMA_PRIORITIES)
        return c

    def drain(r, c):
        for k in range(TOP_K):
            row_copy(r, k).wait()
        return c

    lax.fori_loop(0, TM, issue, 0, unroll=ROW_DMA_UNROLL)
    lax.fori_loop(0, TM, drain, 0, unroll=ROW_DMA_UNROLL)


def _dispatch(slot_flat, pe, pd, xm):
    return pl.pallas_call(
        _dispatch_body,
        grid_spec=pltpu.PrefetchScalarGridSpec(
            num_scalar_prefetch=3, grid=(NT,),
            in_specs=[pl.BlockSpec((TM * SUB, LANE), lambda i, s, a, b: (i, 0))],
            out_specs=pl.BlockSpec(memory_space=pl.ANY),
            scratch_shapes=[pltpu.VMEM((MOE_BM * SUB, LANE), F32), pltpu.SemaphoreType.DMA]),
        out_shape=jax.ShapeDtypeStruct((N_SLOTS * SUB, LANE), F32),
        compiler_params=_params(("arbitrary",), VMEM_LIMIT),
        name="moe_dispatch",
    )(slot_flat, pe, pd, xm)


def _expert_body(be_ref, nu_ref, rb_ref, nx_ref, par_ref, xb_ref, guw_hbm, gub_ref, dnw_hbm, dnb_ref, yb_ref,
                 gu_f32, dn_f32, gu_scr, dn_scr, sem, *, l):
    i = pl.program_id(0)
    rows = rb_ref[i]

    def weight_copies(e, slot):
        return (pltpu.make_async_copy(guw_hbm.at[l, e], gu_f32.at[slot], sem.at[0, slot]),
                pltpu.make_async_copy(dnw_hbm.at[l, e], dn_f32.at[slot], sem.at[1, slot]))

    def chain(r0):
        xb = _load_rows(xb_ref, MOE_CHAIN, row0=r0).astype(BF16)
        hgu = jnp.dot(xb, gu_scr[...], preferred_element_type=F32) + gub_ref[...]
        g = jnp.minimum(hgu[:, :D_FF], SWIGLU_LIMIT)
        lin = jnp.clip(hgu[:, D_FF:], -SWIGLU_LIMIT, SWIGLU_LIMIT)
        act = (lin + 1.0) * (g / (1.0 + jnp.exp(-SWIGLU_ALPHA * g)))
        y = jnp.dot(act.astype(BF16), dn_scr[...], preferred_element_type=F32) + dnb_ref[...]
        _store_rows(yb_ref, y, row0=r0)

    @pl.when(i >= nu_ref[0])
    def _():
        yb_ref[...] = jnp.zeros_like(yb_ref)

    @pl.when(i < nu_ref[0])
    def _():
        e = be_ref[i]
        slot = par_ref[i]

        @pl.when(i == 0)
        def _():
            for cp in weight_copies(e, slot):
                cp.start()

        @pl.when((i == 0) | (e != be_ref[jnp.maximum(i - 1, 0)]))
        def _():
            nxt = nx_ref[i]

            @pl.when(nxt < N_EXPERTS)
            def _():
                for cp in weight_copies(nxt, 1 - slot):
                    cp.start()

            for cp in weight_copies(e, slot):
                cp.wait()
            gu_scr[...] = gu_f32[slot].astype(BF16)
            dn_scr[...] = dn_f32[slot].astype(BF16)

        @pl.when(rows > MOE_CHAIN)
        def _():
            for r0 in range(0, MOE_BM, MOE_CHAIN):
                chain(r0)

        @pl.when(rows <= MOE_CHAIN)
        def _():
            chain(0)
            yb_ref[MOE_CHAIN * SUB:, :] = jnp.zeros((MOE_BM * SUB - MOE_CHAIN * SUB, LANE), F32)


def _experts(block_e, n_used, block_rows, next_e, parity, xb, l, gu_w, gu_b, dn_w, dn_b):
    def blk(i, nu):
        return jnp.minimum(i, nu[0] - 1)

    def bias(width):
        return pl.BlockSpec((None, None, 1, width), lambda i, be, nu, rb, nx, par: (l, be[blk(i, nu)], 0, 0))

    return pl.pallas_call(
        functools.partial(_expert_body, l=l),
        grid_spec=pltpu.PrefetchScalarGridSpec(
            num_scalar_prefetch=5, grid=(N_SLOT_BLOCKS,),
            in_specs=[pl.BlockSpec((MOE_BM * SUB, LANE), lambda i, be, nu, rb, nx, par: (blk(i, nu), 0)),
                      pl.BlockSpec(memory_space=pl.ANY), bias(2 * D_FF),
                      pl.BlockSpec(memory_space=pl.ANY), bias(D)],
            out_specs=pl.BlockSpec((MOE_BM * SUB, LANE), lambda i, be, nu, rb, nx, par: (i, 0)),
            scratch_shapes=[pltpu.VMEM((2, D, 2 * D_FF), F32), pltpu.VMEM((2, D_FF, D), F32),
                            pltpu.VMEM((D, 2 * D_FF), BF16), pltpu.VMEM((D_FF, D), BF16),
                            pltpu.SemaphoreType.DMA((2, 2))]),
        out_shape=jax.ShapeDtypeStruct((N_SLOTS * SUB, LANE), F32),
        compiler_params=_params(("arbitrary",), VMEM_LIMIT),
        name="moe_experts",
    )(block_e, n_used, block_rows, next_e, parity, xb, gu_w, gu_b.reshape(DEPTH, N_EXPERTS, 1, 2 * D_FF), dn_w,
      dn_b.reshape(DEPTH, N_EXPERTS, 1, D))


def _post2_body(slot_ref, x1_ref, gt_ref, m_ref, g_ref, be_ref, yb_ref, x2_ref, y_scr, sem):
    i = pl.program_id(0)
    buf = i % 2

    def row_copy(t, r, k):
        s = slot_ref[t * (TOP_K * TM) + k * TM + r]
        return pltpu.make_async_copy(yb_ref.at[pl.ds(pl.multiple_of(s * SUB, SUB), SUB)],
                                     y_scr.at[t % 2, k, pl.ds(pl.multiple_of(r * SUB, SUB), SUB)],
                                     sem.at[t % 2])

    def fetch(t):
        def issue(r, c):
            for k in range(TOP_K):
                row_copy(t, r, k).start(priority=k % DMA_PRIORITIES)
            return c
        lax.fori_loop(0, TM, issue, 0, unroll=ROW_DMA_UNROLL)

    @pl.when(i == 0)
    def _():
        fetch(i)

    @pl.when(i + 1 < NT)
    def _():
        fetch(i + 1)

    def drain(r, c):
        for k in range(TOP_K):
            row_copy(i, r, k).wait()
        return c

    lax.fori_loop(0, TM, drain, 0, unroll=ROW_DMA_UNROLL)
    f = _load_rows(y_scr, TM, (buf, 0)) * gt_ref[:, 0:1]
    for k in range(1, TOP_K):
        f = f + _load_rows(y_scr, TM, (buf, k)) * gt_ref[:, k:k + 1]
    x2_ref[...] = _layer_norm(DN_ALPHA * x1_ref[...] + m_ref[5:6, :] * f, g_ref[...], be_ref[...])


def _post2(slot_flat, x1, gates_t, mods, l, g, be, yb):
    full = lambda shape: pl.BlockSpec(shape, lambda i, s: (0,) * len(shape))
    return pl.pallas_call(
        _post2_body,
        grid_spec=pltpu.PrefetchScalarGridSpec(
            num_scalar_prefetch=1, grid=(NT,),
            in_specs=[pl.BlockSpec((TM, D), lambda i, s: (i, 0)),
                      pl.BlockSpec((TM, TOP_K), lambda i, s: (i, 0)),
                      pl.BlockSpec((None, None, 6, D), lambda i, s: (l, _mod_row(i), 0, 0)),
                      full((1, D)), full((1, D)),
                      pl.BlockSpec(memory_space=pl.ANY)],
            out_specs=pl.BlockSpec((TM, D), lambda i, s: (i, 0)),
            scratch_shapes=[pltpu.VMEM((2, TOP_K, TM * SUB, LANE), F32), pltpu.SemaphoreType.DMA((2,))]),
        out_shape=jax.ShapeDtypeStruct((T, D), F32),
        compiler_params=_params(("arbitrary",), VMEM_LIMIT),
        name="post2",
    )(slot_flat, x1, gates_t, mods, g.reshape(1, D), be.reshape(1, D), yb)


def _rope_tables():
    quarter = HEAD // 4
    pos = jnp.arange(DEC_SEQ)
    rows = (pos // GRID_W).astype(F32)[:, None]
    cols = (pos % GRID_W).astype(F32)[:, None]
    lane = jnp.arange(LANE)[None, :]
    d = lane % HEAD
    inv = ROPE_THETA ** (-(d % quarter).astype(F32) / quarter)
    ang = jnp.where(d < HEAD // 2, rows, cols) * inv
    sign = jnp.where((d // quarter) % 2 == 0, -1.0, 1.0)
    cos = jnp.concatenate([jnp.ones((TC, LANE), F32), jnp.tile(jnp.cos(ang), (DEC_BATCH, 1))], axis=0)
    sin = jnp.concatenate([jnp.zeros((TC, LANE), F32), jnp.tile(sign * jnp.sin(ang), (DEC_BATCH, 1))], axis=0)
    return cos, sin


def _moe_and_norm(x1, xm, logits, mods, l, ln_g, ln_b, exp_gu_w, exp_gu_b, exp_dn_w, exp_dn_b):
    slot, gates, pe, pd, be, rb, nx, par = _route(logits)
    slot_flat = slot.reshape(-1)
    pe1 = pe[:, 0]
    xb = _dispatch(slot_flat, pe1, pd[:, 0], xm)
    n_used = pe1[N_EXPERTS - 1:] // MOE_BM
    yb = _experts(be.reshape(-1), n_used, rb.reshape(-1), nx.reshape(-1), par.reshape(-1), xb, l,
                  exp_gu_w, exp_gu_b, exp_dn_w, exp_dn_b)
    gates_t = gates.transpose(0, 2, 1).reshape(T, TOP_K)
    return _post2(slot_flat, x1, gates_t, mods, l, ln_g[l, 1], ln_b[l, 1], yb)


def kernel(x_prompt, x_sample, cache_na_k, cache_na_v, cache_swa_k, cache_swa_v, c, c_ctx, ada_w, ada_b, ln_g, ln_b, ev_in_w, ev_in_b, hy_conv_w, hy_conv_b, hy_f_w1, hy_f_b1, hy_f_freq, hy_f_w2, hy_f_b2, hy_f_w3, hy_skip, na_rpb, od_in_w, od_in_b, swa_sink, mix_out_w, mix_out_b, router_w, router_b, exp_gu_w, exp_gu_b, exp_dn_w, exp_dn_b):
    x = jnp.concatenate([x_prompt.reshape(TC, D), x_sample.reshape(TS, D)], axis=0)
    cond8 = jnp.concatenate([c_ctx[None, :], c, jnp.zeros((8 - 1 - DEC_BATCH, D), F32)], axis=0)
    mods = _adaln(cond8, ada_w, ada_b).reshape(DEPTH, 8, 6, D)

    rope_tabs = _rope_tables()
    hy_consts = {L: (_hy_features(L), _dft_mats(L)) for L in (SEQ, DEC_SEQ)}
    ck_na = cache_na_k.reshape(DEC_BATCH, -1, PAST, NA_W)
    cv_na = cache_na_v.reshape(DEC_BATCH, -1, PAST, NA_W)
    ck_swa = cache_swa_k.reshape(DEC_BATCH, -1, PAST, SWA_KV * HEAD)
    cv_swa = cache_swa_v.reshape(DEC_BATCH, -1, PAST, SWA_KV * HEAD)

    na_k, na_v, swa_k, swa_v = [], [], [], []
    for l in range(DEPTH):
        w_out = mix_out_w[l].astype(BF16)
        rw_t = router_w[l].T
        if l % 2 == 0:
            e = l // 2
            x0, vg, vgb, q, k, v = _inproj_even(x, mods, l, ev_in_w[e].astype(BF16), ev_in_b[e],
                                                hy_conv_w[e].reshape(3, HY_PROJ), hy_conv_b[e].reshape(1, HY_PROJ))
            convs = []
            for L, nseq, off in ((SEQ, BATCH, 0), (DEC_SEQ, DEC_BATCH, TC)):
                feats, mats = hy_consts[L]
                kre, kim = _hy_filter(L, feats, mats, hy_f_w1[e], hy_f_b1[e], hy_f_freq[e], hy_f_w2[e],
                                      hy_f_b2[e], hy_f_w3[e])
                convs.append(_hy_conv(vgb, mats, kre, kim, L, nseq, off))
            parts = (convs[0], convs[1], vg, x0, hy_skip[e].reshape(1, HY_W), _ctx_even_attn(q, k, v),
                     _na_attn(q, k, v, ck_na, cv_na, _na_biasmask(na_rpb[e]), e))
            na_k.append(k[:TC].reshape(BATCH, SEQ, NA_HEADS, HEAD))
            na_v.append(v[:TC].reshape(BATCH, SEQ, NA_HEADS, HEAD))
        else:
            o = l // 2
            q, k, v = _inproj(x, mods, l, od_in_w[o].astype(BF16), od_in_b[o],
                              ((SWA_HEADS * HEAD, True, BF16), (SWA_KV * HEAD, True, F32),
                               (SWA_KV * HEAD, False, F32)),
                              rope_tabs)
            parts = (_ctx_odd_attn(swa_sink[o], q, k, v), _swa_attn(swa_sink[o], q, k, v, ck_swa, cv_swa, o))
            swa_k.append(k[:TC].reshape(BATCH, SEQ, SWA_KV, HEAD))
            swa_v.append(v[:TC].reshape(BATCH, SEQ, SWA_KV, HEAD))
        x1, xm, logits = _post1(x, mods, l, parts, w_out, mix_out_b[l], ln_g[l, 0], ln_b[l, 0], rw_t,
                                router_b[l], l % 2 == 0)
        x = _moe_and_norm(x1, xm, logits, mods, l, ln_g, ln_b, exp_gu_w, exp_gu_b, exp_dn_w, exp_dn_b)

    return (x[:TC].reshape(BATCH, SEQ, D), x[TC:].reshape(DEC_BATCH, DEC_SEQ, D),
            jnp.stack(na_k, axis=1), jnp.stack(na_v, axis=1), jnp.stack(swa_k, axis=1), jnp.stack(swa_v, axis=1))
```

```python
import functools
import math

import jax
import jax.numpy as jnp
from jax import lax
from jax.experimental import pallas as pl
from jax.experimental.pallas import tpu as pltpu

F32 = jnp.float32
BF16 = jnp.bfloat16
I32 = jnp.int32
HI = lax.Precision.HIGHEST

D = 1024
DEPTH = 4
BATCH, SEQ = 16, 256
DEC_BATCH, DEC_SEQ = 4, 2048
PAST = 256
GRID_W = 64
HEAD = 64
HY_W = 512
HY_PROJ = 3 * HY_W
HY_BANDS = 16
HY_EMB = 1 + 2 * HY_BANDS
HY_FF = 64
NA_HEADS = 8
NA_W = NA_HEADS * HEAD
NA_KH, NA_KW = 8, 16
SWA_HEADS, SWA_KV = 16, 4
SWA_WINDOW = 128
N_EXPERTS, TOP_K = 32, 4
D_FF = 1024
SWIGLU_LIMIT = 7.0
SWIGLU_ALPHA = 1.702
DN_ALPHA = (2 * DEPTH) ** 0.25
LN_EPS = 1e-5
NEG = -1e30
LOG2E = 1.0 / math.log(2.0)
ROPE_THETA = 10000.0

TC = BATCH * SEQ
TS = DEC_BATCH * DEC_SEQ
T = TC + TS
TM = 256
NT = T // TM
NCT = TC // TM
TPS = DEC_SEQ // TM
LANE = 128
MOE_BM = 512
MOE_CHAIN = 256
N_SLOT_BLOCKS = T * TOP_K // MOE_BM + N_EXPERTS
N_SLOTS = N_SLOT_BLOCKS * MOE_BM
VMEM_LIMIT = 56 * 1024 * 1024
DMA_PRIORITIES = 2
ROW_DMA_UNROLL = 8
POST_CHAIN = 128


def _params(sem, vmem=None):
    return pltpu.CompilerParams(dimension_semantics=sem, vmem_limit_bytes=vmem)


def _mod_row(i):
    return jnp.where(i < NCT, 0, 1 + (i - NCT) // TPS)


SUB = 8
ROW_TILES = D // LANE


def _load_rows(ref, nrows, lead=(), row0=0):
    parts = [ref[lead + (pl.ds(row0 * ROW_TILES + s, nrows, stride=ROW_TILES), slice(None))]
             for s in range(ROW_TILES)]
    return jnp.concatenate(parts, axis=1)


def _store_rows(ref, val, row0=0):
    nrows = val.shape[0]
    for s in range(ROW_TILES):
        ref[pl.ds(row0 * ROW_TILES + s, nrows, stride=ROW_TILES), :] = val[:, s * LANE:(s + 1) * LANE]


def _adaln_body(c_ref, w_ref, b_ref, o_ref):
    c = c_ref[...]
    s = c / (1.0 + jnp.exp(-c))
    o_ref[...] = jnp.dot(s, w_ref[...], precision=HI, preferred_element_type=F32) + b_ref[...]


def _adaln(cond8, ada_w, ada_b):
    nt = 3 * D
    return pl.pallas_call(
        _adaln_body,
        grid=(DEPTH, 6 * D // nt),
        in_specs=[pl.BlockSpec((8, D), lambda l, j: (0, 0)),
                  pl.BlockSpec((None, D, nt), lambda l, j: (l, 0, j)),
                  pl.BlockSpec((None, 1, nt), lambda l, j: (l, 0, j))],
        out_specs=pl.BlockSpec((None, 8, nt), lambda l, j: (l, 0, j)),
        out_shape=jax.ShapeDtypeStruct((DEPTH, 8, 6 * D), F32),
        compiler_params=_params(("arbitrary", "arbitrary"), VMEM_LIMIT),
        name="adaln",
    )(cond8, ada_w, ada_b.reshape(DEPTH, 1, 6 * D))


def _rope(x, cos, sin):
    lane = lax.broadcasted_iota(I32, (1, LANE), 1)
    first = (lane & 16) == 0
    outs = []
    for j in range(x.shape[1] // LANE):
        blk = x[:, j * LANE:(j + 1) * LANE]
        partner = jnp.where(first, pltpu.roll(blk, LANE - 16, 1), pltpu.roll(blk, 16, 1))
        outs.append(blk * cos + partner * sin)
    return jnp.concatenate(outs, axis=1) if len(outs) > 1 else outs[0]


def _store_context_kv(kc_ref, vc_ref, k, v):
    @pl.when(pl.program_id(0) < NCT)
    def _():
        kc_ref[...] = k
        vc_ref[...] = v


def _ctx_rows(width):
    return pl.BlockSpec((TM, width), lambda i: (jnp.minimum(i, NCT - 1), 0))


def _inproj_body(*refs, splits, rope):
    if rope:
        x_ref, m_ref, w_ref, b_ref, cos_ref, sin_ref = refs[:6]
        outs = refs[6:]
    else:
        x_ref, m_ref, w_ref, b_ref = refs[:4]
        outs = refs[4:]
    h = x_ref[...] * (1.0 + m_ref[1:2, :]) + m_ref[0:1, :]
    y = jnp.dot(h.astype(BF16), w_ref[...], preferred_element_type=F32) + b_ref[...]
    off = 0
    parts = []
    for o_ref, (width, do_rope, _) in zip(outs, splits):
        part = y[:, off:off + width]
        if do_rope:
            part = _rope(part, cos_ref[...], sin_ref[...])
        o_ref[...] = part.astype(o_ref.dtype)
        parts.append(part)
        off += width
    _store_context_kv(outs[-2], outs[-1], parts[-2], parts[-1])


def _inproj(x, mods, l, w_bf, b, splits, rope_tabs=None):
    p = w_bf.shape[1]
    in_specs = [pl.BlockSpec((TM, D), lambda i: (i, 0)),
                pl.BlockSpec((None, None, 6, D), lambda i: (l, _mod_row(i), 0, 0)),
                pl.BlockSpec((D, p), lambda i: (0, 0)),
                pl.BlockSpec((1, p), lambda i: (0, 0))]
    args = [x, mods, w_bf, b.reshape(1, p)]
    if rope_tabs is not None:
        in_specs += [pl.BlockSpec((TM, LANE), lambda i: (i, 0))] * 2
        args += list(rope_tabs)
    return pl.pallas_call(
        functools.partial(_inproj_body, splits=splits, rope=rope_tabs is not None),
        grid=(NT,),
        in_specs=in_specs,
        out_specs=[pl.BlockSpec((TM, w), lambda i: (i, 0)) for w, _, _ in splits]
        + [_ctx_rows(w) for w, _, _ in splits[-2:]],
        out_shape=[jax.ShapeDtypeStruct((T, w), dt) for w, _, dt in splits]
        + [jax.ShapeDtypeStruct((TC, w), F32) for w, _, _ in splits[-2:]],
        compiler_params=_params(("arbitrary",), VMEM_LIMIT),
        name="inproj",
    )(*args)


HALO = SUB


def _inproj_even_body(xp_ref, x_ref, xn_ref, m_ref, w_ref, b_ref, cw_ref, cb_ref,
                      x0_ref, vg_ref, vgb_ref, q_ref, k_ref, v_ref, kc_ref, vc_ref):
    i = pl.program_id(0)
    ti = (i - NCT) % TPS
    first = (i < NCT) | (ti == 0)
    last = (i < NCT) | (ti == TPS - 1)
    xs = jnp.concatenate([xp_ref[...], x_ref[...], xn_ref[...]], axis=0)
    h = xs * (1.0 + m_ref[1:2, :]) + m_ref[0:1, :]
    y = jnp.dot(h.astype(BF16), w_ref[...], preferred_element_type=F32) + b_ref[...]
    u = y[:, :HY_PROJ]
    row = lax.broadcasted_iota(I32, (TM, 1), 0)
    before = jnp.where((row == 0) & first, 0.0, u[HALO - 1:HALO - 1 + TM])
    after = jnp.where((row == TM - 1) & last, 0.0, u[HALO + 1:HALO + 1 + TM])
    uc = cw_ref[0:1, :] * before + cw_ref[1:2, :] * u[HALO:HALO + TM] + cw_ref[2:3, :] * after + cb_ref[...]
    x0_ref[...] = uc[:, :HY_W]
    vg = uc[:, 2 * HY_W:] * uc[:, HY_W:2 * HY_W]
    vg_ref[...] = vg
    vgb_ref[...] = vg.astype(BF16)
    qkv = y[HALO:HALO + TM, HY_PROJ:]
    q_ref[...] = qkv[:, :NA_W].astype(q_ref.dtype)
    k_ref[...] = qkv[:, NA_W:2 * NA_W]
    v_ref[...] = qkv[:, 2 * NA_W:]
    _store_context_kv(kc_ref, vc_ref, qkv[:, NA_W:2 * NA_W], qkv[:, 2 * NA_W:])


def _inproj_even(x, mods, l, w_bf, b, conv_w, conv_b):
    p = w_bf.shape[1]
    nb = TM // HALO
    full = lambda shape: pl.BlockSpec(shape, lambda i: (0,) * len(shape))
    row = lambda w: pl.BlockSpec((TM, w), lambda i: (i, 0))
    return pl.pallas_call(
        _inproj_even_body,
        grid=(NT,),
        in_specs=[pl.BlockSpec((HALO, D), lambda i: (jnp.maximum(i * nb - 1, 0), 0)),
                  row(D),
                  pl.BlockSpec((HALO, D), lambda i: (jnp.minimum(i * nb + nb, T // HALO - 1), 0)),
                  pl.BlockSpec((None, None, 6, D), lambda i: (l, _mod_row(i), 0, 0)),
                  full((D, p)), full((1, p)), full((3, HY_PROJ)), full((1, HY_PROJ))],
        out_specs=[row(HY_W)] * 3 + [row(NA_W)] * 3 + [_ctx_rows(NA_W)] * 2,
        out_shape=[jax.ShapeDtypeStruct((T, HY_W), F32), jax.ShapeDtypeStruct((T, HY_W), F32),
                   jax.ShapeDtypeStruct((T, HY_W), BF16), jax.ShapeDtypeStruct((T, NA_W), BF16),
                   jax.ShapeDtypeStruct((T, NA_W), F32), jax.ShapeDtypeStruct((T, NA_W), F32),
                   jax.ShapeDtypeStruct((TC, NA_W), F32), jax.ShapeDtypeStruct((TC, NA_W), F32)],
        compiler_params=_params(("arbitrary",), VMEM_LIMIT),
        name="inproj_even",
    )(x, x, x, mods, w_bf, b.reshape(1, p), conv_w, conv_b)


def _lo_mask():
    return lax.broadcasted_iota(I32, (1, LANE), 1) < HEAD


def _dup_half(x2, upper):
    swapped = pltpu.roll(x2, HEAD, 1)
    keep = jnp.logical_xor(_lo_mask(), upper)
    return jnp.where(keep, x2, swapped)


def _pair_attention(q2, segs, sink_lo=None, sink_hi=None):
    m_rows = q2.shape[0]
    lo = _lo_mask()
    q2 = q2.astype(F32) * (HEAD ** -0.5 * LOG2E)
    qs = jnp.concatenate([jnp.where(lo, q2, 0.0), jnp.where(lo, 0.0, q2)], axis=0).astype(BF16)
    scores = []
    for kd, _, bias in segs:
        s = lax.dot_general(qs, kd, (((1,), (1,)), ((), ())), preferred_element_type=F32)
        if bias is not None:
            s = s + bias
        scores.append(s)
    mx = functools.reduce(jnp.maximum, [jnp.max(s, axis=-1, keepdims=True) for s in scores])
    sink = None
    if sink_lo is not None:
        row = lax.broadcasted_iota(I32, (2 * m_rows, 1), 0)
        sink = jnp.where(row < m_rows, sink_lo, sink_hi) * LOG2E
        mx = jnp.maximum(mx, sink)
    den = jnp.zeros_like(mx)
    acc = jnp.zeros((2 * m_rows, LANE), F32)
    for s, (_, vd, _) in zip(scores, segs):
        p = jnp.exp2(s - mx)
        den = den + jnp.sum(p, axis=-1, keepdims=True)
        acc = acc + jnp.dot(p.astype(BF16), vd, preferred_element_type=F32)
    if sink is not None:
        den = den + jnp.exp2(sink - mx)
    o = acc / den
    return jnp.where(lo, o[:m_rows], o[m_rows:])


def _lanes(j, width=LANE):
    return slice(j * width, (j + 1) * width)


def _ctx_even_body(q_ref, k_ref, v_ref, o_ref):
    for j in range(NA_W // LANE):
        segs = [(k_ref[:, _lanes(j)].astype(BF16), v_ref[:, _lanes(j)].astype(BF16), None)]
        o_ref[:, _lanes(j)] = _pair_attention(q_ref[:, _lanes(j)], segs).astype(o_ref.dtype)


def _ctx_even_attn(q, k, v):
    spec = pl.BlockSpec((SEQ, NA_W), lambda b: (b, 0))
    return pl.pallas_call(
        _ctx_even_body,
        grid=(BATCH,),
        in_specs=[spec, spec, spec],
        out_specs=spec,
        out_shape=jax.ShapeDtypeStruct((TC, NA_W), BF16),
        compiler_params=_params(("parallel",)),
        name="ctx_attn_even",
    )(q, k, v)


def _ctx_odd_body(sink_ref, q_ref, k_ref, v_ref, o_ref):
    for g in range(SWA_KV):
        upper = (g % 2) == 1
        kd = _dup_half(k_ref[:, _lanes(g // 2)], upper).astype(BF16)
        vd = _dup_half(v_ref[:, _lanes(g // 2)], upper).astype(BF16)
        for j in (2 * g, 2 * g + 1):
            o_ref[:, _lanes(j)] = _pair_attention(q_ref[:, _lanes(j)], [(kd, vd, None)],
                                                  sink_ref[2 * j], sink_ref[2 * j + 1]).astype(o_ref.dtype)


def _ctx_odd_attn(sink, q, k, v):
    qspec = pl.BlockSpec((SEQ, SWA_HEADS * HEAD), lambda b, s: (b, 0))
    kspec = pl.BlockSpec((SEQ, SWA_KV * HEAD), lambda b, s: (b, 0))
    return pl.pallas_call(
        _ctx_odd_body,
        grid_spec=pltpu.PrefetchScalarGridSpec(
            num_scalar_prefetch=1, grid=(BATCH,),
            in_specs=[qspec, kspec, kspec], out_specs=qspec),
        out_shape=jax.ShapeDtypeStruct((TC, SWA_HEADS * HEAD), BF16),
        compiler_params=_params(("parallel",)),
        name="ctx_attn_odd",
    )(sink, q, k, v)


SWA_BLK = 128
SWA_NB = DEC_SEQ // SWA_BLK


def _swa_body(sink_ref, q_ref, kp_ref, kc_ref, kn_ref, vp_ref, vc_ref, vn_ref, ck_ref, cv_ref, o_ref):
    n = pl.program_id(1)
    qi = lax.broadcasted_iota(I32, (SWA_BLK, 3 * SWA_BLK), 0)
    kj = lax.broadcasted_iota(I32, (SWA_BLK, 3 * SWA_BLK), 1)
    kpos = (n - 1) * SWA_BLK + kj
    rel = kj - SWA_BLK - qi
    ok = (jnp.abs(rel) <= SWA_WINDOW) & (kpos >= 0) & (kpos < DEC_SEQ)
    mask = jnp.where(ok, 0.0, NEG)
    mask2 = jnp.concatenate([mask, mask], axis=0)
    for g in range(SWA_KV):
        upper = (g % 2) == 1
        kv = _lanes(g // 2)
        kloc = jnp.concatenate([kp_ref[:, kv], kc_ref[:, kv], kn_ref[:, kv]], axis=0)
        vloc = jnp.concatenate([vp_ref[:, kv], vc_ref[:, kv], vn_ref[:, kv]], axis=0)
        segs = [(_dup_half(kloc, upper).astype(BF16), _dup_half(vloc, upper).astype(BF16), mask2),
                (_dup_half(ck_ref[:, kv], upper).astype(BF16), _dup_half(cv_ref[:, kv], upper).astype(BF16),
                 None)]
        for j in (2 * g, 2 * g + 1):
            o_ref[:, _lanes(j)] = _pair_attention(q_ref[:, _lanes(j)], segs, sink_ref[2 * j],
                                                  sink_ref[2 * j + 1]).astype(o_ref.dtype)


def _swa_attn(sink, q, k, v, ck, cv, o):
    base = TC // SWA_BLK
    qspec = pl.BlockSpec((SWA_BLK, SWA_HEADS * HEAD), lambda b, n, s: (base + b * SWA_NB + n, 0))

    def kv(shift):
        return pl.BlockSpec(
            (SWA_BLK, SWA_KV * HEAD),
            lambda b, n, s: (base + b * SWA_NB + jnp.clip(n + shift, 0, SWA_NB - 1), 0))

    cspec = pl.BlockSpec((None, None, PAST, SWA_KV * HEAD), lambda b, n, s: (b, o, 0, 0))
    ospec = pl.BlockSpec((SWA_BLK, SWA_HEADS * HEAD), lambda b, n, s: (b * SWA_NB + n, 0))
    return pl.pallas_call(
        _swa_body,
        grid_spec=pltpu.PrefetchScalarGridSpec(
            num_scalar_prefetch=1, grid=(DEC_BATCH, SWA_NB),
            in_specs=[qspec, kv(-1), kv(0), kv(1), kv(-1), kv(0), kv(1), cspec, cspec],
            out_specs=ospec),
        out_shape=jax.ShapeDtypeStruct((TS, SWA_HEADS * HEAD), BF16),
        compiler_params=_params(("parallel", "parallel")),
        name="swa_attn",
    )(sink, q, k, k, k, v, v, v, ck, cv)


NA_QROWS = 4
NA_TILES = DEC_SEQ // TM
NA_KTILES = 3
NA_KSTART_MAX = NA_TILES - NA_KTILES


def _na_body(q_ref, k0_ref, k1_ref, k2_ref, v0_ref, v1_ref, v2_ref, ck_ref, cv_ref, bias_ref, o_ref):
    for j in range(NA_W // LANE):
        ln = _lanes(j)
        kloc = jnp.concatenate([k0_ref[:, ln], k1_ref[:, ln], k2_ref[:, ln]], axis=0).astype(BF16)
        vloc = jnp.concatenate([v0_ref[:, ln], v1_ref[:, ln], v2_ref[:, ln]], axis=0).astype(BF16)
        bias = jnp.concatenate([bias_ref[2 * j], bias_ref[2 * j + 1]], axis=0)
        segs = [(kloc, vloc, bias), (ck_ref[:, ln].astype(BF16), cv_ref[:, ln].astype(BF16), None)]
        o_ref[:, ln] = _pair_attention(q_ref[:, ln], segs).astype(o_ref.dtype)


def _na_attn(q, k, v, ck, cv, biasmask, e):
    base = NCT

    def kstart(i):
        return jnp.clip(i - 1, 0, NA_KSTART_MAX)

    qspec = pl.BlockSpec((TM, NA_W), lambda i, b: (base + b * NA_TILES + i, 0))

    def kv(t):
        return pl.BlockSpec((TM, NA_W), lambda i, b: (base + b * NA_TILES + kstart(i) + t, 0))

    cspec = pl.BlockSpec((None, None, PAST, NA_W), lambda i, b: (b, e, 0, 0))
    bspec = pl.BlockSpec((None, NA_HEADS, TM, NA_KTILES * TM),
                         lambda i, b: (jnp.where(i == 0, 0, jnp.where(i == NA_TILES - 1, 2, 1)), 0, 0, 0))
    ospec = pl.BlockSpec((TM, NA_W), lambda i, b: (b * NA_TILES + i, 0))
    return pl.pallas_call(
        _na_body,
        grid=(NA_TILES, DEC_BATCH),
        in_specs=[qspec, kv(0), kv(1), kv(2), kv(0), kv(1), kv(2), cspec, cspec, bspec],
        out_specs=ospec,
        out_shape=jax.ShapeDtypeStruct((TS, NA_W), BF16),
        compiler_params=_params(("parallel", "parallel"), VMEM_LIMIT),
        name="na_attn",
    )(q, k, k, k, v, v, v, ck, cv, biasmask)


NA_REL_ROWS = 2 * NA_KH - 1
NA_REL_COLS = 2 * NA_KW - 1


def _na_bias_body(rpb_ref, o_ref):
    h = pl.program_id(0)
    rows = DEC_SEQ // GRID_W
    c = lax.broadcasted_iota(I32, (GRID_W, LANE), 0)
    lane = lax.broadcasted_iota(I32, (GRID_W, LANE), 1)
    c2 = lane % GRID_W
    rel = c2 - c + (NA_KW - 1)
    win_lo = jnp.clip(c - NA_KW // 2, 0, GRID_W - NA_KW)
    col_ok = (c2 >= win_lo) & (c2 < win_lo + NA_KW)
    hits = [rel == j for j in range(NA_REL_COLS)]
    outside = jnp.full((GRID_W, LANE), NEG, F32)
    toep = []
    for dr in range(NA_REL_ROWS):
        base = (h * NA_REL_ROWS + dr) * NA_REL_COLS
        t = outside
        for j in range(NA_REL_COLS):
            t = jnp.where(hits[j], rpb_ref[base + j] * LOG2E, t)
        toep.append(jnp.where(col_ok, t, NEG))
    first_half = lane < GRID_W
    for pi, i in enumerate((0, 1, NA_TILES - 1)):
        key_row0 = NA_QROWS * min(max(i - 1, 0), NA_KSTART_MAX)
        for a in range(NA_QROWS):
            r = NA_QROWS * i + a
            rs = min(max(r - NA_KH // 2, 0), rows - NA_KH)

            def block(kr):
                r2 = key_row0 + kr
                return toep[r2 - r + NA_KH - 1] if rs <= r2 < rs + NA_KH else outside

            for s in range(NA_KTILES * NA_QROWS // 2):
                o_ref[pi, a * GRID_W:(a + 1) * GRID_W, s * LANE:(s + 1) * LANE] = jnp.where(
                    first_half, block(2 * s), block(2 * s + 1))


def _na_biasmask(rpb):
    return pl.pallas_call(
        _na_bias_body,
        grid_spec=pltpu.PrefetchScalarGridSpec(
            num_scalar_prefetch=1, grid=(NA_HEADS,), in_specs=[],
            out_specs=pl.BlockSpec((3, None, TM, NA_KTILES * TM), lambda h, s: (0, h, 0, 0))),
        out_shape=jax.ShapeDtypeStruct((3, NA_HEADS, TM, NA_KTILES * TM), F32),
        compiler_params=_params(("parallel",), VMEM_LIMIT),
        name="na_bias",
    )(rpb.reshape(-1))


DFT_ROWS = 256


def _dft_body(ac_ref, as_ref, bc_ref, bs_ref, c_ref, s_ref, st_ref, *, rb):
    h = pl.program_id(0)
    ac, sa, bc, sb = ac_ref[...], as_ref[...], bc_ref[...], bs_ref[...]
    cosv = ac * bc - sa * sb
    nsin = -(sa * bc + ac * sb)
    row = h * rb + lax.broadcasted_iota(I32, (rb, 1), 0)
    col = lax.broadcasted_iota(I32, (1, cosv.shape[1]), 1)
    c_ref[...] = cosv.astype(BF16)
    s_ref[...] = jnp.where(row == 0, jnp.where(col % 2 == 0, 1.0, -1.0), nsin).astype(BF16)
    st_ref[...] = jnp.where(col == 0, jnp.where(row % 2 == 0, 1.0, -1.0), nsin).astype(BF16)


def _dft_mats(L):
    n = 2 * L
    rb = min(L, DFT_ROWS)
    t = jnp.arange(L, dtype=I32)[None, :]
    hi = jnp.arange(L // rb, dtype=I32)[:, None]
    lo = jnp.arange(rb, dtype=I32)[:, None]
    pa = ((rb * hi * t) % n).astype(F32) * (2.0 * math.pi / n)
    pb = ((lo * t) % n).astype(F32) * (2.0 * math.pi / n)
    coarse = lambda x: x.reshape(L // rb, 1, L)
    full = lambda shape: pl.BlockSpec(shape, lambda h: (0,) * len(shape))
    return pl.pallas_call(
        functools.partial(_dft_body, rb=rb),
        grid=(L // rb,),
        in_specs=[pl.BlockSpec((None, 1, L), lambda h: (h, 0, 0))] * 2 + [full((rb, L))] * 2,
        out_specs=[pl.BlockSpec((rb, L), lambda h: (h, 0))] * 3,
        out_shape=[jax.ShapeDtypeStruct((L, L), BF16)] * 3,
        compiler_params=_params(("parallel",), VMEM_LIMIT),
        name="dft_tables",
    )(coarse(jnp.cos(pa)), coarse(jnp.sin(pa)), jnp.cos(pb), jnp.sin(pb))


def _hy_features(L):
    nn = jnp.arange(L, dtype=F32)[:, None]
    t = jnp.linspace(0.0, 1.0, L, dtype=F32)[:, None]
    bands = jnp.linspace(1e-4, HY_BANDS - 1, HY_BANDS, dtype=F32)[None, :]
    ang = (2.0 * math.pi / L) * nn * bands
    z = jnp.concatenate([t, jnp.cos(ang), -jnp.sin(ang)], axis=-1)
    z = jnp.pad(z, ((0, 0), (0, 40 - HY_EMB)))
    deltas = jnp.abs(jnp.linspace(math.log(1e-2) / 1.5, math.log(1e-2) / 0.3, HY_W, dtype=F32))
    decay = jnp.exp(-t * deltas)
    bwd = jnp.where(jnp.arange(L)[:, None] == 0, 0.0, decay)
    return z, jnp.concatenate([decay, bwd], axis=1)


def _hy_filter_body(z_ref, w1_ref, b1_ref, fr_ref, w2_ref, b2_ref, w3_ref, dec_ref, c_ref, s_ref,
                    kre_ref, kim_ref, h_scr, *, L, fb):
    f = pl.program_id(0)

    @pl.when(f == 0)
    def _():
        fr = fr_ref[...]
        h = jnp.sin(fr * (jnp.dot(z_ref[...], w1_ref[...], precision=HI, preferred_element_type=F32)
                          + b1_ref[...]))
        h = jnp.sin(fr * (jnp.dot(h, w2_ref[...], precision=HI, preferred_element_type=F32) + b2_ref[...]))
        h = jnp.dot(h, w3_ref[...], precision=HI, preferred_element_type=F32) * dec_ref[...]
        h_scr[...] = h.astype(BF16)

    hb = h_scr[...]
    a_re = jnp.dot(c_ref[...], hb, preferred_element_type=F32)
    a_im = jnp.dot(s_ref[...], hb, preferred_element_type=F32)
    grow = f * fb + lax.broadcasted_iota(I32, (fb, 1), 0)
    packed = grow == 0
    sc = jnp.where(packed, 1.0 / (2 * L), 2.0 / (2 * L))
    kre_ref[...] = (a_re[:, :HY_W] + a_re[:, HY_W:]) * sc
    kim_ref[...] = (a_im[:, :HY_W] + jnp.where(packed, 1.0, -1.0) * a_im[:, HY_W:]) * sc


def _hy_filter(L, feats, mats, w1, b1, freq, w2, b2, w3):
    z, dec = feats
    c, s, _ = mats
    fb = min(L, 512)
    full = lambda shape: pl.BlockSpec(shape, lambda f: (0,) * len(shape))
    w1p = jnp.pad(w1, ((0, 40 - HY_EMB), (0, 0)))
    return pl.pallas_call(
        functools.partial(_hy_filter_body, L=L, fb=fb),
        grid=(L // fb,),
        in_specs=[full((L, 40)), full((40, HY_FF)), full((1, HY_FF)), full((1, HY_FF)),
                  full((HY_FF, HY_FF)), full((1, HY_FF)), full((HY_FF, 2 * HY_W)), full((L, 2 * HY_W)),
                  pl.BlockSpec((fb, L), lambda f: (f, 0)), pl.BlockSpec((fb, L), lambda f: (f, 0))],
        out_specs=[pl.BlockSpec((fb, HY_W), lambda f: (f, 0))] * 2,
        out_shape=[jax.ShapeDtypeStruct((L, HY_W), F32)] * 2,
        scratch_shapes=[pltpu.VMEM((L, 2 * HY_W), BF16)],
        compiler_params=_params(("arbitrary",), VMEM_LIMIT),
        name="hy_filter",
    )(z, w1p, b1.reshape(1, -1), freq.reshape(1, -1), w2, b2.reshape(1, -1), w3, dec, c, s)


HY_FREQ_BLOCK = 512


def _hy_conv_body(v_ref, cr_ref, sr_ref, cc_ref, sc_ref, kre_ref, kim_ref, o_ref, *, fb):
    f = pl.program_id(1)
    vb = v_ref[...]
    vre = jnp.dot(cr_ref[...], vb, preferred_element_type=F32)
    vim = jnp.dot(sr_ref[...], vb, preferred_element_type=F32)
    kre = kre_ref[...]
    kim = kim_ref[...]
    grow = f * fb + lax.broadcasted_iota(I32, (fb, 1), 0)
    packed = grow == 0
    yre = vre * kre - jnp.where(packed, 0.0, vim * kim)
    yim = jnp.where(packed, vim * kim, vre * kim + vim * kre)
    part = (jnp.dot(cc_ref[...], yre.astype(BF16), preferred_element_type=F32)
            + jnp.dot(sc_ref[...], yim.astype(BF16), preferred_element_type=F32))

    @pl.when(f == 0)
    def _():
        o_ref[...] = part

    @pl.when(f > 0)
    def _():
        o_ref[...] += part


def _hy_conv(vgb, mats, kre, kim, L, nseq, row_off):
    c, s, st = mats
    fb = min(L, HY_FREQ_BLOCK)
    off = row_off // L
    return pl.pallas_call(
        functools.partial(_hy_conv_body, fb=fb),
        grid=(nseq, L // fb),
        in_specs=[pl.BlockSpec((L, HY_W), lambda b, f: (off + b, 0)),
                  pl.BlockSpec((fb, L), lambda b, f: (f, 0)),
                  pl.BlockSpec((fb, L), lambda b, f: (f, 0)),
                  pl.BlockSpec((L, fb), lambda b, f: (0, f)),
                  pl.BlockSpec((L, fb), lambda b, f: (0, f)),
                  pl.BlockSpec((fb, HY_W), lambda b, f: (f, 0)),
                  pl.BlockSpec((fb, HY_W), lambda b, f: (f, 0))],
        out_specs=pl.BlockSpec((L, HY_W), lambda b, f: (b, 0)),
        out_shape=jax.ShapeDtypeStruct((nseq * L, HY_W), F32),
        compiler_params=_params(("parallel", "arbitrary"), VMEM_LIMIT),
        name="hy_conv",
    )(vgb, c, s, c, st, kre, kim)


def _layer_norm(z, g, b):
    mu = jnp.mean(z, axis=-1, keepdims=True)
    zc = z - mu
    var = jnp.mean(zc * zc, axis=-1, keepdims=True)
    return zc * lax.rsqrt(var + LN_EPS) * g + b


def _post1_body(*refs, even):
    is_ctx = pl.program_id(0) < NCT
    if even:
        (x_ref, m_ref, convc_ref, convs_ref, vg_ref, x0_ref, skip_ref, atc_ref, ats_ref, w_ref, b_ref,
         g_ref, be_ref, rw_ref, rb_ref, x1_ref, xm_ref, lg_ref) = refs
    else:
        (x_ref, m_ref, atc_ref, ats_ref, w_ref, b_ref, g_ref, be_ref, rw_ref, rb_ref, x1_ref, xm_ref,
         lg_ref) = refs
    for r0 in range(0, TM, POST_CHAIN):
        rs = slice(r0, r0 + POST_CHAIN)
        at = jnp.where(is_ctx, atc_ref[rs, :], ats_ref[rs, :])
        if even:
            conv = jnp.where(is_ctx, convc_ref[rs, :], convs_ref[rs, :])
            yh = ((conv + vg_ref[rs, :] * skip_ref[...]) * x0_ref[rs, :]).astype(BF16)
            y = (jnp.dot(yh, w_ref[:HY_W, :], preferred_element_type=F32)
                 + jnp.dot(at, w_ref[HY_W:, :], preferred_element_type=F32))
        else:
            y = jnp.dot(at, w_ref[...], preferred_element_type=F32)
        y = y + b_ref[...]
        x1 = _layer_norm(DN_ALPHA * x_ref[rs, :] + m_ref[2:3, :] * y, g_ref[...], be_ref[...])
        x1_ref[rs, :] = x1
        xm = x1 * (1.0 + m_ref[4:5, :]) + m_ref[3:4, :]
        _store_rows(xm_ref, xm, row0=r0)
        lg_ref[:, rs] = lax.dot_general(rw_ref[...], xm, (((1,), (1,)), ((), ())), precision=HI,
                                        preferred_element_type=F32) + rb_ref[...]


def _post1(x, mods, l, parts, w_bf, b, g, be, rw_t, rb, even):
    row = lambda w: pl.BlockSpec((TM, w), lambda i: (i, 0))
    full = lambda shape: pl.BlockSpec(shape, lambda i: (0,) * len(shape))
    in_specs = [row(D), pl.BlockSpec((None, None, 6, D), lambda i: (l, _mod_row(i), 0, 0))]
    ctx = lambda w: pl.BlockSpec((TM, w), lambda i: (jnp.minimum(i, NCT - 1), 0))
    lat = lambda w: pl.BlockSpec((TM, w), lambda i: (jnp.maximum(i - NCT, 0), 0))
    if even:
        in_specs += [ctx(HY_W), lat(HY_W), row(HY_W), row(HY_W), full((1, HY_W)), ctx(NA_W), lat(NA_W)]
    else:
        in_specs += [ctx(D), lat(D)]
    in_specs += [full((D, D)), full((1, D)), full((1, D)), full((1, D)), full((N_EXPERTS, D)),
                 full((N_EXPERTS, 1))]
    return pl.pallas_call(
        functools.partial(_post1_body, even=even),
        grid=(NT,),
        in_specs=in_specs,
        out_specs=[row(D), pl.BlockSpec((TM * SUB, LANE), lambda i: (i, 0)),
                   pl.BlockSpec((None, N_EXPERTS, TM), lambda i: (i, 0, 0))],
        out_shape=[jax.ShapeDtypeStruct((T, D), F32), jax.ShapeDtypeStruct((T * SUB, LANE), F32),
                   jax.ShapeDtypeStruct((NT, N_EXPERTS, TM), F32)],
        compiler_params=_params(("parallel",), VMEM_LIMIT),
        name="post1",
    )(x, mods, *parts, w_bf, b.reshape(1, D), g.reshape(1, D), be.reshape(1, D), rw_t,
      rb.reshape(N_EXPERTS, 1))


def _route_body(lg_ref, slot_ref, gate_ref, pe_ref, pd_ref, be_ref, rb_ref, nx_ref, par_ref, eidx_scr,
                rank_scr):
    eio = lax.broadcasted_iota(I32, (N_EXPERTS, TM), 0)
    ri = lax.broadcasted_iota(I32, (TM, TM), 0)
    ci = lax.broadcasted_iota(I32, (TM, TM), 1)
    upper = jnp.where(ri < ci, 1.0, 0.0).astype(BF16)

    def tile(i, carry):
        lg = lg_ref[i]
        sel = jnp.zeros((N_EXPERTS, TM), F32)
        vals, hots = [], []
        for k in range(TOP_K):
            mx = jnp.max(lg, axis=0, keepdims=True)
            idx = jnp.min(jnp.where(lg == mx, eio, N_EXPERTS), axis=0, keepdims=True)
            hot = eio == idx
            lg = jnp.where(hot, -jnp.inf, lg)
            sel = sel + jnp.where(hot, 1.0, 0.0)
            vals.append(mx)
            hots.append(hot)
            eidx_scr[i, k:k + 1, :] = idx
        rank = jnp.dot(sel.astype(BF16), upper, preferred_element_type=F32) + carry
        ex = [jnp.exp(v - vals[0]) for v in vals]
        den = ex[0] + ex[1] + ex[2] + ex[3]
        for k in range(TOP_K):
            gate_ref[i, k:k + 1, :] = ex[k] / den
            rank_scr[i, k:k + 1, :] = jnp.sum(jnp.where(hots[k], rank, 0.0), axis=0, keepdims=True)
        return carry + jnp.sum(sel, axis=1, keepdims=True)

    counts = lax.fori_loop(0, NT, tile, jnp.zeros((N_EXPERTS, 1), F32))
    padded = jnp.floor((counts + (MOE_BM - 1)) * (1.0 / MOE_BM)) * MOE_BM
    e_r = lax.broadcasted_iota(I32, (N_EXPERTS, N_EXPERTS), 0)
    e_c = lax.broadcasted_iota(I32, (N_EXPERTS, N_EXPERTS), 1)
    incl = jnp.where(e_c <= e_r, 1.0, 0.0)
    padded_b = jnp.broadcast_to(padded, (N_EXPERTS, LANE))
    pad_end = jnp.dot(incl, padded_b, precision=HI, preferred_element_type=F32)
    pad_start = pad_end[:, 0:1] - padded
    pe_ref[...] = pad_end.astype(I32)
    pd_ref[...] = padded_b.astype(I32)
    blk0 = (lax.broadcasted_iota(I32, (1, TM), 1) * MOE_BM).astype(F32)
    nle = jnp.sum(jnp.where(pad_end[:, 0:1] <= blk0, 1.0, 0.0), axis=0, keepdims=True)
    blk_e = jnp.minimum(nle, N_EXPERTS - 1.0)
    be_ref[...] = blk_e.astype(I32)
    mine = eio.astype(F32) == blk_e
    cnt_b = jnp.sum(jnp.where(mine, counts, 0.0), axis=0, keepdims=True)
    start_b = jnp.sum(jnp.where(mine, pad_start, 0.0), axis=0, keepdims=True)
    rb_ref[...] = jnp.clip(cnt_b - (blk0 - start_b), 0.0, float(MOE_BM)).astype(I32)
    used = padded > 0.0
    e_f = eio.astype(F32)
    nx_ref[...] = jnp.min(jnp.where((e_f > blk_e) & used, e_f, float(N_EXPERTS)), axis=0,
                          keepdims=True).astype(I32)
    ordinal = jnp.sum(jnp.where((e_f < blk_e) & used, 1.0, 0.0), axis=0, keepdims=True)
    par_ref[...] = (ordinal - 2.0 * jnp.floor(ordinal * 0.5)).astype(I32)

    def place(i, c):
        for k in range(TOP_K):
            hot = eio == eidx_scr[i, k:k + 1, :]
            start = jnp.sum(jnp.where(hot, pad_start, 0.0), axis=0, keepdims=True)
            slot_ref[i, k:k + 1, :] = (start + rank_scr[i, k:k + 1, :]).astype(I32)
        return c

    lax.fori_loop(0, NT, place, 0)


def _route(logits):
    return pl.pallas_call(
        _route_body,
        out_shape=[jax.ShapeDtypeStruct((NT, TOP_K, TM), I32), jax.ShapeDtypeStruct((NT, TOP_K, TM), F32),
                   jax.ShapeDtypeStruct((N_EXPERTS, LANE), I32), jax.ShapeDtypeStruct((N_EXPERTS, LANE), I32),
                   ] + [jax.ShapeDtypeStruct((1, TM), I32)] * 4,
        scratch_shapes=[pltpu.VMEM((NT, TOP_K, TM), I32), pltpu.VMEM((NT, TOP_K, TM), F32)],
        compiler_params=_params(None, VMEM_LIMIT),
        name="route",
    )(logits)


def _dispatch_body(slot_ref, pe_ref, pd_ref, xm_ref, xb_ref, zero_scr, sem):
    i = pl.program_id(0)

    def fill_block(b):
        start = b * (MOE_BM * SUB)
        if not isinstance(b, int):
            start = pl.multiple_of(start, MOE_BM * SUB)
        return pltpu.make_async_copy(zero_scr, xb_ref.at[pl.ds(start, MOE_BM * SUB)], sem)

    @pl.when(i == 0)
    def _():
        zero_scr[...] = jnp.zeros_like(zero_scr)
        n_used = pe_ref[N_EXPERTS - 1] // MOE_BM
        for start_or_wait in (True, False):
            for e in range(N_EXPERTS):
                @pl.when(pd_ref[e] > 0)
                def _():
                    cp = fill_block(pe_ref[e] // MOE_BM - 1)
                    cp.start() if start_or_wait else cp.wait()
            for b in range(T * TOP_K // MOE_BM, N_SLOT_BLOCKS):
                @pl.when(b >= n_used)
                def _():
                    cp = fill_block(b)
                    cp.start() if start_or_wait else cp.wait()

    def row_copy(r, k):
        s = slot_ref[i * (TOP_K * TM) + k * TM + r]
        return pltpu.make_async_copy(xm_ref.at[pl.ds(pl.multiple_of(r * SUB, SUB), SUB)],
                                     xb_ref.at[pl.ds(pl.multiple_of(s * SUB, SUB), SUB)], sem)

    def issue(r, c):
        for k in range(TOP_K):
            row_copy(r, k).start(priority=k % DMA_PRIORITIES)
        return c

    def drain(r, c):
        for k in range(TOP_K):
            row_copy(r, k).wait()
        return c

    lax.fori_loop(0, TM, issue, 0, unroll=ROW_DMA_UNROLL)
    lax.fori_loop(0, TM, drain, 0, unroll=ROW_DMA_UNROLL)


def _dispatch(slot_flat, pe, pd, xm):
    return pl.pallas_call(
        _dispatch_body,
        grid_spec=pltpu.PrefetchScalarGridSpec(
            num_scalar_prefetch=3, grid=(NT,),
            in_specs=[pl.BlockSpec((TM * SUB, LANE), lambda i, s, a, b: (i, 0))],
            out_specs=pl.BlockSpec(memory_space=pl.ANY),
            scratch_shapes=[pltpu.VMEM((MOE_BM * SUB, LANE), F32), pltpu.SemaphoreType.DMA]),
        out_shape=jax.ShapeDtypeStruct((N_SLOTS * SUB, LANE), F32),
        compiler_params=_params(("arbitrary",), VMEM_LIMIT),
        name="moe_dispatch",
    )(slot_flat, pe, pd, xm)


def _expert_body(be_ref, nu_ref, rb_ref, nx_ref, par_ref, xb_ref, guw_hbm, gub_ref, dnw_hbm, dnb_ref, yb_ref,
                 gu_f32, dn_f32, gu_scr, dn_scr, sem, *, l):
    i = pl.program_id(0)
    rows = rb_ref[i]

    def weight_copies(e, slot):
        return (pltpu.make_async_copy(guw_hbm.at[l, e], gu_f32.at[slot], sem.at[0, slot]),
                pltpu.make_async_copy(dnw_hbm.at[l, e], dn_f32.at[slot], sem.at[1, slot]))

    def chain(r0):
        xb = _load_rows(xb_ref, MOE_CHAIN, row0=r0).astype(BF16)
        hgu = jnp.dot(xb, gu_scr[...], preferred_element_type=F32) + gub_ref[...]
        g = jnp.minimum(hgu[:, :D_FF], SWIGLU_LIMIT)
        lin = jnp.clip(hgu[:, D_FF:], -SWIGLU_LIMIT, SWIGLU_LIMIT)
        act = (lin + 1.0) * (g / (1.0 + jnp.exp(-SWIGLU_ALPHA * g)))
        y = jnp.dot(act.astype(BF16), dn_scr[...], preferred_element_type=F32) + dnb_ref[...]
        _store_rows(yb_ref, y, row0=r0)

    @pl.when(i >= nu_ref[0])
    def _():
        yb_ref[...] = jnp.zeros_like(yb_ref)

    @pl.when(i < nu_ref[0])
    def _():
        e = be_ref[i]
        slot = par_ref[i]

        @pl.when(i == 0)
        def _():
            for cp in weight_copies(e, slot):
                cp.start()

        @pl.when((i == 0) | (e != be_ref[jnp.maximum(i - 1, 0)]))
        def _():
            nxt = nx_ref[i]

            @pl.when(nxt < N_EXPERTS)
            def _():
                for cp in weight_copies(nxt, 1 - slot):
                    cp.start()

            for cp in weight_copies(e, slot):
                cp.wait()
            gu_scr[...] = gu_f32[slot].astype(BF16)
            dn_scr[...] = dn_f32[slot].astype(BF16)

        @pl.when(rows > MOE_CHAIN)
        def _():
            for r0 in range(0, MOE_BM, MOE_CHAIN):
                chain(r0)

        @pl.when(rows <= MOE_CHAIN)
        def _():
            chain(0)
            yb_ref[MOE_CHAIN * SUB:, :] = jnp.zeros((MOE_BM * SUB - MOE_CHAIN * SUB, LANE), F32)


def _experts(block_e, n_used, block_rows, next_e, parity, xb, l, gu_w, gu_b, dn_w, dn_b):
    def blk(i, nu):
        return jnp.minimum(i, nu[0] - 1)

    def bias(width):
        return pl.BlockSpec((None, None, 1, width), lambda i, be, nu, rb, nx, par: (l, be[blk(i, nu)], 0, 0))

    return pl.pallas_call(
        functools.partial(_expert_body, l=l),
        grid_spec=pltpu.PrefetchScalarGridSpec(
            num_scalar_prefetch=5, grid=(N_SLOT_BLOCKS,),
            in_specs=[pl.BlockSpec((MOE_BM * SUB, LANE), lambda i, be, nu, rb, nx, par: (blk(i, nu), 0)),
                      pl.BlockSpec(memory_space=pl.ANY), bias(2 * D_FF),
                      pl.BlockSpec(memory_space=pl.ANY), bias(D)],
            out_specs=pl.BlockSpec((MOE_BM * SUB, LANE), lambda i, be, nu, rb, nx, par: (i, 0)),
            scratch_shapes=[pltpu.VMEM((2, D, 2 * D_FF), F32), pltpu.VMEM((2, D_FF, D), F32),
                            pltpu.VMEM((D, 2 * D_FF), BF16), pltpu.VMEM((D_FF, D), BF16),
                            pltpu.SemaphoreType.DMA((2, 2))]),
        out_shape=jax.ShapeDtypeStruct((N_SLOTS * SUB, LANE), F32),
        compiler_params=_params(("arbitrary",), VMEM_LIMIT),
        name="moe_experts",
    )(block_e, n_used, block_rows, next_e, parity, xb, gu_w, gu_b.reshape(DEPTH, N_EXPERTS, 1, 2 * D_FF), dn_w,
      dn_b.reshape(DEPTH, N_EXPERTS, 1, D))


def _post2_body(slot_ref, x1_ref, gt_ref, m_ref, g_ref, be_ref, yb_ref, *rest, split):
    if split:
        x2c_ref, x2s_ref, y_scr, sem = rest
    else:
        x2_ref, y_scr, sem = rest
    i = pl.program_id(0)
    buf = i % 2

    def row_copy(t, r, k):
        s = slot_ref[t * (TOP_K * TM) + k * TM + r]
        return pltpu.make_async_copy(yb_ref.at[pl.ds(pl.multiple_of(s * SUB, SUB), SUB)],
                                     y_scr.at[t % 2, k, pl.ds(pl.multiple_of(r * SUB, SUB), SUB)],
                                     sem.at[t % 2])

    def fetch(t):
        def issue(r, c):
            for k in range(TOP_K):
                row_copy(t, r, k).start(priority=k % DMA_PRIORITIES)
            return c
        lax.fori_loop(0, TM, issue, 0, unroll=ROW_DMA_UNROLL)

    @pl.when(i == 0)
    def _():
        fetch(i)

    @pl.when(i + 1 < NT)
    def _():
        fetch(i + 1)

    def drain(r, c):
        for k in range(TOP_K):
            row_copy(i, r, k).wait()
        return c

    lax.fori_loop(0, TM, drain, 0, unroll=ROW_DMA_UNROLL)
    f = _load_rows(y_scr, TM, (buf, 0)) * gt_ref[:, 0:1]
    for k in range(1, TOP_K):
        f = f + _load_rows(y_scr, TM, (buf, k)) * gt_ref[:, k:k + 1]
    x2 = _layer_norm(DN_ALPHA * x1_ref[...] + m_ref[5:6, :] * f, g_ref[...], be_ref[...])
    if not split:
        x2_ref[...] = x2
    else:
        @pl.when(i < NCT)
        def _():
            x2c_ref[...] = x2
            x2s_ref[...] = jnp.zeros_like(x2s_ref)

        @pl.when(i >= NCT)
        def _():
            x2s_ref[...] = x2


def _post2(slot_flat, x1, gates_t, mods, l, g, be, yb, split=False):
    full = lambda shape: pl.BlockSpec(shape, lambda i, s: (0,) * len(shape))
    if split:
        out_specs = [pl.BlockSpec((TM, D), lambda i, s: (jnp.minimum(i, NCT - 1), 0)),
                     pl.BlockSpec((TM, D), lambda i, s: (jnp.maximum(i - NCT, 0), 0))]
        out_shape = [jax.ShapeDtypeStruct((TC, D), F32), jax.ShapeDtypeStruct((TS, D), F32)]
    else:
        out_specs = pl.BlockSpec((TM, D), lambda i, s: (i, 0))
        out_shape = jax.ShapeDtypeStruct((T, D), F32)
    return pl.pallas_call(
        functools.partial(_post2_body, split=split),
        grid_spec=pltpu.PrefetchScalarGridSpec(
            num_scalar_prefetch=1, grid=(NT,),
            in_specs=[pl.BlockSpec((TM, D), lambda i, s: (i, 0)),
                      pl.BlockSpec((TM, TOP_K), lambda i, s: (i, 0)),
                      pl.BlockSpec((None, None, 6, D), lambda i, s: (l, _mod_row(i), 0, 0)),
                      full((1, D)), full((1, D)),
                      pl.BlockSpec(memory_space=pl.ANY)],
            out_specs=out_specs,
            scratch_shapes=[pltpu.VMEM((2, TOP_K, TM * SUB, LANE), F32), pltpu.SemaphoreType.DMA((2,))]),
        out_shape=out_shape,
        compiler_params=_params(("arbitrary",), VMEM_LIMIT),
        name="post2",
    )(slot_flat, x1, gates_t, mods, g.reshape(1, D), be.reshape(1, D), yb)


def _rope_tables():
    quarter = HEAD // 4
    pos = jnp.arange(DEC_SEQ)
    rows = (pos // GRID_W).astype(F32)[:, None]
    cols = (pos % GRID_W).astype(F32)[:, None]
    lane = jnp.arange(LANE)[None, :]
    d = lane % HEAD
    inv = ROPE_THETA ** (-(d % quarter).astype(F32) / quarter)
    ang = jnp.where(d < HEAD // 2, rows, cols) * inv
    sign = jnp.where((d // quarter) % 2 == 0, -1.0, 1.0)
    cos = jnp.concatenate([jnp.ones((TC, LANE), F32), jnp.tile(jnp.cos(ang), (DEC_BATCH, 1))], axis=0)
    sin = jnp.concatenate([jnp.zeros((TC, LANE), F32), jnp.tile(sign * jnp.sin(ang), (DEC_BATCH, 1))], axis=0)
    return cos, sin


def _moe_and_norm(x1, xm, logits, mods, l, ln_g, ln_b, exp_gu_w, exp_gu_b, exp_dn_w, exp_dn_b):
    slot, gates, pe, pd, be, rb, nx, par = _route(logits)
    slot_flat = slot.reshape(-1)
    pe1 = pe[:, 0]
    xb = _dispatch(slot_flat, pe1, pd[:, 0], xm)
    n_used = pe1[N_EXPERTS - 1:] // MOE_BM
    yb = _experts(be.reshape(-1), n_used, rb.reshape(-1), nx.reshape(-1), par.reshape(-1), xb, l,
                  exp_gu_w, exp_gu_b, exp_dn_w, exp_dn_b)
    gates_t = gates.transpose(0, 2, 1).reshape(T, TOP_K)
    return _post2(slot_flat, x1, gates_t, mods, l, ln_g[l, 1], ln_b[l, 1], yb, split=l == DEPTH - 1)


def kernel(x_prompt, x_sample, cache_na_k, cache_na_v, cache_swa_k, cache_swa_v, c, c_ctx, ada_w, ada_b, ln_g, ln_b, ev_in_w, ev_in_b, hy_conv_w, hy_conv_b, hy_f_w1, hy_f_b1, hy_f_freq, hy_f_w2, hy_f_b2, hy_f_w3, hy_skip, na_rpb, od_in_w, od_in_b, swa_sink, mix_out_w, mix_out_b, router_w, router_b, exp_gu_w, exp_gu_b, exp_dn_w, exp_dn_b):
    x = jnp.concatenate([x_prompt.reshape(TC, D), x_sample.reshape(TS, D)], axis=0)
    cond8 = jnp.concatenate([c_ctx[None, :], c, jnp.zeros((8 - 1 - DEC_BATCH, D), F32)], axis=0)
    mods = _adaln(cond8, ada_w, ada_b).reshape(DEPTH, 8, 6, D)

    rope_tabs = _rope_tables()
    hy_consts = {L: (_hy_features(L), _dft_mats(L)) for L in (SEQ, DEC_SEQ)}
    ck_na = cache_na_k.reshape(DEC_BATCH, -1, PAST, NA_W)
    cv_na = cache_na_v.reshape(DEC_BATCH, -1, PAST, NA_W)
    ck_swa = cache_swa_k.reshape(DEC_BATCH, -1, PAST, SWA_KV * HEAD)
    cv_swa = cache_swa_v.reshape(DEC_BATCH, -1, PAST, SWA_KV * HEAD)

    na_k, na_v, swa_k, swa_v = [], [], [], []
    for l in range(DEPTH):
        w_out = mix_out_w[l].astype(BF16)
        rw_t = router_w[l].T
        if l % 2 == 0:
            e = l // 2
            x0, vg, vgb, q, k, v, kc, vc = _inproj_even(x, mods, l, ev_in_w[e].astype(BF16), ev_in_b[e],
                                                hy_conv_w[e].reshape(3, HY_PROJ), hy_conv_b[e].reshape(1, HY_PROJ))
            convs = []
            for L, nseq, off in ((SEQ, BATCH, 0), (DEC_SEQ, DEC_BATCH, TC)):
                feats, mats = hy_consts[L]
                kre, kim = _hy_filter(L, feats, mats, hy_f_w1[e], hy_f_b1[e], hy_f_freq[e], hy_f_w2[e],
                                      hy_f_b2[e], hy_f_w3[e])
                convs.append(_hy_conv(vgb, mats, kre, kim, L, nseq, off))
            parts = (convs[0], convs[1], vg, x0, hy_skip[e].reshape(1, HY_W), _ctx_even_attn(q, k, v),
                     _na_attn(q, k, v, ck_na, cv_na, _na_biasmask(na_rpb[e]), e))
            na_k.append(kc.reshape(BATCH, SEQ, NA_HEADS, HEAD))
            na_v.append(vc.reshape(BATCH, SEQ, NA_HEADS, HEAD))
        else:
            o = l // 2
            q, k, v, kc, vc = _inproj(x, mods, l, od_in_w[o].astype(BF16), od_in_b[o],
                              ((SWA_HEADS * HEAD, True, BF16), (SWA_KV * HEAD, True, F32),
                               (SWA_KV * HEAD, False, F32)),
                              rope_tabs)
            parts = (_ctx_odd_attn(swa_sink[o], q, k, v), _swa_attn(swa_sink[o], q, k, v, ck_swa, cv_swa, o))
            swa_k.append(kc.reshape(BATCH, SEQ, SWA_KV, HEAD))
            swa_v.append(vc.reshape(BATCH, SEQ, SWA_KV, HEAD))
        x1, xm, logits = _post1(x, mods, l, parts, w_out, mix_out_b[l], ln_g[l, 0], ln_b[l, 0], rw_t,
                                router_b[l], l % 2 == 0)
        x = _moe_and_norm(x1, xm, logits, mods, l, ln_g, ln_b, exp_gu_w, exp_gu_b, exp_dn_w, exp_dn_b)

    y_ctx, y_lat = x
    return (y_ctx.reshape(BATCH, SEQ, D), y_lat.reshape(DEC_BATCH, DEC_SEQ, D),
            jnp.stack(na_k, axis=1), jnp.stack(na_v, axis=1), jnp.stack(swa_k, axis=1), jnp.stack(swa_v, axis=1))
```
